```python
import jax, jax.numpy as jnp
from jax import lax
import numpy as np

D_MODEL = 2048
BATCH = 2
SEQ = 4096
DEPTH = 1

SSD_HEADS = 32
SSD_HEAD_DIM = 64
SSD_INNER = SSD_HEADS * SSD_HEAD_DIM
SSD_GROUPS = 4
SSD_STATE = 128
SSD_CONV = 4
SSD_CHUNK = 256
SSD_CONV_DIM = SSD_INNER + 2 * SSD_GROUPS * SSD_STATE
GLA_HEADS = 4
GLA_KEY_DIM = D_MODEL // 2
GLA_VAL_DIM = D_MODEL
GLA_HK = GLA_KEY_DIM // GLA_HEADS
GLA_HV = GLA_VAL_DIM // GLA_HEADS
GLA_GATE_RANK = 16
GLA_GATE_NORM = 16.0
GLA_CHUNK = 64
N_EXPERTS = 32
TOP_K = 4
D_EXPERT = D_MODEL
SWIGLU_LIMIT = 7.0
SWIGLU_ALPHA = 1.702
MOE_BLOCK = 128
EPS = 1e-6
IN_SIZES = (SSD_INNER, SSD_CONV_DIM, SSD_HEADS, GLA_KEY_DIM, GLA_KEY_DIM, GLA_VAL_DIM, GLA_VAL_DIM, GLA_GATE_RANK, D_MODEL, D_MODEL)
D_IN_PROJ = sum(IN_SIZES)

kernel_name = 'hybrid_ssd_gla_moe_adaln'


def split_cols(a, sizes):
    offs = np.cumsum(np.array(sizes))[:-1].tolist()
    return jnp.split(a, offs, axis=-1)


def rms_norm(x, w):
    xf = x.astype(jnp.float32)
    y = xf * lax.rsqrt(jnp.mean(xf * xf, axis=-1, keepdims=True) + EPS)
    return (y * w.astype(jnp.float32)).astype(x.dtype)


def causal_depthwise_conv(u, w, b):
    C = u.shape[-1]
    y = lax.conv_general_dilated(u, w[:, None, :].astype(u.dtype), window_strides=(1,), padding=[(SSD_CONV - 1, 0)],
                                 dimension_numbers=('NWC', 'WIO', 'NWC'), feature_group_count=C)
    return y + b.astype(u.dtype)


def ssd_chunked(X, A, Bm, Cm):
    b, L, H, P = X.shape
    Q = SSD_CHUNK
    nc = L // Q
    G, J, N = SSD_GROUPS, SSD_HEADS // SSD_GROUPS, SSD_STATE
    X = X.reshape(b, nc, Q, G, J, P)
    A = A.reshape(b, nc, Q, G, J).transpose(0, 3, 4, 1, 2)
    Bm = Bm.reshape(b, nc, Q, G, N)
    Cm = Cm.reshape(b, nc, Q, G, N)
    A_cs = jnp.cumsum(A, axis=-1)
    causal = jnp.tril(jnp.ones((Q, Q), dtype=bool))
    seg = A_cs[..., :, None] - A_cs[..., None, :]
    Lmat = jnp.exp(jnp.where(causal, seg, -jnp.inf))
    scores = jnp.einsum('bcqgn,bcsgn->bgcqs', Cm, Bm)
    y_diag = jnp.einsum('bgcqs,bgjcqs,bcsgjp->bcqgjp', scores, Lmat, X)
    decay_states = jnp.exp(A_cs[..., -1:] - A_cs)
    states = jnp.einsum('bcsgn,bgjcs,bcsgjp->bcgjpn', Bm, decay_states, X)
    chunk_decay = jnp.exp(A_cs[..., -1])

    def step(h, inp):
        s, d = inp
        return h * d[..., None, None] + s, h

    h0 = jnp.zeros((b, G, J, P, N), dtype=X.dtype)
    _, prev = lax.scan(step, h0, (states.transpose(1, 0, 2, 3, 4, 5), chunk_decay.transpose(3, 0, 1, 2)))
    prev = prev.transpose(1, 0, 2, 3, 4, 5)
    y_off = jnp.einsum('bcqgn,bcgjpn,bgjcq->bcqgjp', Cm, prev, jnp.exp(A_cs))
    return (y_diag + y_off).reshape(b, L, H, P)


def ssd_mixer(z, xbc, dt_raw, conv_w, conv_b, dt_bias, a_log, d_skip, norm_w):
    b, L, _ = xbc.shape
    xbc = jax.nn.silu(causal_depthwise_conv(xbc, conv_w, conv_b)).astype(jnp.float32)
    xs, Bm, Cm = split_cols(xbc, (SSD_INNER, SSD_GROUPS * SSD_STATE, SSD_GROUPS * SSD_STATE))
    xs = xs.reshape(b, L, SSD_HEADS, SSD_HEAD_DIM)
    Bm = Bm.reshape(b, L, SSD_GROUPS, SSD_STATE)
    Cm = Cm.reshape(b, L, SSD_GROUPS, SSD_STATE)
    dt = jax.nn.softplus(dt_raw.astype(jnp.float32) + dt_bias.astype(jnp.float32))
    A = -jnp.exp(a_log.astype(jnp.float32))
    y = ssd_chunked(xs * dt[..., None], dt * A, Bm, Cm)
    y = y + xs * d_skip.astype(jnp.float32)[:, None]
    y = y.reshape(b, L, SSD_INNER) * jax.nn.silu(z.astype(jnp.float32))
    gs = SSD_INNER // SSD_GROUPS
    y = rms_norm(y.reshape(b, L, SSD_GROUPS, gs), norm_w.reshape(SSD_GROUPS, gs))
    return y.reshape(b, L, SSD_INNER).astype(z.dtype)


def gla_mixer(q, k, v, g_out, gk_low, w_gk2, b_gk2, norm_w):
    b, L, _ = q.shape
    Q = GLA_CHUNK
    nc = L // Q
    H, DK, DV = GLA_HEADS, GLA_HK, GLA_HV
    f32 = jnp.float32
    q = q.astype(f32).reshape(b, nc, Q, H, DK) * (DK ** -0.5)
    k = k.astype(f32).reshape(b, nc, Q, H, DK)
    v = v.astype(f32).reshape(b, nc, Q, H, DV)
    log_a = jax.nn.log_sigmoid((gk_low @ w_gk2 + b_gk2).astype(f32)) / GLA_GATE_NORM
    bcs = jnp.cumsum(log_a.reshape(b, nc, Q, H, DK), axis=2)
    last = bcs[:, :, -1:]
    q_dec = q * jnp.exp(bcs)
    k_inv = k * jnp.exp(-bcs)
    k_state = k * jnp.exp(last - bcs)
    causal = jnp.tril(jnp.ones((Q, Q), dtype=bool))
    attn = jnp.einsum('bcqhd,bcshd->bchqs', q_dec, k_inv)
    attn = jnp.where(causal, attn, 0.0)
    o_intra = jnp.einsum('bchqs,bcshv->bcqhv', attn, v)
    kv = jnp.einsum('bcshd,bcshv->bchdv', k_state, v)
    chunk_decay = jnp.exp(last[:, :, 0])

    def step(S, inp):
        s, d = inp
        return S * d[..., None] + s, S

    S0 = jnp.zeros((b, H, DK, DV), dtype=f32)
    _, prev = lax.scan(step, S0, (kv.transpose(1, 0, 2, 3, 4), chunk_decay.transpose(1, 0, 2, 3)))
    prev = prev.transpose(1, 0, 2, 3, 4)
    o_inter = jnp.einsum('bcqhd,bchdv->bcqhv', q_dec, prev)
    o = (o_intra + o_inter).reshape(b, L, H, DV)
    o = rms_norm(o, norm_w) * jax.nn.silu(g_out.astype(f32).reshape(b, L, H, DV))
    return o.reshape(b, L, GLA_VAL_DIM).astype(g_out.dtype)


def moe_ffn(x2d, w_router, b_router, w_gate_up, b_gate_up, w_down, b_down):
    T = x2d.shape[0]
    logits = (x2d @ w_router + b_router).astype(jnp.float32)
    top_vals, top_idx = lax.top_k(logits, TOP_K)
    weights = jax.nn.softmax(top_vals, axis=-1)
    A = T * TOP_K
    flat_e = top_idx.reshape(-1)
    order = jnp.argsort(flat_e)
    sorted_e = flat_e[order]
    counts = jnp.bincount(flat_e, length=N_EXPERTS)
    start = jnp.cumsum(counts) - counts
    padded = (counts + MOE_BLOCK - 1) // MOE_BLOCK * MOE_BLOCK
    pend = jnp.cumsum(padded)
    pstart = pend - padded
    dest_sorted = pstart[sorted_e] + (jnp.arange(A, dtype=jnp.int32) - start[sorted_e])
    n_blocks = -(-A // MOE_BLOCK) + N_EXPERTS
    buf_tok = jnp.zeros((n_blocks * MOE_BLOCK,), jnp.int32).at[dest_sorted].set((order // TOP_K).astype(jnp.int32))
    block_expert = jnp.minimum(jnp.searchsorted(pend, jnp.arange(n_blocks) * MOE_BLOCK, side='right'), N_EXPERTS - 1)

    def block_fn(args):
        tok, e = args
        xb = x2d[tok]
        gu = xb @ w_gate_up[e] + b_gate_up[e]
        gate, up = gu[:, ::2], gu[:, 1::2]
        gate = jnp.minimum(gate, SWIGLU_LIMIT)
        up = jnp.clip(up, -SWIGLU_LIMIT, SWIGLU_LIMIT)
        h = gate * jax.nn.sigmoid(gate * SWIGLU_ALPHA) * (up + 1.0)
        return h @ w_down[e] + b_down[e]

    out = lax.map(block_fn, (buf_tok.reshape(n_blocks, MOE_BLOCK), block_expert)).reshape(-1, x2d.shape[-1])
    dest = jnp.zeros((A,), jnp.int32).at[order].set(dest_sorted.astype(jnp.int32)).reshape(T, TOP_K)
    return jnp.einsum('tk,tkd->td', weights.astype(out.dtype), out[dest])


def setup_inputs(seed: int = 0) -> dict:
    key = jax.random.key(seed)
    ks = jax.random.split(key, 32)
    f32 = jnp.float32
    nrm = lambda k, shape, s: jax.random.normal(k, shape, f32) * s
    D = D_MODEL
    u = jax.random.uniform(ks[8], (DEPTH, SSD_HEADS), f32)
    dt0 = jnp.maximum(jnp.exp(u * (np.log(0.1) - np.log(0.001)) + np.log(0.001)), 1e-4)
    return {
        'x': nrm(ks[0], (BATCH, SEQ, D), 1.0),
        'c': nrm(ks[1], (BATCH, D), 1.0),
        'w_ada': nrm(ks[2], (DEPTH, D, 6 * D), D ** -0.5),
        'b_ada': nrm(ks[3], (DEPTH, 6 * D), 0.02),
        'norm1_w': 1.0 + nrm(ks[4], (DEPTH, D), 0.02),
        'w_in': nrm(ks[5], (DEPTH, D, D_IN_PROJ), D ** -0.5),
        'ssd_conv_w': nrm(ks[6], (DEPTH, SSD_CONV, SSD_CONV_DIM), SSD_CONV ** -0.5),
        'ssd_conv_b': nrm(ks[7], (DEPTH, SSD_CONV_DIM), 0.02),
        'ssd_dt_bias': dt0 + jnp.log(-jnp.expm1(-dt0)),
        'ssd_a_log': jnp.log(jax.random.uniform(ks[9], (DEPTH, SSD_HEADS), f32, 1.0, 16.0)),
        'ssd_d_skip': 1.0 + nrm(ks[10], (DEPTH, SSD_HEADS), 0.02),
        'ssd_norm_w': 1.0 + nrm(ks[11], (DEPTH, SSD_INNER), 0.02),
        'gla_w_gk2': nrm(ks[12], (DEPTH, GLA_GATE_RANK, GLA_KEY_DIM), GLA_GATE_RANK ** -0.5),
        'gla_b_gk2': nrm(ks[13], (DEPTH, GLA_KEY_DIM), 0.02),
        'gla_norm_w': 1.0 + nrm(ks[14], (DEPTH, GLA_HV), 0.02),
        'w_ssd_out': nrm(ks[15], (DEPTH, SSD_INNER, D), SSD_INNER ** -0.5),
        'w_gla_out': nrm(ks[16], (DEPTH, GLA_VAL_DIM, D), GLA_VAL_DIM ** -0.5),
        'w_out': nrm(ks[17], (DEPTH, D, D), D ** -0.5),
        'norm2_w': 1.0 + nrm(ks[18], (DEPTH, D), 0.02),
        'w_router': nrm(ks[19], (DEPTH, D, N_EXPERTS), D ** -0.5),
        'b_router': nrm(ks[20], (DEPTH, N_EXPERTS), 0.01),
        'w_gate_up': nrm(ks[21], (DEPTH, N_EXPERTS, D, 2 * D_EXPERT), D ** -0.5),
        'b_gate_up': nrm(ks[22], (DEPTH, N_EXPERTS, 2 * D_EXPERT), 0.02),
        'w_down': nrm(ks[23], (DEPTH, N_EXPERTS, D_EXPERT, D), D_EXPERT ** -0.5),
        'b_down': nrm(ks[24], (DEPTH, N_EXPERTS, D), 0.02),
        'final_norm_w': 1.0 + nrm(ks[25], (D,), 0.02),
    }


def reference(x, c, w_ada, b_ada, norm1_w, w_in, ssd_conv_w, ssd_conv_b, ssd_dt_bias, ssd_a_log, ssd_d_skip, ssd_norm_w,
              gla_w_gk2, gla_b_gk2, gla_norm_w, w_ssd_out, w_gla_out, w_out, norm2_w, w_router, b_router,
              w_gate_up, b_gate_up, w_down, b_down, final_norm_w):
    b, L, D = x.shape
    h = x
    c_act = jax.nn.silu(c)
    for i in range(DEPTH):
        mod = c_act @ w_ada[i] + b_ada[i]
        sh1, sc1, g1, sh2, sc2, g2 = jnp.split(mod[:, None, :], 6, axis=-1)
        n = rms_norm(h, norm1_w[i]) * (1.0 + sc1) + sh1
        proj = n @ w_in[i]
        z, xbc, dt_raw, q, k, v, g_out, gk_low, gate_s, gate_g = split_cols(proj, IN_SIZES)
        y_ssd = ssd_mixer(z, xbc, dt_raw, ssd_conv_w[i], ssd_conv_b[i], ssd_dt_bias[i], ssd_a_log[i], ssd_d_skip[i], ssd_norm_w[i])
        y_gla = gla_mixer(q, k, v, g_out, gk_low, gla_w_gk2[i], gla_b_gk2[i], gla_norm_w[i])
        merged = jax.nn.sigmoid(gate_s) * (y_ssd @ w_ssd_out[i]) + jax.nn.sigmoid(gate_g) * (y_gla @ w_gla_out[i])
        h = h + g1 * (merged @ w_out[i])
        n2 = rms_norm(h, norm2_w[i]) * (1.0 + sc2) + sh2
        ffn = moe_ffn(n2.reshape(b * L, D), w_router[i], b_router[i], w_gate_up[i], b_gate_up[i], w_down[i], b_down[i])
        h = h + g2 * ffn.reshape(b, L, D)
    return rms_norm(h, final_norm_w)
```

```python
import functools

import jax
import jax.numpy as jnp
import numpy as np
from jax import lax
from jax.experimental import pallas as pl
from jax.experimental.pallas import tpu as pltpu

F32 = jnp.float32
BF16 = jnp.bfloat16
U32 = jnp.uint32
I32 = jnp.int32

EPS = 1e-6
LANES = 128
VMEM_LIMIT = 56 * 1024 * 1024

D_MODEL = 2048
SSD_HEADS, SSD_P, SSD_G, SSD_N, SSD_CONV, SSD_Q = 32, 64, 4, 128, 4, 256
SSD_INNER = SSD_HEADS * SSD_P
GLA_H, GLA_DK, GLA_DV, GLA_RANK, GLA_C = 4, 256, 512, 16, 64
GLA_GATE_NORM = 16.0
N_EXPERTS, TOP_K = 32, 4
SWIGLU_LIMIT, SWIGLU_ALPHA = 7.0, 1.702

COL_Z, COL_XS, COL_V, COL_GO, COL_GS, COL_GG = 0, 2048, 4096, 6144, 8192, 10240
COL_BC, COL_Q, COL_K, COL_DT, COL_GK = 12288, 13312, 14336, 15360, 15488
PROJ_TN = 512
PROJ_COLS = 15872

MOE_SUB = 256
MOE_TM = 2048
MOE_TH = 256


def _cparams(sem, vmem=VMEM_LIMIT):
    return pltpu.CompilerParams(dimension_semantics=sem, vmem_limit_bytes=vmem)


def _split3(x):
    hi = x.astype(BF16)
    r1 = x - hi.astype(F32)
    mid = r1.astype(BF16)
    lo = (r1 - mid.astype(F32)).astype(BF16)
    return hi, mid, lo


def _dot(a, b):
    return jnp.dot(a, b, preferred_element_type=F32)


def _dot_nt(a, b):
    return lax.dot_general(a, b, (((1,), (1,)), ((), ())), preferred_element_type=F32)


def _dot_tn(a, b):
    return lax.dot_general(a, b, (((0,), (0,)), ((), ())), preferred_element_type=F32)


def _dot_exact_lhs(a01, x):
    hi, mid, lo = _split3(x)
    return _dot(a01, hi) + _dot(a01, mid) + _dot(a01, lo)


def _dot_exact_rhs(x, b01):
    hi, mid, lo = _split3(x)
    return _dot(hi, b01) + _dot(mid, b01) + _dot(lo, b01)


def _silu(x):
    return x * jax.nn.sigmoid(x)


def _softplus(x):
    return jnp.maximum(x, 0.0) + jnp.log1p(jnp.exp(-jnp.abs(x)))


def _ada_body(ct_ref, w_ref, b_ref, o_ref):
    ct = ct_ref[...]
    cs = _silu(ct)
    w = w_ref[...]
    for m in range(ct.shape[1]):
        o_ref[m:m + 1, :] = jnp.sum(w * cs[:, m:m + 1], axis=0, keepdims=True) + b_ref[...]


def _ada(c, w_ada, b_ada):
    B, D = c.shape
    N = w_ada.shape[1]
    tn = 1024
    return pl.pallas_call(
        _ada_body,
        out_shape=jax.ShapeDtypeStruct((B, N), F32),
        grid=(N // tn,),
        in_specs=[pl.BlockSpec((D, B), lambda j: (0, 0)),
                  pl.BlockSpec((D, tn), lambda j: (0, j)),
                  pl.BlockSpec((1, tn), lambda j: (0, j))],
        out_specs=pl.BlockSpec((B, tn), lambda j: (0, j)),
        compiler_params=_cparams(("arbitrary",)),
        name="ada",
    )(c.T, w_ada, b_ada.reshape(1, N))


def _inproj_body(x_ref, nw_ref, sc_ref, sh_ref, w_ref, o_ref, n_scr):
    @pl.when(pl.program_id(1) == 0)
    def _():
        x = x_ref[...]
        ms = jnp.mean(x * x, axis=-1, keepdims=True)
        y = x * lax.rsqrt(ms + EPS) * nw_ref[...]
        n_scr[...] = (y * (1.0 + sc_ref[0]) + sh_ref[0]).astype(BF16)

    o_ref[...] = _dot(n_scr[...], w_ref[...]).astype(BF16)


def _inproj(x2d, norm_w, sc, sh, w_cat, L):
    T, D = x2d.shape
    NP = w_cat.shape[1]
    tm, tn = 1024, PROJ_TN
    tiles_per_batch = L // tm
    return pl.pallas_call(
        _inproj_body,
        out_shape=jax.ShapeDtypeStruct((T, NP), BF16),
        grid=(T // tm, NP // tn),
        in_specs=[pl.BlockSpec((tm, D), lambda i, j: (i, 0)),
                  pl.BlockSpec((1, D), lambda i, j: (0, 0)),
                  pl.BlockSpec((1, 1, D), lambda i, j: (i // tiles_per_batch, 0, 0)),
                  pl.BlockSpec((1, 1, D), lambda i, j: (i // tiles_per_batch, 0, 0)),
                  pl.BlockSpec((D, tn), lambda i, j: (0, j))],
        out_specs=pl.BlockSpec((tm, tn), lambda i, j: (i, j)),
        scratch_shapes=[pltpu.VMEM((tm, D), BF16)],
        compiler_params=_cparams(("arbitrary", "arbitrary")),
        name="inproj",
    )(x2d, norm_w, sc, sh, w_cat)


def _build_w_cat(w_in):
    D = w_in.shape[0]
    o = np.cumsum([0, SSD_INNER, SSD_INNER + 2 * SSD_G * SSD_N, SSD_HEADS, GLA_H * GLA_DK, GLA_H * GLA_DK,
                   GLA_H * GLA_DV, GLA_H * GLA_DV, GLA_RANK, D, D]).tolist()
    z, xbc, dt, q, k, v, go, gk, gs, gg = [w_in[:, o[i]:o[i + 1]] for i in range(10)]
    pad = lambda a: jnp.pad(a, ((0, 0), (0, LANES - a.shape[1])))
    segs = [z, xbc[:, :SSD_INNER], v, go, gs, gg, xbc[:, SSD_INNER:], q, k, pad(dt), pad(gk)]
    w = jnp.concatenate(segs, axis=1)
    w = jnp.pad(w, ((0, 0), (0, PROJ_COLS - w.shape[1])))
    return w.astype(BF16)


def _ssd_body(z_ref, xs_ref, bc_ref, dt_ref, cwx_ref, cwbc_ref, cbx_ref, cbbc_ref, dtb_ref, alog_ref,
              dskip_ref, nw_ref, exp_ref, o_ref, ubx, ubbc, state):
    Q = SSD_Q
    c = pl.program_id(1)

    @pl.when(c == 0)
    def _():
        ubx[0:8, :] = jnp.zeros((8, ubx.shape[1]), F32)
        ubbc[0:8, :] = jnp.zeros((8, ubbc.shape[1]), F32)
        state[...] = jnp.zeros(state.shape, F32)

    def conv_silu(u_ref, ub, cw_ref, cb_ref):
        ub[8:8 + Q, :] = u_ref[0].astype(F32)
        acc = cb_ref[...]
        for kk in range(SSD_CONV):
            off = 8 - (SSD_CONV - 1) + kk
            acc = acc + cw_ref[kk:kk + 1, :] * ub[off:off + Q, :]
        ub[0:8, :] = ub[Q:Q + 8, :]
        return _silu(acc)

    xs = conv_silu(xs_ref, ubx, cwx_ref, cbx_ref)
    bcm = conv_silu(bc_ref, ubbc, cwbc_ref, cbbc_ref)
    GN = SSD_G * SSD_N
    bm_f, cm = bcm[:, :GN], bcm[:, GN:].astype(BF16)

    dt = _softplus(dt_ref[0].astype(F32) + dtb_ref[...])
    a_neg = -jnp.exp(alog_ref[...])
    dA = dt * a_neg

    row = lax.broadcasted_iota(I32, (Q, Q), 0)
    col = lax.broadcasted_iota(I32, (Q, Q), 1)
    causal = row >= col
    tri = jnp.where(causal, 1.0, 0.0).astype(BF16)
    acs = _dot_exact_lhs(tri, dA)
    acs_t = acs.T
    acs_last = acs[Q - 1:Q, :]
    exp_a = jnp.exp(acs)
    decay_st = jnp.exp(acs_last - acs)

    expand = exp_ref[...]
    dt_e = _dot_exact_rhs(dt, expand)
    dtd_e = _dot_exact_rhs(dt * decay_st, expand)
    expa_e = _dot_exact_rhs(exp_a, expand)

    x_dt = (xs * dt_e).astype(BF16)
    x_dd = (xs * dtd_e).astype(BF16)

    lane = lax.broadcasted_iota(I32, (Q, LANES), 1)
    lo_half = lane < SSD_P
    HG = SSD_HEADS // SSD_G
    GW = HG * SSD_P
    y_groups = []
    for g in range(SSD_G):
        cg = cm[:, g * SSD_N:(g + 1) * SSD_N]
        bg_f = bm_f[:, g * SSD_N:(g + 1) * SSD_N]
        bg = bg_f.astype(BF16)
        scores = _dot_nt(cg, bg)
        pieces = []
        for p in range(HG // 2):
            h0 = g * HG + 2 * p
            xp = x_dt[:, h0 * SSD_P:h0 * SSD_P + LANES]
            acc = None
            for s in range(2):
                h = h0 + s
                seg = acs[:, h:h + 1] - acs_t[h:h + 1, :]
                lmat = jnp.exp(jnp.where(causal, seg, -jnp.inf))
                m = (scores * lmat).astype(BF16)
                xh = jnp.where(lo_half if s == 0 else jnp.logical_not(lo_half), xp, jnp.zeros_like(xp))
                part = _dot(m, xh)
                acc = part if acc is None else acc + part
            pieces.append(acc)
        y_diag = jnp.concatenate(pieces, axis=1)
        st_old = state[g]
        y_off = _dot(cg, st_old.astype(BF16)) * expa_e[:, g * GW:(g + 1) * GW]
        st_new = _dot(bg_f.T.astype(BF16), x_dd[:, g * GW:(g + 1) * GW])
        state[g] = st_old * expa_e[Q - 1:Q, g * GW:(g + 1) * GW] + st_new
        y_groups.append(y_diag + y_off)

    y = jnp.concatenate(y_groups, axis=1) + xs * dskip_ref[...]
    y = y * _silu(z_ref[0].astype(F32))
    outs = []
    for g in range(SSD_G):
        yg = y[:, g * GW:(g + 1) * GW]
        ms = jnp.mean(yg * yg, axis=-1, keepdims=True)
        outs.append(yg * lax.rsqrt(ms + EPS) * nw_ref[:, g * GW:(g + 1) * GW])
    o_ref[0] = jnp.concatenate(outs, axis=1).astype(BF16)


def _ssd(proj3, conv_w, conv_b, dt_bias, a_log, d_skip, norm_w):
    B, L, _ = proj3.shape
    Q, DI, BCW = SSD_Q, SSD_INNER, 2 * SSD_G * SSD_N
    padl = lambda a: jnp.pad(a.reshape(1, -1), ((0, 0), (0, LANES - a.shape[-1])))
    expand = (np.arange(LANES)[:, None] == (np.arange(DI)[None, :] // SSD_P)).astype(np.float32)
    const = lambda shape: pl.BlockSpec(shape, lambda b, c: (0,) * len(shape))
    return pl.pallas_call(
        _ssd_body,
        out_shape=jax.ShapeDtypeStruct((B, L, DI), BF16),
        grid=(B, L // Q),
        in_specs=[pl.BlockSpec((1, Q, DI), lambda b, c: (b, c, COL_Z // DI)),
                  pl.BlockSpec((1, Q, DI), lambda b, c: (b, c, COL_XS // DI)),
                  pl.BlockSpec((1, Q, BCW), lambda b, c: (b, c, COL_BC // BCW)),
                  pl.BlockSpec((1, Q, LANES), lambda b, c: (b, c, COL_DT // LANES)),
                  const((SSD_CONV, DI)), const((SSD_CONV, BCW)), const((1, DI)), const((1, BCW)),
                  const((1, LANES)), const((1, LANES)), const((1, DI)), const((1, DI)),
                  const((LANES, DI))],
        out_specs=pl.BlockSpec((1, Q, DI), lambda b, c: (b, c, 0)),
        scratch_shapes=[pltpu.VMEM((Q + 8, DI), F32), pltpu.VMEM((Q + 8, BCW), F32),
                        pltpu.VMEM((SSD_G, SSD_N, DI // SSD_G), F32)],
        compiler_params=_cparams(("arbitrary", "arbitrary")),
        name="ssd",
    )(proj3, proj3, proj3, proj3, conv_w[:, :DI], conv_w[:, DI:], conv_b[:DI].reshape(1, DI),
      conv_b[DI:].reshape(1, BCW), padl(dt_bias), padl(a_log), jnp.repeat(d_skip, SSD_P).reshape(1, DI),
      norm_w.reshape(1, DI), jnp.asarray(expand, BF16))


GLA_RB = 256


def _gla_body(q_ref, k_ref, v_ref, g_ref, gk_ref, w_ref, b_ref, nw_ref, o_ref, st):
    RB, C = GLA_RB, GLA_C

    @pl.when(pl.program_id(1) == 0)
    def _():
        st[...] = jnp.zeros(st.shape, F32)

    w_hi, w_mid, w_lo = _split3(w_ref[...])
    gk = gk_ref[0]
    pre = _dot(gk, w_hi) + _dot(gk, w_mid) + _dot(gk, w_lo) + b_ref[...]
    log_a = (jnp.minimum(pre, 0.0) - jnp.log1p(jnp.exp(-jnp.abs(pre)))) * (1.0 / GLA_GATE_NORM)

    row = lax.broadcasted_iota(I32, (RB, RB), 0)
    col = lax.broadcasted_iota(I32, (RB, RB), 1)
    blocktri = jnp.where((row // C == col // C) & (row >= col), 1.0, 0.0).astype(BF16)
    bcs_all = _dot_exact_lhs(blocktri, log_a)

    r64 = lax.broadcasted_iota(I32, (C, C), 0)
    c64 = lax.broadcasted_iota(I32, (C, C), 1)
    causal = r64 >= c64
    scale = GLA_DK ** -0.5

    for s in range(RB // C):
        rs = slice(s * C, (s + 1) * C)
        bcs = bcs_all[rs, :]
        last = bcs[C - 1:C, :]
        qf = q_ref[0, rs, :].astype(F32)
        kf = k_ref[0, rs, :].astype(F32)
        q_dec = (qf * scale * jnp.exp(bcs)).astype(BF16)
        k_inv = (kf * jnp.exp(-bcs)).astype(BF16)
        k_st = (kf * jnp.exp(last - bcs)).astype(BF16)
        cdec = jnp.exp(last)
        for h in range(GLA_H):
            ks = slice(h * GLA_DK, (h + 1) * GLA_DK)
            vs = slice(h * GLA_DV, (h + 1) * GLA_DV)
            vh = v_ref[0, rs, vs]
            attn = jnp.where(causal, _dot_nt(q_dec[:, ks], k_inv[:, ks]), 0.0)
            st_h = st[h]
            o = _dot(attn.astype(BF16), vh) + _dot_nt(q_dec[:, ks], st_h.astype(BF16))
            st[h] = st_h * cdec[:, ks] + _dot_tn(vh, k_st[:, ks])
            ms = jnp.mean(o * o, axis=-1, keepdims=True)
            o = o * lax.rsqrt(ms + EPS) * nw_ref[...]
            o_ref[0, rs, vs] = (o * _silu(g_ref[0, rs, vs].astype(F32))).astype(BF16)


def _gla(proj3, w_gk2, b_gk2, norm_w):
    B, L, _ = proj3.shape
    RB, KD, VD = GLA_RB, GLA_H * GLA_DK, GLA_H * GLA_DV
    w_pad = jnp.pad(w_gk2, ((0, LANES - w_gk2.shape[0]), (0, 0)))
    const = lambda shape: pl.BlockSpec(shape, lambda b, c: (0,) * len(shape))
    return pl.pallas_call(
        _gla_body,
        out_shape=jax.ShapeDtypeStruct((B, L, VD), BF16),
        grid=(B, L // RB),
        in_specs=[pl.BlockSpec((1, RB, KD), lambda b, c: (b, c, COL_Q // KD)),
                  pl.BlockSpec((1, RB, KD), lambda b, c: (b, c, COL_K // KD)),
                  pl.BlockSpec((1, RB, VD), lambda b, c: (b, c, COL_V // VD)),
                  pl.BlockSpec((1, RB, VD), lambda b, c: (b, c, COL_GO // VD)),
                  pl.BlockSpec((1, RB, LANES), lambda b, c: (b, c, COL_GK // LANES)),
                  const((LANES, KD)), const((1, KD)), const((1, GLA_DV))],
        out_specs=pl.BlockSpec((1, RB, VD), lambda b, c: (b, c, 0)),
        scratch_shapes=[pltpu.VMEM((GLA_H, GLA_DV, GLA_DK), F32)],
        compiler_params=_cparams(("arbitrary", "arbitrary")),
        name="gla",
    )(proj3, proj3, proj3, proj3, proj3, w_pad, b_gk2.reshape(1, KD), norm_w.reshape(1, GLA_DV))


def _merge_body(ys_ref, yg_ref, gs_ref, gg_ref, ws_ref, wg_ref, o_ref):
    a = _dot(ys_ref[...], ws_ref[...])
    b = _dot(yg_ref[...], wg_ref[...])
    m = jax.nn.sigmoid(gs_ref[...].astype(F32)) * a + jax.nn.sigmoid(gg_ref[...].astype(F32)) * b
    o_ref[...] = m.astype(BF16)


def _merge(y_ssd, y_gla, proj, w_ssd_out, w_gla_out):
    T, D = y_ssd.shape
    tm, tn = 512, 1024
    return pl.pallas_call(
        _merge_body,
        out_shape=jax.ShapeDtypeStruct((T, D), BF16),
        grid=(D // tn, T // tm),
        in_specs=[pl.BlockSpec((tm, D), lambda j, i: (i, 0)),
                  pl.BlockSpec((tm, D), lambda j, i: (i, 0)),
                  pl.BlockSpec((tm, tn), lambda j, i: (i, COL_GS // tn + j)),
                  pl.BlockSpec((tm, tn), lambda j, i: (i, COL_GG // tn + j)),
                  pl.BlockSpec((D, tn), lambda j, i: (0, j)),
                  pl.BlockSpec((D, tn), lambda j, i: (0, j))],
        out_specs=pl.BlockSpec((tm, tn), lambda j, i: (i, j)),
        compiler_params=_cparams(("arbitrary", "arbitrary")),
        name="merge",
    )(y_ssd, y_gla, proj, proj, w_ssd_out, w_gla_out)


def _pack_bf16_pair(a, b):
    ua = lax.bitcast_convert_type(a.astype(BF16).astype(F32), U32)
    ub = lax.bitcast_convert_type(b.astype(BF16).astype(F32), U32)
    return (ua & jnp.uint32(0xFFFF0000)) | (ub >> 16)


def _unpack_bf16_pair(w):
    a = lax.bitcast_convert_type(w & jnp.uint32(0xFFFF0000), F32)
    b = lax.bitcast_convert_type(w << 16, F32)
    return a, b


def _post_body(m_ref, x_ref, g1_ref, nw_ref, sc_ref, sh_ref, wo_ref, wr_ref, br_ref,
               h_ref, n2p_ref, tw_ref, ti_ref):
    D = x_ref.shape[1]
    h = x_ref[...] + g1_ref[0] * _dot(m_ref[...], wo_ref[...])
    h_ref[...] = h
    ms = jnp.mean(h * h, axis=-1, keepdims=True)
    n2 = h * lax.rsqrt(ms + EPS) * nw_ref[...] * (1.0 + sc_ref[0]) + sh_ref[0]
    n2p_ref[...] = _pack_bf16_pair(n2[:, :D // 2], n2[:, D // 2:])

    n_hi = n2.astype(BF16)
    n_lo = (n2 - n_hi.astype(F32)).astype(BF16)
    wr = wr_ref[...]
    w_hi = wr.astype(BF16)
    w_lo = (wr - w_hi.astype(F32)).astype(BF16)
    logits = _dot(n_hi, w_hi) + _dot(n_hi, w_lo) + _dot(n_lo, w_hi) + br_ref[...]

    lane = lax.broadcasted_iota(I32, logits.shape, 1)
    cur = jnp.where(lane < N_EXPERTS, logits, -jnp.inf)
    vals, idxs = [], []
    for _ in range(TOP_K):
        mx = jnp.max(cur, axis=-1, keepdims=True)
        ix = jnp.min(jnp.where(cur == mx, lane, LANES), axis=-1, keepdims=True)
        vals.append(mx)
        idxs.append(ix)
        cur = jnp.where(lane == ix, -jnp.inf, cur)
    es = [jnp.exp(v - vals[0]) for v in vals]
    denom = es[0] + es[1] + es[2] + es[3]
    tw = jnp.zeros(logits.shape, F32)
    ti = jnp.zeros(logits.shape, I32)
    for kk in range(TOP_K):
        tw = jnp.where(lane == kk, es[kk] / denom, tw)
        ti = jnp.where(lane == kk, idxs[kk], ti)
    tw_ref[...] = tw
    ti_ref[...] = ti


def _post(merged, x2d, g1, norm_w, sc, sh, w_out, w_router, b_router, L):
    T, D = x2d.shape
    tm = 512
    tpb = L // tm
    wr = jnp.pad(w_router, ((0, 0), (0, LANES - w_router.shape[1])))
    br = jnp.pad(b_router.reshape(1, -1), ((0, 0), (0, LANES - b_router.shape[0])))
    row = lambda w: pl.BlockSpec((tm, w), lambda i: (i, 0))
    per_b = pl.BlockSpec((1, 1, D), lambda i: (i // tpb, 0, 0))
    const = lambda shape: pl.BlockSpec(shape, lambda i: (0,) * len(shape))
    return pl.pallas_call(
        _post_body,
        out_shape=(jax.ShapeDtypeStruct((T, D), F32), jax.ShapeDtypeStruct((T, D // 2), U32),
                   jax.ShapeDtypeStruct((T, LANES), F32), jax.ShapeDtypeStruct((T, LANES), I32)),
        grid=(T // tm,),
        in_specs=[row(D), row(D), per_b, const((1, D)), per_b, per_b, const((D, D)), const((D, LANES)),
                  const((1, LANES))],
        out_specs=(row(D), row(D // 2), row(LANES), row(LANES)),
        compiler_params=_cparams(("arbitrary",)),
        name="post",
    )(merged, x2d, g1, norm_w, sc, sh, w_out, wr, br)


def _routing(top_idx, n_rows, n_sb):
    T = top_idx.shape[0]
    A = T * TOP_K
    flat_e = top_idx.reshape(A)
    onehot = (flat_e[:, None] == jnp.arange(N_EXPERTS, dtype=I32)[None, :]).astype(I32)
    csum = jnp.cumsum(onehot, axis=0)
    rank = jnp.sum(onehot * csum, axis=1) - 1
    counts = csum[-1]
    padded = (counts + MOE_SUB - 1) // MOE_SUB * MOE_SUB
    pend = jnp.cumsum(padded)
    pstart = pend - padded
    dest = (pstart[flat_e] + rank).astype(I32)
    buf_tok = jnp.zeros((n_rows,), I32).at[dest].set(jnp.arange(A, dtype=I32) // TOP_K)

    nsb = (padded + MOE_TM - 1) // MOE_TM
    sb_end = jnp.cumsum(nsb)
    total = sb_end[-1]
    i = jnp.arange(n_sb, dtype=I32)
    e_of = jnp.minimum(jnp.searchsorted(sb_end, i, side="right"), N_EXPERTS - 1).astype(I32)
    local = i - (sb_end[e_of] - nsb[e_of])
    valid = i < total
    last_e = e_of[jnp.maximum(total - 1, 0)]
    sb_e = jnp.where(valid, e_of, last_e).astype(I32)
    sb_start = jnp.where(valid, pstart[e_of] + local * MOE_TM, 0).astype(I32)
    sb_start = jnp.concatenate([sb_start, pend[-1:].astype(I32)])
    sb_n = jnp.where(valid, jnp.clip(padded[e_of] - local * MOE_TM, 0, MOE_TM), 0).astype(I32)
    return dest, buf_tok, sb_e, sb_start, sb_n


GATHER_ROWS = 512


def _gather_body(tok_ref, src_hbm, dst_hbm, sem):
    base = pl.program_id(0) * GATHER_ROWS

    def copy(r):
        return pltpu.make_async_copy(src_hbm.at[pl.ds(tok_ref[0, 0, r], 1)], dst_hbm.at[pl.ds(base + r, 1)], sem)

    def start(r, carry):
        copy(r).start()
        return carry

    def wait(r, carry):
        copy(r).wait()
        return carry

    lax.fori_loop(0, GATHER_ROWS, start, 0, unroll=8)
    lax.fori_loop(0, GATHER_ROWS, wait, 0, unroll=8)


def _gather(n2p, buf_tok):
    n_rows = buf_tok.shape[0]
    steps = n_rows // GATHER_ROWS
    return pl.pallas_call(
        _gather_body,
        out_shape=jax.ShapeDtypeStruct((n_rows, n2p.shape[1]), n2p.dtype),
        grid=(steps,),
        in_specs=[pl.BlockSpec((1, 1, GATHER_ROWS), lambda i: (i, 0, 0), memory_space=pltpu.SMEM),
                  pl.BlockSpec(memory_space=pl.ANY)],
        out_specs=pl.BlockSpec(memory_space=pl.ANY),
        scratch_shapes=[pltpu.SemaphoreType.DMA(())],
        compiler_params=_cparams(("arbitrary",)),
        name="gather",
    )(buf_tok.reshape(steps, 1, GATHER_ROWS), n2p)


def _deinterleave(gu):
    rows, two_w = gu.shape
    lane = lax.broadcasted_iota(I32, (rows, LANES), 1)
    idx_e = (2 * lane) % LANES
    idx_o = idx_e + 1
    first = lane < LANES // 2
    gates, ups = [], []
    for p in range(two_w // (2 * LANES)):
        a = gu[:, (2 * p) * LANES:(2 * p + 1) * LANES]
        b = gu[:, (2 * p + 1) * LANES:(2 * p + 2) * LANES]
        gates.append(jnp.where(first, jnp.take_along_axis(a, idx_e, axis=1), jnp.take_along_axis(b, idx_e, axis=1)))
        ups.append(jnp.where(first, jnp.take_along_axis(a, idx_o, axis=1), jnp.take_along_axis(b, idx_o, axis=1)))
    return jnp.concatenate(gates, axis=1), jnp.concatenate(ups, axis=1)


def _moe_body(sbe_ref, sbs_ref, sbn_ref, xs_hbm, wgu_ref, bgu_ref, wd_ref, bd_ref, y_hbm,
              xs_buf, xb_buf, acc, ystage, sem_in, sem_out):
    i = pl.program_id(0)
    hc = pl.program_id(1)
    n_sb = pl.num_programs(0)
    n_hc = pl.num_programs(1)
    SUB = MOE_SUB
    half = xs_buf.shape[1]
    nblk = sbn_ref[i] // SUB
    start_row = sbs_ref[i]

    def in_copy(row0, r):
        off = pl.multiple_of(r * SUB, SUB)
        return pltpu.make_async_copy(xs_hbm.at[pl.ds(pl.multiple_of(row0 + off, SUB), SUB)],
                                     xs_buf.at[pl.ds(off, SUB)], sem_in)

    def start_in(sb):
        row0 = sbs_ref[sb]

        def body(r, carry):
            in_copy(row0, r).start()
            return carry

        lax.fori_loop(0, sbn_ref[sb] // SUB, body, 0)

    @pl.when(hc == 0)
    def _():
        @pl.when(i == 0)
        def _():
            start_in(0)

        def wait_in(r, carry):
            in_copy(start_row, r).wait()
            return carry

        lax.fori_loop(0, nblk, wait_in, 0)

        def convert(r, carry):
            rows = pl.ds(pl.multiple_of(r * SUB, SUB), SUB)
            a, b = _unpack_bf16_pair(xs_buf[rows, :])
            xb_buf[rows, 0:half] = a.astype(BF16)
            xb_buf[rows, half:2 * half] = b.astype(BF16)
            return carry

        lax.fori_loop(0, nblk, convert, 0)

        @pl.when(i + 1 < n_sb)
        def _():
            start_in(i + 1)

    wg = wgu_ref[0].astype(BF16)
    wd = wd_ref[0].astype(BF16)
    bgu = bgu_ref[0]

    def contrib(r):
        rows = pl.ds(pl.multiple_of(r * SUB, SUB), SUB)
        gu = _dot(xb_buf[rows, :], wg) + bgu
        gate, up = _deinterleave(gu)
        gate = jnp.minimum(gate, SWIGLU_LIMIT)
        up = jnp.clip(up, -SWIGLU_LIMIT, SWIGLU_LIMIT)
        hidden = gate * jax.nn.sigmoid(gate * SWIGLU_ALPHA) * (up + 1.0)
        return rows, _dot(hidden.astype(BF16), wd)

    @pl.when(hc == 0)
    def _():
        def body(r, carry):
            rows, v = contrib(r)
            acc[rows, :] = v + bd_ref[0]
            return carry

        lax.fori_loop(0, nblk, body, 0)

    @pl.when(hc != 0)
    def _():
        def body(r, carry):
            rows, v = contrib(r)
            acc[rows, :] += v
            return carry

        lax.fori_loop(0, nblk, body, 0)

    @pl.when(hc == n_hc - 1)
    def _():
        def out_copy(r, slot):
            dst = pl.multiple_of(start_row + r * SUB, SUB)
            return pltpu.make_async_copy(ystage.at[slot], y_hbm.at[pl.ds(dst, SUB)], sem_out.at[slot])

        def body(r, carry):
            slot = r % 2

            @pl.when(r >= 2)
            def _():
                out_copy(r - 2, slot).wait()

            rows = pl.ds(pl.multiple_of(r * SUB, SUB), SUB)
            v = acc[rows, :]
            ystage[slot] = _pack_bf16_pair(v[:, :half], v[:, half:])
            out_copy(r, slot).start()
            return carry

        lax.fori_loop(0, nblk, body, 0)

        def drain(r, carry):
            out_copy(r, r % 2).wait()
            return carry

        lax.fori_loop(jnp.maximum(nblk - 2, 0), nblk, drain, 0)

    @pl.when((i == n_sb - 1) & (hc == n_hc - 1))
    def _():
        ystage[0] = jnp.zeros(ystage.shape[1:], U32)

        def tail_copy(b):
            return pltpu.make_async_copy(ystage.at[0], y_hbm.at[pl.ds(pl.multiple_of(b * SUB, SUB), SUB)], sem_out.at[0])

        def start(b, carry):
            tail_copy(b).start()
            return carry

        def wait(b, carry):
            tail_copy(b).wait()
            return carry

        first, stop = sbs_ref[n_sb] // SUB, y_hbm.shape[0] // SUB
        lax.fori_loop(first, stop, start, 0)
        lax.fori_loop(first, stop, wait, 0)


def _moe(xs_sorted, sb_e, sb_start, sb_n, w_gate_up, b_gate_up, w_down, b_down):
    n_rows, half = xs_sorted.shape
    E, D, H2 = w_gate_up.shape
    n_sb = sb_e.shape[0]
    TH = MOE_TH
    n_hc = (H2 // 2) // TH
    hc_eff = lambda i, hc, sbn: jnp.where(sbn[i] > 0, hc, n_hc - 1)
    grid_spec = pltpu.PrefetchScalarGridSpec(
        num_scalar_prefetch=3,
        grid=(n_sb, n_hc),
        in_specs=[pl.BlockSpec(memory_space=pl.ANY),
                  pl.BlockSpec((1, D, 2 * TH), lambda i, hc, sbe, sbs, sbn: (sbe[i], 0, hc_eff(i, hc, sbn))),
                  pl.BlockSpec((1, 1, 2 * TH), lambda i, hc, sbe, sbs, sbn: (sbe[i], 0, hc_eff(i, hc, sbn))),
                  pl.BlockSpec((1, TH, D), lambda i, hc, sbe, sbs, sbn: (sbe[i], hc_eff(i, hc, sbn), 0)),
                  pl.BlockSpec((1, 1, D), lambda i, hc, sbe, sbs, sbn: (sbe[i], 0, 0))],
        out_specs=pl.BlockSpec(memory_space=pl.ANY),
        scratch_shapes=[pltpu.VMEM((MOE_TM, half), U32), pltpu.VMEM((MOE_TM, 2 * half), BF16),
                        pltpu.VMEM((MOE_TM, D), F32), pltpu.VMEM((2, MOE_SUB, half), U32),
                        pltpu.SemaphoreType.DMA(()), pltpu.SemaphoreType.DMA((2,))],
    )
    return pl.pallas_call(
        _moe_body,
        out_shape=jax.ShapeDtypeStruct((n_rows, half), U32),
        grid_spec=grid_spec,
        compiler_params=_cparams(("arbitrary", "arbitrary")),
        name="moe",
    )(sb_e, sb_start, sb_n, xs_sorted, w_gate_up, b_gate_up.reshape(E, 1, H2), w_down, b_down.reshape(E, 1, D))


COMBINE_TM = 256


def _combine_body(dest_ref, h_ref, tw_ref, g2_ref, nw_ref, y_hbm, o_ref, ybuf, sem):
    tm = COMBINE_TM
    n = TOP_K * tm

    def copy(j):
        return pltpu.make_async_copy(y_hbm.at[pl.ds(dest_ref[0, 0, j], 1)], ybuf.at[pl.ds(j, 1)], sem)

    def start(j, carry):
        copy(j).start()
        return carry

    def wait(j, carry):
        copy(j).wait()
        return carry

    lax.fori_loop(0, n, start, 0, unroll=8)
    lax.fori_loop(0, n, wait, 0, unroll=8)

    tw = tw_ref[...]
    lo = hi = None
    for kk in range(TOP_K):
        a, b = _unpack_bf16_pair(ybuf[kk * tm:(kk + 1) * tm, :])
        wk = tw[:, kk:kk + 1]
        lo = wk * a if lo is None else lo + wk * a
        hi = wk * b if hi is None else hi + wk * b
    ffn = jnp.concatenate([lo, hi], axis=1)
    h = h_ref[...] + g2_ref[0] * ffn
    ms = jnp.mean(h * h, axis=-1, keepdims=True)
    o_ref[...] = h * lax.rsqrt(ms + EPS) * nw_ref[...]


def _combine(h1, y_sorted, dest, top_w, g2, final_norm_w, L):
    T, D = h1.shape
    tm = COMBINE_TM
    tiles = T // tm
    tpb = L // tm
    dest_tiles = dest.reshape(tiles, tm, TOP_K).transpose(0, 2, 1).reshape(tiles, 1, TOP_K * tm)
    return pl.pallas_call(
        _combine_body,
        out_shape=jax.ShapeDtypeStruct((T, D), F32),
        grid=(tiles,),
        in_specs=[pl.BlockSpec((1, 1, TOP_K * tm), lambda i: (i, 0, 0), memory_space=pltpu.SMEM),
                  pl.BlockSpec((tm, D), lambda i: (i, 0)),
                  pl.BlockSpec((tm, LANES), lambda i: (i, 0)),
                  pl.BlockSpec((1, 1, D), lambda i: (i // tpb, 0, 0)),
                  pl.BlockSpec((1, D), lambda i: (0, 0)),
                  pl.BlockSpec(memory_space=pl.ANY)],
        out_specs=pl.BlockSpec((tm, D), lambda i: (i, 0)),
        scratch_shapes=[pltpu.VMEM((TOP_K * tm, D // 2), U32), pltpu.SemaphoreType.DMA(())],
        compiler_params=_cparams(("arbitrary",)),
        name="combine",
    )(dest_tiles, h1, top_w, g2, final_norm_w.reshape(1, D), y_sorted)


def kernel(x, c, w_ada, b_ada, norm1_w, w_in, ssd_conv_w, ssd_conv_b, ssd_dt_bias, ssd_a_log, ssd_d_skip, ssd_norm_w, gla_w_gk2, gla_b_gk2, gla_norm_w, w_ssd_out, w_gla_out, w_out, norm2_w, w_router, b_router, w_gate_up, b_gate_up, w_down, b_down, final_norm_w):
    B, L, D = x.shape
    T = B * L
    assert D == D_MODEL and w_ada.shape[0] == 1
    x2d = x.reshape(T, D)

    mod = _ada(c, w_ada[0], b_ada[0])
    sh1, sc1, g1, sh2, sc2, g2 = [mod[:, i * D:(i + 1) * D].reshape(B, 1, D) for i in range(6)]

    proj = _inproj(x2d, norm1_w[0].reshape(1, D), sc1, sh1, _build_w_cat(w_in[0]), L)
    proj3 = proj.reshape(B, L, PROJ_COLS)
    y_ssd = _ssd(proj3, ssd_conv_w[0], ssd_conv_b[0], ssd_dt_bias[0], ssd_a_log[0], ssd_d_skip[0], ssd_norm_w[0])
    y_gla = _gla(proj3, gla_w_gk2[0], gla_b_gk2[0], gla_norm_w[0])
    merged = _merge(y_ssd.reshape(T, D), y_gla.reshape(T, D), proj, w_ssd_out[0].astype(BF16), w_gla_out[0].astype(BF16))
    h1, n2p, top_w, top_i = _post(merged, x2d, g1, norm2_w[0].reshape(1, D), sc2, sh2, w_out[0].astype(BF16),
                                  w_router[0], b_router[0], L)

    A = T * TOP_K
    n_rows = A + N_EXPERTS * MOE_SUB
    n_sb = N_EXPERTS + n_rows // MOE_TM
    dest, buf_tok, sb_e, sb_start, sb_n = _routing(top_i[:, :TOP_K], n_rows, n_sb)
    xs_sorted = _gather(n2p, buf_tok)
    y_sorted = _moe(xs_sorted, sb_e, sb_start, sb_n, w_gate_up[0], b_gate_up[0], w_down[0], b_down[0])
    out = _combine(h1, y_sorted, dest, top_w, g2, final_norm_w, L)
    return out.reshape(B, L, D)
```

```python
import functools

import jax
import jax.numpy as jnp
import numpy as np
from jax import lax
from jax.experimental import pallas as pl
from jax.experimental.pallas import tpu as pltpu

F32 = jnp.float32
BF16 = jnp.bfloat16
U32 = jnp.uint32
I32 = jnp.int32

EPS = 1e-6
LANES = 128
VMEM_LIMIT = 56 * 1024 * 1024

D_MODEL = 2048
SSD_HEADS, SSD_P, SSD_G, SSD_N, SSD_CONV, SSD_Q = 32, 64, 4, 128, 4, 256
SSD_INNER = SSD_HEADS * SSD_P
GLA_H, GLA_DK, GLA_DV, GLA_RANK, GLA_C = 4, 256, 512, 16, 64
GLA_GATE_NORM = 16.0
N_EXPERTS, TOP_K = 32, 4
SWIGLU_LIMIT, SWIGLU_ALPHA = 7.0, 1.702

COL_Z, COL_XS, COL_V, COL_GO, COL_GS, COL_GG = 0, 2048, 4096, 6144, 8192, 10240
COL_BC, COL_Q, COL_K, COL_DT, COL_GK = 12288, 13312, 14336, 15360, 15488
PROJ_TN = 512
PROJ_COLS = 15872

MOE_SUB = 256
MOE_TM = 2048
MOE_TH = 256


def _cparams(sem, vmem=VMEM_LIMIT):
    return pltpu.CompilerParams(dimension_semantics=sem, vmem_limit_bytes=vmem)


def _split3(x):
    hi = x.astype(BF16)
    r1 = x - hi.astype(F32)
    mid = r1.astype(BF16)
    lo = (r1 - mid.astype(F32)).astype(BF16)
    return hi, mid, lo


def _dot(a, b):
    return jnp.dot(a, b, preferred_element_type=F32)


def _dot_nt(a, b):
    return lax.dot_general(a, b, (((1,), (1,)), ((), ())), preferred_element_type=F32)


def _dot_tn(a, b):
    return lax.dot_general(a, b, (((0,), (0,)), ((), ())), preferred_element_type=F32)


def _dot_exact_lhs(a01, x):
    hi, mid, lo = _split3(x)
    return _dot(a01, hi) + _dot(a01, mid) + _dot(a01, lo)


def _dot_exact_rhs(x, b01):
    hi, mid, lo = _split3(x)
    return _dot(hi, b01) + _dot(mid, b01) + _dot(lo, b01)


def _silu(x):
    return x * jax.nn.sigmoid(x)


def _softplus(x):
    return jnp.maximum(x, 0.0) + jnp.log1p(jnp.exp(-jnp.abs(x)))


def _ada_body(ct_ref, w_ref, b_ref, o_ref):
    ct = ct_ref[...]
    cs = _silu(ct)
    w = w_ref[...]
    for m in range(ct.shape[1]):
        o_ref[m:m + 1, :] = jnp.sum(w * cs[:, m:m + 1], axis=0, keepdims=True) + b_ref[...]


def _ada(c, w_ada, b_ada):
    B, D = c.shape
    N = w_ada.shape[1]
    tn = 1024
    return pl.pallas_call(
        _ada_body,
        out_shape=jax.ShapeDtypeStruct((B, N), F32),
        grid=(N // tn,),
        in_specs=[pl.BlockSpec((D, B), lambda j: (0, 0)),
                  pl.BlockSpec((D, tn), lambda j: (0, j)),
                  pl.BlockSpec((1, tn), lambda j: (0, j))],
        out_specs=pl.BlockSpec((B, tn), lambda j: (0, j)),
        compiler_params=_cparams(("arbitrary",)),
        name="ada",
    )(c.T, w_ada, b_ada.reshape(1, N))


def _inproj_body(x_ref, nw_ref, sc_ref, sh_ref, w_ref, o_ref, n_scr):
    @pl.when(pl.program_id(1) == 0)
    def _():
        x = x_ref[...]
        ms = jnp.mean(x * x, axis=-1, keepdims=True)
        y = x * lax.rsqrt(ms + EPS) * nw_ref[...]
        n_scr[...] = (y * (1.0 + sc_ref[0]) + sh_ref[0]).astype(BF16)

    o_ref[...] = _dot(n_scr[...], w_ref[...]).astype(BF16)


def _inproj(x2d, norm_w, sc, sh, w_cat, L):
    T, D = x2d.shape
    NP = w_cat.shape[1]
    tm, tn = 1024, PROJ_TN
    tiles_per_batch = L // tm
    return pl.pallas_call(
        _inproj_body,
        out_shape=jax.ShapeDtypeStruct((T, NP), BF16),
        grid=(T // tm, NP // tn),
        in_specs=[pl.BlockSpec((tm, D), lambda i, j: (i, 0)),
                  pl.BlockSpec((1, D), lambda i, j: (0, 0)),
                  pl.BlockSpec((1, 1, D), lambda i, j: (i // tiles_per_batch, 0, 0)),
                  pl.BlockSpec((1, 1, D), lambda i, j: (i // tiles_per_batch, 0, 0)),
                  pl.BlockSpec((D, tn), lambda i, j: (0, j))],
        out_specs=pl.BlockSpec((tm, tn), lambda i, j: (i, j)),
        scratch_shapes=[pltpu.VMEM((tm, D), BF16)],
        compiler_params=_cparams(("arbitrary", "arbitrary")),
        name="inproj",
    )(x2d, norm_w, sc, sh, w_cat)


def _build_w_cat(w_in):
    D = w_in.shape[0]
    o = np.cumsum([0, SSD_INNER, SSD_INNER + 2 * SSD_G * SSD_N, SSD_HEADS, GLA_H * GLA_DK, GLA_H * GLA_DK,
                   GLA_H * GLA_DV, GLA_H * GLA_DV, GLA_RANK, D, D]).tolist()
    z, xbc, dt, q, k, v, go, gk, gs, gg = [w_in[:, o[i]:o[i + 1]] for i in range(10)]
    pad = lambda a: jnp.pad(a, ((0, 0), (0, LANES - a.shape[1])))
    segs = [z, xbc[:, :SSD_INNER], v, go, gs, gg, xbc[:, SSD_INNER:], q, k, pad(dt), pad(gk)]
    w = jnp.concatenate(segs, axis=1)
    w = jnp.pad(w, ((0, 0), (0, PROJ_COLS - w.shape[1])))
    return w.astype(BF16)


def _ssd_body(z_ref, xs_ref, bc_ref, dt_ref, cwx_ref, cwbc_ref, cbx_ref, cbbc_ref, dtb_ref, alog_ref,
              dskip_ref, nw_ref, exp_ref, o_ref, ubx, ubbc, state):
    Q = SSD_Q
    c = pl.program_id(1)

    @pl.when(c == 0)
    def _():
        ubx[0:8, :] = jnp.zeros((8, ubx.shape[1]), F32)
        ubbc[0:8, :] = jnp.zeros((8, ubbc.shape[1]), F32)
        state[...] = jnp.zeros(state.shape, F32)

    def conv_silu(u_ref, ub, cw_ref, cb_ref):
        ub[8:8 + Q, :] = u_ref[0].astype(F32)
        acc = cb_ref[...]
        for kk in range(SSD_CONV):
            off = 8 - (SSD_CONV - 1) + kk
            acc = acc + cw_ref[kk:kk + 1, :] * ub[off:off + Q, :]
        ub[0:8, :] = ub[Q:Q + 8, :]
        return _silu(acc)

    xs = conv_silu(xs_ref, ubx, cwx_ref, cbx_ref)
    bcm = conv_silu(bc_ref, ubbc, cwbc_ref, cbbc_ref)
    GN = SSD_G * SSD_N
    bm_f, cm = bcm[:, :GN], bcm[:, GN:].astype(BF16)

    dt = _softplus(dt_ref[0].astype(F32) + dtb_ref[...])
    a_neg = -jnp.exp(alog_ref[...])
    dA = dt * a_neg

    row = lax.broadcasted_iota(I32, (Q, Q), 0)
    col = lax.broadcasted_iota(I32, (Q, Q), 1)
    causal = row >= col
    tri = jnp.where(causal, 1.0, 0.0).astype(BF16)
    acs = _dot_exact_lhs(tri, dA)
    acs_t = acs.T
    acs_last = acs[Q - 1:Q, :]
    exp_a = jnp.exp(acs)
    decay_st = jnp.exp(acs_last - acs)

    expand = exp_ref[...]
    dt_e = _dot_exact_rhs(dt, expand)
    dtd_e = _dot_exact_rhs(dt * decay_st, expand)
    expa_e = _dot_exact_rhs(exp_a, expand)

    x_dt = (xs * dt_e).astype(BF16)
    x_dd = (xs * dtd_e).astype(BF16)

    lane = lax.broadcasted_iota(I32, (Q, LANES), 1)
    lo_half = lane < SSD_P
    HG = SSD_HEADS // SSD_G
    GW = HG * SSD_P
    y_groups = []
    for g in range(SSD_G):
        cg = cm[:, g * SSD_N:(g + 1) * SSD_N]
        bg_f = bm_f[:, g * SSD_N:(g + 1) * SSD_N]
        bg = bg_f.astype(BF16)
        scores = _dot_nt(cg, bg)
        pieces = []
        for p in range(HG // 2):
            h0 = g * HG + 2 * p
            xp = x_dt[:, h0 * SSD_P:h0 * SSD_P + LANES]
            acc = None
            for s in range(2):
                h = h0 + s
                seg = acs[:, h:h + 1] - acs_t[h:h + 1, :]
                lmat = jnp.exp(jnp.where(causal, seg, -jnp.inf))
                m = (scores * lmat).astype(BF16)
                xh = jnp.where(lo_half if s == 0 else jnp.logical_not(lo_half), xp, jnp.zeros_like(xp))
                part = _dot(m, xh)
                acc = part if acc is None else acc + part
            pieces.append(acc)
        y_diag = jnp.concatenate(pieces, axis=1)
        st_old = state[g]
        y_off = _dot(cg, st_old.astype(BF16)) * expa_e[:, g * GW:(g + 1) * GW]
        st_new = _dot(bg_f.T.astype(BF16), x_dd[:, g * GW:(g + 1) * GW])
        state[g] = st_old * expa_e[Q - 1:Q, g * GW:(g + 1) * GW] + st_new
        y_groups.append(y_diag + y_off)

    y = jnp.concatenate(y_groups, axis=1) + xs * dskip_ref[...]
    y = y * _silu(z_ref[0].astype(F32))
    outs = []
    for g in range(SSD_G):
        yg = y[:, g * GW:(g + 1) * GW]
        ms = jnp.mean(yg * yg, axis=-1, keepdims=True)
        outs.append(yg * lax.rsqrt(ms + EPS) * nw_ref[:, g * GW:(g + 1) * GW])
    o_ref[0] = jnp.concatenate(outs, axis=1).astype(BF16)


def _ssd(proj3, conv_w, conv_b, dt_bias, a_log, d_skip, norm_w):
    B, L, _ = proj3.shape
    Q, DI, BCW = SSD_Q, SSD_INNER, 2 * SSD_G * SSD_N
    padl = lambda a: jnp.pad(a.reshape(1, -1), ((0, 0), (0, LANES - a.shape[-1])))
    expand = (np.arange(LANES)[:, None] == (np.arange(DI)[None, :] // SSD_P)).astype(np.float32)
    const = lambda shape: pl.BlockSpec(shape, lambda b, c: (0,) * len(shape))
    return pl.pallas_call(
        _ssd_body,
        out_shape=jax.ShapeDtypeStruct((B, L, DI), BF16),
        grid=(B, L // Q),
        in_specs=[pl.BlockSpec((1, Q, DI), lambda b, c: (b, c, COL_Z // DI)),
                  pl.BlockSpec((1, Q, DI), lambda b, c: (b, c, COL_XS // DI)),
                  pl.BlockSpec((1, Q, BCW), lambda b, c: (b, c, COL_BC // BCW)),
                  pl.BlockSpec((1, Q, LANES), lambda b, c: (b, c, COL_DT // LANES)),
                  const((SSD_CONV, DI)), const((SSD_CONV, BCW)), const((1, DI)), const((1, BCW)),
                  const((1, LANES)), const((1, LANES)), const((1, DI)), const((1, DI)),
                  const((LANES, DI))],
        out_specs=pl.BlockSpec((1, Q, DI), lambda b, c: (b, c, 0)),
        scratch_shapes=[pltpu.VMEM((Q + 8, DI), F32), pltpu.VMEM((Q + 8, BCW), F32),
                        pltpu.VMEM((SSD_G, SSD_N, DI // SSD_G), F32)],
        compiler_params=_cparams(("arbitrary", "arbitrary")),
        name="ssd",
    )(proj3, proj3, proj3, proj3, conv_w[:, :DI], conv_w[:, DI:], conv_b[:DI].reshape(1, DI),
      conv_b[DI:].reshape(1, BCW), padl(dt_bias), padl(a_log), jnp.repeat(d_skip, SSD_P).reshape(1, DI),
      norm_w.reshape(1, DI), jnp.asarray(expand, BF16))


GLA_RB = 256


def _gla_body(q_ref, k_ref, v_ref, g_ref, gk_ref, w_ref, b_ref, nw_ref, o_ref, st):
    RB, C = GLA_RB, GLA_C

    @pl.when(pl.program_id(1) == 0)
    def _():
        st[...] = jnp.zeros(st.shape, F32)

    w_hi, w_mid, w_lo = _split3(w_ref[...])
    gk = gk_ref[0]
    pre = _dot(gk, w_hi) + _dot(gk, w_mid) + _dot(gk, w_lo) + b_ref[...]
    log_a = (jnp.minimum(pre, 0.0) - jnp.log1p(jnp.exp(-jnp.abs(pre)))) * (1.0 / GLA_GATE_NORM)

    row = lax.broadcasted_iota(I32, (RB, RB), 0)
    col = lax.broadcasted_iota(I32, (RB, RB), 1)
    blocktri = jnp.where((row // C == col // C) & (row >= col), 1.0, 0.0).astype(BF16)
    bcs_all = _dot_exact_lhs(blocktri, log_a)

    r64 = lax.broadcasted_iota(I32, (C, C), 0)
    c64 = lax.broadcasted_iota(I32, (C, C), 1)
    causal = r64 >= c64
    scale = GLA_DK ** -0.5

    for s in range(RB // C):
        rs = slice(s * C, (s + 1) * C)
        bcs = bcs_all[rs, :]
        last = bcs[C - 1:C, :]
        qf = q_ref[0, rs, :].astype(F32)
        kf = k_ref[0, rs, :].astype(F32)
        q_dec = (qf * scale * jnp.exp(bcs)).astype(BF16)
        k_inv = (kf * jnp.exp(-bcs)).astype(BF16)
        k_st = (kf * jnp.exp(last - bcs)).astype(BF16)
        cdec = jnp.exp(last)
        for h in range(GLA_H):
            ks = slice(h * GLA_DK, (h + 1) * GLA_DK)
            vs = slice(h * GLA_DV, (h + 1) * GLA_DV)
            vh = v_ref[0, rs, vs]
            attn = jnp.where(causal, _dot_nt(q_dec[:, ks], k_inv[:, ks]), 0.0)
            st_h = st[h]
            o = _dot(attn.astype(BF16), vh) + _dot_nt(q_dec[:, ks], st_h.astype(BF16))
            st[h] = st_h * cdec[:, ks] + _dot_tn(vh, k_st[:, ks])
            ms = jnp.mean(o * o, axis=-1, keepdims=True)
            o = o * lax.rsqrt(ms + EPS) * nw_ref[...]
            o_ref[0, rs, vs] = (o * _silu(g_ref[0, rs, vs].astype(F32))).astype(BF16)


def _gla(proj3, w_gk2, b_gk2, norm_w):
    B, L, _ = proj3.shape
    RB, KD, VD = GLA_RB, GLA_H * GLA_DK, GLA_H * GLA_DV
    w_pad = jnp.pad(w_gk2, ((0, LANES - w_gk2.shape[0]), (0, 0)))
    const = lambda shape: pl.BlockSpec(shape, lambda b, c: (0,) * len(shape))
    return pl.pallas_call(
        _gla_body,
        out_shape=jax.ShapeDtypeStruct((B, L, VD), BF16),
        grid=(B, L // RB),
        in_specs=[pl.BlockSpec((1, RB, KD), lambda b, c: (b, c, COL_Q // KD)),
                  pl.BlockSpec((1, RB, KD), lambda b, c: (b, c, COL_K // KD)),
                  pl.BlockSpec((1, RB, VD), lambda b, c: (b, c, COL_V // VD)),
                  pl.BlockSpec((1, RB, VD), lambda b, c: (b, c, COL_GO // VD)),
                  pl.BlockSpec((1, RB, LANES), lambda b, c: (b, c, COL_GK // LANES)),
                  const((LANES, KD)), const((1, KD)), const((1, GLA_DV))],
        out_specs=pl.BlockSpec((1, RB, VD), lambda b, c: (b, c, 0)),
        scratch_shapes=[pltpu.VMEM((GLA_H, GLA_DV, GLA_DK), F32)],
        compiler_params=_cparams(("arbitrary", "arbitrary")),
        name="gla",
    )(proj3, proj3, proj3, proj3, proj3, w_pad, b_gk2.reshape(1, KD), norm_w.reshape(1, GLA_DV))


def _merge_body(ys_ref, yg_ref, gs_ref, gg_ref, ws_ref, wg_ref, o_ref):
    a = _dot(ys_ref[...], ws_ref[...])
    b = _dot(yg_ref[...], wg_ref[...])
    m = jax.nn.sigmoid(gs_ref[...].astype(F32)) * a + jax.nn.sigmoid(gg_ref[...].astype(F32)) * b
    o_ref[...] = m.astype(BF16)


def _merge(y_ssd, y_gla, proj, w_ssd_out, w_gla_out):
    T, D = y_ssd.shape
    tm, tn = 512, 1024
    return pl.pallas_call(
        _merge_body,
        out_shape=jax.ShapeDtypeStruct((T, D), BF16),
        grid=(D // tn, T // tm),
        in_specs=[pl.BlockSpec((tm, D), lambda j, i: (i, 0)),
                  pl.BlockSpec((tm, D), lambda j, i: (i, 0)),
                  pl.BlockSpec((tm, tn), lambda j, i: (i, COL_GS // tn + j)),
                  pl.BlockSpec((tm, tn), lambda j, i: (i, COL_GG // tn + j)),
                  pl.BlockSpec((D, tn), lambda j, i: (0, j)),
                  pl.BlockSpec((D, tn), lambda j, i: (0, j))],
        out_specs=pl.BlockSpec((tm, tn), lambda j, i: (i, j)),
        compiler_params=_cparams(("arbitrary", "arbitrary")),
        name="merge",
    )(y_ssd, y_gla, proj, proj, w_ssd_out, w_gla_out)


def _pack_bf16_pair(a, b):
    ua = lax.bitcast_convert_type(a.astype(BF16).astype(F32), U32)
    ub = lax.bitcast_convert_type(b.astype(BF16).astype(F32), U32)
    return (ua & jnp.uint32(0xFFFF0000)) | (ub >> 16)


def _unpack_bf16_pair(w):
    a = lax.bitcast_convert_type(w & jnp.uint32(0xFFFF0000), F32)
    b = lax.bitcast_convert_type(w << 16, F32)
    return a, b


def _post_body(m_ref, x_ref, g1_ref, nw_ref, sc_ref, sh_ref, wo_ref, wr_ref, br_ref,
               h_ref, n2p_ref, tw_ref, ti_ref):
    D = x_ref.shape[1]
    h = x_ref[...] + g1_ref[0] * _dot(m_ref[...], wo_ref[...])
    h_ref[...] = h
    ms = jnp.mean(h * h, axis=-1, keepdims=True)
    n2 = h * lax.rsqrt(ms + EPS) * nw_ref[...] * (1.0 + sc_ref[0]) + sh_ref[0]
    n2p_ref[...] = _pack_bf16_pair(n2[:, :D // 2], n2[:, D // 2:])

    n_hi = n2.astype(BF16)
    n_lo = (n2 - n_hi.astype(F32)).astype(BF16)
    wr = wr_ref[...]
    w_hi = wr.astype(BF16)
    w_lo = (wr - w_hi.astype(F32)).astype(BF16)
    logits = _dot(n_hi, w_hi) + _dot(n_hi, w_lo) + _dot(n_lo, w_hi) + br_ref[...]

    lane = lax.broadcasted_iota(I32, logits.shape, 1)
    cur = jnp.where(lane < N_EXPERTS, logits, -jnp.inf)
    vals, idxs = [], []
    for _ in range(TOP_K):
        mx = jnp.max(cur, axis=-1, keepdims=True)
        ix = jnp.min(jnp.where(cur == mx, lane, LANES), axis=-1, keepdims=True)
        vals.append(mx)
        idxs.append(ix)
        cur = jnp.where(lane == ix, -jnp.inf, cur)
    es = [jnp.exp(v - vals[0]) for v in vals]
    denom = es[0] + es[1] + es[2] + es[3]
    tw = jnp.zeros(logits.shape, F32)
    ti = jnp.zeros(logits.shape, I32)
    for kk in range(TOP_K):
        tw = jnp.where(lane == kk, es[kk] / denom, tw)
        ti = jnp.where(lane == kk, idxs[kk], ti)
    tw_ref[...] = tw
    ti_ref[...] = ti


def _post(merged, x2d, g1, norm_w, sc, sh, w_out, w_router, b_router, L):
    T, D = x2d.shape
    tm = 512
    tpb = L // tm
    wr = jnp.pad(w_router, ((0, 0), (0, LANES - w_router.shape[1])))
    br = jnp.pad(b_router.reshape(1, -1), ((0, 0), (0, LANES - b_router.shape[0])))
    row = lambda w: pl.BlockSpec((tm, w), lambda i: (i, 0))
    per_b = pl.BlockSpec((1, 1, D), lambda i: (i // tpb, 0, 0))
    const = lambda shape: pl.BlockSpec(shape, lambda i: (0,) * len(shape))
    return pl.pallas_call(
        _post_body,
        out_shape=(jax.ShapeDtypeStruct((T, D), F32), jax.ShapeDtypeStruct((T, D // 2), U32),
                   jax.ShapeDtypeStruct((T, LANES), F32), jax.ShapeDtypeStruct((T, LANES), I32)),
        grid=(T // tm,),
        in_specs=[row(D), row(D), per_b, const((1, D)), per_b, per_b, const((D, D)), const((D, LANES)),
                  const((1, LANES))],
        out_specs=(row(D), row(D // 2), row(LANES), row(LANES)),
        compiler_params=_cparams(("arbitrary",)),
        name="post",
    )(merged, x2d, g1, norm_w, sc, sh, w_out, wr, br)


def _routing(top_idx, n_rows, n_sb):
    T = top_idx.shape[0]
    A = T * TOP_K
    flat_e = top_idx.reshape(A)
    onehot = (flat_e[:, None] == jnp.arange(N_EXPERTS, dtype=I32)[None, :]).astype(I32)
    csum = jnp.cumsum(onehot, axis=0)
    rank = jnp.sum(onehot * csum, axis=1) - 1
    counts = csum[-1]
    padded = (counts + MOE_SUB - 1) // MOE_SUB * MOE_SUB
    pend = jnp.cumsum(padded)
    pstart = pend - padded
    dest = (pstart[flat_e] + rank).astype(I32)
    buf_tok = jnp.zeros((n_rows,), I32).at[dest].set(jnp.arange(A, dtype=I32) // TOP_K)

    nsb = (padded + MOE_TM - 1) // MOE_TM
    sb_end = jnp.cumsum(nsb)
    total = sb_end[-1]
    i = jnp.arange(n_sb, dtype=I32)
    e_of = jnp.minimum(jnp.searchsorted(sb_end, i, side="right"), N_EXPERTS - 1).astype(I32)
    local = i - (sb_end[e_of] - nsb[e_of])
    valid = i < total
    last_e = e_of[jnp.maximum(total - 1, 0)]
    sb_e = jnp.where(valid, e_of, last_e).astype(I32)
    sb_start = jnp.where(valid, pstart[e_of] + local * MOE_TM, 0).astype(I32)
    sb_start = jnp.concatenate([sb_start, pend[-1:].astype(I32)])
    sb_n = jnp.where(valid, jnp.clip(padded[e_of] - local * MOE_TM, 0, MOE_TM), 0).astype(I32)
    return dest, buf_tok, sb_e, sb_start, sb_n


GATHER_ROWS = 512


def _gather_body(tok_ref, src_hbm, o_ref, sem):
    def copy(r):
        return pltpu.make_async_copy(src_hbm.at[pl.ds(tok_ref[0, 0, r], 1)], o_ref.at[pl.ds(r, 1)], sem)

    def start(r, carry):
        copy(r).start()
        return carry

    def wait(r, carry):
        copy(r).wait()
        return carry

    lax.fori_loop(0, GATHER_ROWS, start, 0, unroll=8)
    lax.fori_loop(0, GATHER_ROWS, wait, 0, unroll=8)


def _gather(n2p, buf_tok):
    n_rows = buf_tok.shape[0]
    steps = n_rows // GATHER_ROWS
    return pl.pallas_call(
        _gather_body,
        out_shape=jax.ShapeDtypeStruct((n_rows, n2p.shape[1]), n2p.dtype),
        grid=(steps,),
        in_specs=[pl.BlockSpec((1, 1, GATHER_ROWS), lambda i: (i, 0, 0), memory_space=pltpu.SMEM),
                  pl.BlockSpec(memory_space=pl.ANY)],
        out_specs=pl.BlockSpec((GATHER_ROWS, n2p.shape[1]), lambda i: (i, 0)),
        scratch_shapes=[pltpu.SemaphoreType.DMA(())],
        compiler_params=_cparams(("arbitrary",)),
        name="gather",
    )(buf_tok.reshape(steps, 1, GATHER_ROWS), n2p)


def _deinterleave(gu):
    rows, two_w = gu.shape
    lane = lax.broadcasted_iota(I32, (rows, LANES), 1)
    idx_e = (2 * lane) % LANES
    idx_o = idx_e + 1
    first = lane < LANES // 2
    gates, ups = [], []
    for p in range(two_w // (2 * LANES)):
        a = gu[:, (2 * p) * LANES:(2 * p + 1) * LANES]
        b = gu[:, (2 * p + 1) * LANES:(2 * p + 2) * LANES]
        gates.append(jnp.where(first, jnp.take_along_axis(a, idx_e, axis=1), jnp.take_along_axis(b, idx_e, axis=1)))
        ups.append(jnp.where(first, jnp.take_along_axis(a, idx_o, axis=1), jnp.take_along_axis(b, idx_o, axis=1)))
    return jnp.concatenate(gates, axis=1), jnp.concatenate(ups, axis=1)


def _moe_body(sbe_ref, sbs_ref, sbn_ref, xs_hbm, wgu_ref, bgu_ref, wd_ref, bd_ref, y_hbm,
              xs_buf, xb_buf, acc, ystage, wg_s, wd_s, sem_in, sem_out):
    i = pl.program_id(0)
    hc = pl.program_id(1)
    n_sb = pl.num_programs(0)
    n_hc = pl.num_programs(1)
    SUB = MOE_SUB
    half = xs_buf.shape[1]
    nblk = sbn_ref[i] // SUB
    start_row = sbs_ref[i]

    def in_copy(row0, r):
        off = pl.multiple_of(r * SUB, SUB)
        return pltpu.make_async_copy(xs_hbm.at[pl.ds(pl.multiple_of(row0 + off, SUB), SUB)],
                                     xs_buf.at[pl.ds(off, SUB)], sem_in)

    def start_in(sb):
        row0 = sbs_ref[sb]

        def body(r, carry):
            in_copy(row0, r).start()
            return carry

        lax.fori_loop(0, sbn_ref[sb] // SUB, body, 0)

    @pl.when(hc == 0)
    def _():
        @pl.when(i == 0)
        def _():
            start_in(0)

        def wait_in(r, carry):
            in_copy(start_row, r).wait()
            return carry

        lax.fori_loop(0, nblk, wait_in, 0)

        def convert(r, carry):
            rows = pl.ds(pl.multiple_of(r * SUB, SUB), SUB)
            a, b = _unpack_bf16_pair(xs_buf[rows, :])
            xb_buf[rows, 0:half] = a.astype(BF16)
            xb_buf[rows, half:2 * half] = b.astype(BF16)
            return carry

        lax.fori_loop(0, nblk, convert, 0)

        @pl.when(i + 1 < n_sb)
        def _():
            start_in(i + 1)

    def accumulate(blocks, first):
        rows = [pl.ds(pl.multiple_of(r * SUB, SUB), SUB) for r in blocks]
        gus = [_dot(xb_buf[rw, :], wg_s[...]) + bgu_ref[0] for rw in rows]
        hs = []
        for gu in gus:
            gate, up = _deinterleave(gu)
            gate = jnp.minimum(gate, SWIGLU_LIMIT)
            up = jnp.clip(up, -SWIGLU_LIMIT, SWIGLU_LIMIT)
            hs.append((gate * jax.nn.sigmoid(gate * SWIGLU_ALPHA) * (up + 1.0)).astype(BF16))
        for rw, hidden in zip(rows, hs):
            v = _dot(hidden, wd_s[...])
            if first:
                acc[rw, :] = v + bd_ref[0]
            else:
                acc[rw, :] += v

    def run(first):
        def pair(p, carry):
            accumulate([2 * p, 2 * p + 1], first)
            return carry

        lax.fori_loop(0, nblk // 2, pair, 0)

        @pl.when(nblk % 2 == 1)
        def _():
            accumulate([nblk - 1], first)

    @pl.when(nblk > 0)
    def _():
        wg_s[...] = wgu_ref[0].astype(BF16)
        wd_s[...] = wd_ref[0].astype(BF16)

        @pl.when(hc == 0)
        def _():
            run(True)

        @pl.when(hc != 0)
        def _():
            run(False)

    @pl.when(hc == n_hc - 1)
    def _():
        def out_copy(r, slot):
            dst = pl.multiple_of(start_row + r * SUB, SUB)
            return pltpu.make_async_copy(ystage.at[slot], y_hbm.at[pl.ds(dst, SUB)], sem_out.at[slot])

        def body(r, carry):
            slot = r % 2

            @pl.when(r >= 2)
            def _():
                out_copy(r - 2, slot).wait()

            rows = pl.ds(pl.multiple_of(r * SUB, SUB), SUB)
            v = acc[rows, :]
            ystage[slot] = _pack_bf16_pair(v[:, :half], v[:, half:])
            out_copy(r, slot).start()
            return carry

        lax.fori_loop(0, nblk, body, 0)

        def drain(r, carry):
            out_copy(r, r % 2).wait()
            return carry

        lax.fori_loop(jnp.maximum(nblk - 2, 0), nblk, drain, 0)

    @pl.when((i == n_sb - 1) & (hc == n_hc - 1))
    def _():
        ystage[0] = jnp.zeros(ystage.shape[1:], U32)

        def tail_copy(b):
            return pltpu.make_async_copy(ystage.at[0], y_hbm.at[pl.ds(pl.multiple_of(b * SUB, SUB), SUB)], sem_out.at[0])

        def start(b, carry):
            tail_copy(b).start()
            return carry

        def wait(b, carry):
            tail_copy(b).wait()
            return carry

        first, stop = sbs_ref[n_sb] // SUB, y_hbm.shape[0] // SUB
        lax.fori_loop(first, stop, start, 0)
        lax.fori_loop(first, stop, wait, 0)


def _moe(xs_sorted, sb_e, sb_start, sb_n, w_gate_up, b_gate_up, w_down, b_down):
    n_rows, half = xs_sorted.shape
    E, D, H2 = w_gate_up.shape
    n_sb = sb_e.shape[0]
    TH = MOE_TH
    n_hc = (H2 // 2) // TH
    hc_eff = lambda i, hc, sbn: jnp.where(sbn[i] > 0, hc, n_hc - 1)
    grid_spec = pltpu.PrefetchScalarGridSpec(
        num_scalar_prefetch=3,
        grid=(n_sb, n_hc),
        in_specs=[pl.BlockSpec(memory_space=pl.ANY),
                  pl.BlockSpec((1, D, 2 * TH), lambda i, hc, sbe, sbs, sbn: (sbe[i], 0, hc_eff(i, hc, sbn))),
                  pl.BlockSpec((1, 1, 2 * TH), lambda i, hc, sbe, sbs, sbn: (sbe[i], 0, hc_eff(i, hc, sbn))),
                  pl.BlockSpec((1, TH, D), lambda i, hc, sbe, sbs, sbn: (sbe[i], hc_eff(i, hc, sbn), 0)),
                  pl.BlockSpec((1, 1, D), lambda i, hc, sbe, sbs, sbn: (sbe[i], 0, 0))],
        out_specs=pl.BlockSpec(memory_space=pl.ANY),
        scratch_shapes=[pltpu.VMEM((MOE_TM, half), U32), pltpu.VMEM((MOE_TM, 2 * half), BF16),
                        pltpu.VMEM((MOE_TM, D), F32), pltpu.VMEM((2, MOE_SUB, half), U32),
                        pltpu.VMEM((D, 2 * TH), BF16), pltpu.VMEM((TH, D), BF16),
                        pltpu.SemaphoreType.DMA(()), pltpu.SemaphoreType.DMA((2,))],
    )
    return pl.pallas_call(
        _moe_body,
        out_shape=jax.ShapeDtypeStruct((n_rows, half), U32),
        grid_spec=grid_spec,
        compiler_params=_cparams(("arbitrary", "arbitrary")),
        name="moe",
    )(sb_e, sb_start, sb_n, xs_sorted, w_gate_up, b_gate_up.reshape(E, 1, H2), w_down, b_down.reshape(E, 1, D))


COMBINE_TM = 256


def _combine_body(dest_ref, h_ref, tw_ref, g2_ref, nw_ref, y_hbm, o_ref, ybuf, sem):
    tm = COMBINE_TM
    n = TOP_K * tm

    def copy(j):
        return pltpu.make_async_copy(y_hbm.at[pl.ds(dest_ref[0, 0, j], 1)], ybuf.at[pl.ds(j, 1)], sem)

    def start(j, carry):
        copy(j).start()
        return carry

    def wait(j, carry):
        copy(j).wait()
        return carry

    lax.fori_loop(0, n, start, 0, unroll=8)
    lax.fori_loop(0, n, wait, 0, unroll=8)

    tw = tw_ref[...]
    lo = hi = None
    for kk in range(TOP_K):
        a, b = _unpack_bf16_pair(ybuf[kk * tm:(kk + 1) * tm, :])
        wk = tw[:, kk:kk + 1]
        lo = wk * a if lo is None else lo + wk * a
        hi = wk * b if hi is None else hi + wk * b
    ffn = jnp.concatenate([lo, hi], axis=1)
    h = h_ref[...] + g2_ref[0] * ffn
    ms = jnp.mean(h * h, axis=-1, keepdims=True)
    o_ref[...] = h * lax.rsqrt(ms + EPS) * nw_ref[...]


def _combine(h1, y_sorted, dest, top_w, g2, final_norm_w, L):
    T, D = h1.shape
    tm = COMBINE_TM
    tiles = T // tm
    tpb = L // tm
    dest_tiles = dest.reshape(tiles, tm, TOP_K).transpose(0, 2, 1).reshape(tiles, 1, TOP_K * tm)
    return pl.pallas_call(
        _combine_body,
        out_shape=jax.ShapeDtypeStruct((T, D), F32),
        grid=(tiles,),
        in_specs=[pl.BlockSpec((1, 1, TOP_K * tm), lambda i: (i, 0, 0), memory_space=pltpu.SMEM),
                  pl.BlockSpec((tm, D), lambda i: (i, 0)),
                  pl.BlockSpec((tm, LANES), lambda i: (i, 0)),
                  pl.BlockSpec((1, 1, D), lambda i: (i // tpb, 0, 0)),
                  pl.BlockSpec((1, D), lambda i: (0, 0)),
                  pl.BlockSpec(memory_space=pl.ANY)],
        out_specs=pl.BlockSpec((tm, D), lambda i: (i, 0)),
        scratch_shapes=[pltpu.VMEM((TOP_K * tm, D // 2), U32), pltpu.SemaphoreType.DMA(())],
        compiler_params=_cparams(("arbitrary",)),
        name="combine",
    )(dest_tiles, h1, top_w, g2, final_norm_w.reshape(1, D), y_sorted)


def kernel(x, c, w_ada, b_ada, norm1_w, w_in, ssd_conv_w, ssd_conv_b, ssd_dt_bias, ssd_a_log, ssd_d_skip, ssd_norm_w, gla_w_gk2, gla_b_gk2, gla_norm_w, w_ssd_out, w_gla_out, w_out, norm2_w, w_router, b_router, w_gate_up, b_gate_up, w_down, b_down, final_norm_w):
    B, L, D = x.shape
    T = B * L
    assert D == D_MODEL and w_ada.shape[0] == 1
    x2d = x.reshape(T, D)

    mod = _ada(c, w_ada[0], b_ada[0])
    sh1, sc1, g1, sh2, sc2, g2 = [mod[:, i * D:(i + 1) * D].reshape(B, 1, D) for i in range(6)]

    proj = _inproj(x2d, norm1_w[0].reshape(1, D), sc1, sh1, _build_w_cat(w_in[0]), L)
    proj3 = proj.reshape(B, L, PROJ_COLS)
    y_ssd = _ssd(proj3, ssd_conv_w[0], ssd_conv_b[0], ssd_dt_bias[0], ssd_a_log[0], ssd_d_skip[0], ssd_norm_w[0])
    y_gla = _gla(proj3, gla_w_gk2[0], gla_b_gk2[0], gla_norm_w[0])
    merged = _merge(y_ssd.reshape(T, D), y_gla.reshape(T, D), proj, w_ssd_out[0].astype(BF16), w_gla_out[0].astype(BF16))
    h1, n2p, top_w, top_i = _post(merged, x2d, g1, norm2_w[0].reshape(1, D), sc2, sh2, w_out[0].astype(BF16),
                                  w_router[0], b_router[0], L)

    A = T * TOP_K
    n_rows = A + N_EXPERTS * MOE_SUB
    n_sb = N_EXPERTS + n_rows // MOE_TM
    dest, buf_tok, sb_e, sb_start, sb_n = _routing(top_i[:, :TOP_K], n_rows, n_sb)
    xs_sorted = _gather(n2p, buf_tok)
    y_sorted = _moe(xs_sorted, sb_e, sb_start, sb_n, w_gate_up[0], b_gate_up[0], w_down[0], b_down[0])
    out = _combine(h1, y_sorted, dest, top_w, g2, final_norm_w, L)
    return out.reshape(B, L, D)
```

```python
import functools

import jax
import jax.numpy as jnp
import numpy as np
from jax import lax
from jax.experimental import pallas as pl
from jax.experimental.pallas import tpu as pltpu

F32 = jnp.float32
BF16 = jnp.bfloat16
U32 = jnp.uint32
I32 = jnp.int32

EPS = 1e-6
LANES = 128
VMEM_LIMIT = 56 * 1024 * 1024

D_MODEL = 2048
SSD_HEADS, SSD_P, SSD_G, SSD_N, SSD_CONV, SSD_Q = 32, 64, 4, 128, 4, 256
SSD_INNER = SSD_HEADS * SSD_P
GLA_H, GLA_DK, GLA_DV, GLA_RANK, GLA_C = 4, 256, 512, 16, 64
GLA_GATE_NORM = 16.0
N_EXPERTS, TOP_K = 32, 4
SWIGLU_LIMIT, SWIGLU_ALPHA = 7.0, 1.702

SRC_A = (0, 5120)
SRC_DT = (5120, 5152)
SRC_B = (5152, 11296)
SRC_GK = (11296, 11312)
SRC_C = (11312, 15408)
A_Z, A_XS, A_BC = 0, 2048, 4096
B_Q, B_K, B_V, B_GO = 0, 1024, 2048, 4096
C_GS, C_GG = 0, 2048
PROJ_TN = 512

MOE_SUB = 256
MOE_TM = 2048
MOE_TH = 256


def _cparams(sem, vmem=VMEM_LIMIT):
    return pltpu.CompilerParams(dimension_semantics=sem, vmem_limit_bytes=vmem)


def _split3(x):
    hi = x.astype(BF16)
    r1 = x - hi.astype(F32)
    mid = r1.astype(BF16)
    lo = (r1 - mid.astype(F32)).astype(BF16)
    return hi, mid, lo


def _dot(a, b):
    return jnp.dot(a, b, preferred_element_type=F32)


def _dot_nt(a, b):
    return lax.dot_general(a, b, (((1,), (1,)), ((), ())), preferred_element_type=F32)


def _dot_tn(a, b):
    return lax.dot_general(a, b, (((0,), (0,)), ((), ())), preferred_element_type=F32)


def _dot_exact_lhs(a01, x):
    hi, mid, lo = _split3(x)
    return _dot(a01, hi) + _dot(a01, mid) + _dot(a01, lo)


def _dot_exact_rhs(x, b01):
    hi, mid, lo = _split3(x)
    return _dot(hi, b01) + _dot(mid, b01) + _dot(lo, b01)


def _silu(x):
    return x * jax.nn.sigmoid(x)


def _softplus(x):
    return jnp.maximum(x, 0.0) + jnp.log1p(jnp.exp(-jnp.abs(x)))


def _ada_body(ct_ref, w_ref, b_ref, o_ref):
    ct = ct_ref[...]
    cs = _silu(ct)
    w = w_ref[...]
    for m in range(ct.shape[1]):
        o_ref[m:m + 1, :] = jnp.sum(w * cs[:, m:m + 1], axis=0, keepdims=True) + b_ref[...]


def _ada(c, w_ada, b_ada):
    B, D = c.shape
    N = w_ada.shape[1]
    tn = 1024
    return pl.pallas_call(
        _ada_body,
        out_shape=jax.ShapeDtypeStruct((B, N), F32),
        grid=(N // tn,),
        in_specs=[pl.BlockSpec((D, B), lambda j: (0, 0)),
                  pl.BlockSpec((D, tn), lambda j: (0, j)),
                  pl.BlockSpec((1, tn), lambda j: (0, j))],
        out_specs=pl.BlockSpec((B, tn), lambda j: (0, j)),
        compiler_params=_cparams(("arbitrary",)),
        name="ada",
    )(c.T, w_ada, b_ada.reshape(1, N))


def _inproj_a_body(x_ref, nw_ref, sc_ref, sh_ref, w_ref, o_ref, n_ref):
    @pl.when(pl.program_id(1) == 0)
    def _():
        x = x_ref[...]
        ms = jnp.mean(x * x, axis=-1, keepdims=True)
        y = x * lax.rsqrt(ms + EPS) * nw_ref[...]
        n_ref[...] = (y * (1.0 + sc_ref[0]) + sh_ref[0]).astype(BF16)

    o_ref[...] = _dot(n_ref[...], w_ref[...].astype(BF16)).astype(BF16)


def _inproj_a(x2d, norm_w, sc, sh, w_in, L):
    T, D = x2d.shape
    tm, tn = 1024, PROJ_TN
    n_cols = SRC_A[1] - SRC_A[0]
    tiles_per_batch = L // tm
    return pl.pallas_call(
        _inproj_a_body,
        out_shape=(jax.ShapeDtypeStruct((T, n_cols), BF16), jax.ShapeDtypeStruct((T, D), BF16)),
        grid=(T // tm, n_cols // tn),
        in_specs=[pl.BlockSpec((tm, D), lambda i, j: (i, 0)),
                  pl.BlockSpec((1, D), lambda i, j: (0, 0)),
                  pl.BlockSpec((1, 1, D), lambda i, j: (i // tiles_per_batch, 0, 0)),
                  pl.BlockSpec((1, 1, D), lambda i, j: (i // tiles_per_batch, 0, 0)),
                  pl.BlockSpec((D, tn), lambda i, j: (0, SRC_A[0] // tn + j))],
        out_specs=(pl.BlockSpec((tm, tn), lambda i, j: (i, j)), pl.BlockSpec((tm, D), lambda i, j: (i, 0))),
        compiler_params=_cparams(("arbitrary", "arbitrary")),
        name="inproj_a",
    )(x2d, norm_w, sc, sh, w_in)


def _matmul_body(a_ref, w_ref, o_ref):
    o_ref[...] = _dot(a_ref[...], w_ref[...]).astype(BF16)


def _matmul(a, w, tn, name):
    T, D = a.shape
    N = w.shape[1]
    tm = 1024
    return pl.pallas_call(
        _matmul_body,
        out_shape=jax.ShapeDtypeStruct((T, N), BF16),
        grid=(T // tm, N // tn),
        in_specs=[pl.BlockSpec((tm, D), lambda i, j: (i, 0)), pl.BlockSpec((D, tn), lambda i, j: (0, j))],
        out_specs=pl.BlockSpec((tm, tn), lambda i, j: (i, j)),
        compiler_params=_cparams(("arbitrary", "arbitrary")),
        name=name,
    )(a, w)


def _ssd_body(z_ref, xs_ref, bc_ref, dt_ref, cwx_ref, cwbc_ref, cbx_ref, cbbc_ref, dtb_ref, alog_ref,
              dskip_ref, nw_ref, exp_ref, o_ref, ubx, ubbc, state):
    Q = SSD_Q
    c = pl.program_id(1)

    @pl.when(c == 0)
    def _():
        ubx[0:8, :] = jnp.zeros((8, ubx.shape[1]), F32)
        ubbc[0:8, :] = jnp.zeros((8, ubbc.shape[1]), F32)
        state[...] = jnp.zeros(state.shape, F32)

    def conv_silu(u_ref, ub, cw_ref, cb_ref):
        ub[8:8 + Q, :] = u_ref[0].astype(F32)
        acc = cb_ref[...]
        for kk in range(SSD_CONV):
            off = 8 - (SSD_CONV - 1) + kk
            acc = acc + cw_ref[kk:kk + 1, :] * ub[off:off + Q, :]
        ub[0:8, :] = ub[Q:Q + 8, :]
        return _silu(acc)

    xs = conv_silu(xs_ref, ubx, cwx_ref, cbx_ref)
    bcm = conv_silu(bc_ref, ubbc, cwbc_ref, cbbc_ref)
    GN = SSD_G * SSD_N
    bm_f, cm = bcm[:, :GN], bcm[:, GN:].astype(BF16)

    dt = _softplus(dt_ref[0].astype(F32) + dtb_ref[...])
    a_neg = -jnp.exp(alog_ref[...])
    dA = dt * a_neg

    row = lax.broadcasted_iota(I32, (Q, Q), 0)
    col = lax.broadcasted_iota(I32, (Q, Q), 1)
    causal = row >= col
    tri = jnp.where(causal, 1.0, 0.0).astype(BF16)
    acs = _dot_exact_lhs(tri, dA)
    acs_t = acs.T
    acs_last = acs[Q - 1:Q, :]
    exp_a = jnp.exp(acs)
    decay_st = jnp.exp(acs_last - acs)

    expand = exp_ref[...]
    dt_e = _dot_exact_rhs(dt, expand)
    dtd_e = _dot_exact_rhs(dt * decay_st, expand)
    expa_e = _dot_exact_rhs(exp_a, expand)

    x_dt = (xs * dt_e).astype(BF16)
    x_dd = (xs * dtd_e).astype(BF16)

    lane = lax.broadcasted_iota(I32, (Q, LANES), 1)
    lo_half = lane < SSD_P
    HG = SSD_HEADS // SSD_G
    GW = HG * SSD_P
    y_groups = []
    for g in range(SSD_G):
        cg = cm[:, g * SSD_N:(g + 1) * SSD_N]
        bg_f = bm_f[:, g * SSD_N:(g + 1) * SSD_N]
        bg = bg_f.astype(BF16)
        scores = _dot_nt(cg, bg)
        pieces = []
        for p in range(HG // 2):
            h0 = g * HG + 2 * p
            xp = x_dt[:, h0 * SSD_P:h0 * SSD_P + LANES]
            acc = None
            for s in range(2):
                h = h0 + s
                seg = acs[:, h:h + 1] - acs_t[h:h + 1, :]
                lmat = jnp.exp(jnp.where(causal, seg, -jnp.inf))
                m = (scores * lmat).astype(BF16)
                xh = jnp.where(lo_half if s == 0 else jnp.logical_not(lo_half), xp, jnp.zeros_like(xp))
                part = _dot(m, xh)
                acc = part if acc is None else acc + part
            pieces.append(acc)
        y_diag = jnp.concatenate(pieces, axis=1)
        st_old = state[g]
        y_off = _dot(cg, st_old.astype(BF16)) * expa_e[:, g * GW:(g + 1) * GW]
        st_new = _dot(bg_f.T.astype(BF16), x_dd[:, g * GW:(g + 1) * GW])
        state[g] = st_old * expa_e[Q - 1:Q, g * GW:(g + 1) * GW] + st_new
        y_groups.append(y_diag + y_off)

    y = jnp.concatenate(y_groups, axis=1) + xs * dskip_ref[...]
    y = y * _silu(z_ref[0].astype(F32))
    outs = []
    for g in range(SSD_G):
        yg = y[:, g * GW:(g + 1) * GW]
        ms = jnp.mean(yg * yg, axis=-1, keepdims=True)
        outs.append(yg * lax.rsqrt(ms + EPS) * nw_ref[:, g * GW:(g + 1) * GW])
    o_ref[0] = jnp.concatenate(outs, axis=1).astype(BF16)


def _ssd(pa3, ps3, conv_w, conv_b, dt_bias, a_log, d_skip, norm_w):
    B, L, _ = pa3.shape
    Q, DI, BCW = SSD_Q, SSD_INNER, 2 * SSD_G * SSD_N
    padl = lambda a: jnp.pad(a.reshape(1, -1), ((0, 0), (0, LANES - a.shape[-1])))
    expand = (np.arange(LANES)[:, None] == (np.arange(DI)[None, :] // SSD_P)).astype(np.float32)
    const = lambda shape: pl.BlockSpec(shape, lambda b, c: (0,) * len(shape))
    return pl.pallas_call(
        _ssd_body,
        out_shape=jax.ShapeDtypeStruct((B, L, DI), BF16),
        grid=(B, L // Q),
        in_specs=[pl.BlockSpec((1, Q, DI), lambda b, c: (b, c, A_Z // DI)),
                  pl.BlockSpec((1, Q, DI), lambda b, c: (b, c, A_XS // DI)),
                  pl.BlockSpec((1, Q, BCW), lambda b, c: (b, c, A_BC // BCW)),
                  pl.BlockSpec((1, Q, LANES), lambda b, c: (b, c, 0)),
                  const((SSD_CONV, DI)), const((SSD_CONV, BCW)), const((1, DI)), const((1, BCW)),
                  const((1, LANES)), const((1, LANES)), const((1, DI)), const((1, DI)),
                  const((LANES, DI))],
        out_specs=pl.BlockSpec((1, Q, DI), lambda b, c: (b, c, 0)),
        scratch_shapes=[pltpu.VMEM((Q + 8, DI), F32), pltpu.VMEM((Q + 8, BCW), F32),
                        pltpu.VMEM((SSD_G, SSD_N, DI // SSD_G), F32)],
        compiler_params=_cparams(("arbitrary", "arbitrary")),
        name="ssd",
    )(pa3, pa3, pa3, ps3, conv_w[:, :DI], conv_w[:, DI:], conv_b[:DI].reshape(1, DI),
      conv_b[DI:].reshape(1, BCW), padl(dt_bias), padl(a_log), jnp.repeat(d_skip, SSD_P).reshape(1, DI),
      norm_w.reshape(1, DI), jnp.asarray(expand, BF16))


GLA_RB = 256


def _gla_body(q_ref, k_ref, v_ref, g_ref, gk_ref, w_ref, b_ref, nw_ref, o_ref, st):
    RB, C = GLA_RB, GLA_C

    @pl.when(pl.program_id(1) == 0)
    def _():
        st[...] = jnp.zeros(st.shape, F32)

    w_hi, w_mid, w_lo = _split3(w_ref[...])
    gk = gk_ref[0]
    pre = _dot(gk, w_hi) + _dot(gk, w_mid) + _dot(gk, w_lo) + b_ref[...]
    log_a = (jnp.minimum(pre, 0.0) - jnp.log1p(jnp.exp(-jnp.abs(pre)))) * (1.0 / GLA_GATE_NORM)

    row = lax.broadcasted_iota(I32, (RB, RB), 0)
    col = lax.broadcasted_iota(I32, (RB, RB), 1)
    blocktri = jnp.where((row // C == col // C) & (row >= col), 1.0, 0.0).astype(BF16)
    bcs_all = _dot_exact_lhs(blocktri, log_a)

    r64 = lax.broadcasted_iota(I32, (C, C), 0)
    c64 = lax.broadcasted_iota(I32, (C, C), 1)
    causal = r64 >= c64
    scale = GLA_DK ** -0.5

    for s in range(RB // C):
        rs = slice(s * C, (s + 1) * C)
        bcs = bcs_all[rs, :]
        last = bcs[C - 1:C, :]
        qf = q_ref[0, rs, :].astype(F32)
        kf = k_ref[0, rs, :].astype(F32)
        q_dec = (qf * scale * jnp.exp(bcs)).astype(BF16)
        k_inv = (kf * jnp.exp(-bcs)).astype(BF16)
        k_st = (kf * jnp.exp(last - bcs)).astype(BF16)
        cdec = jnp.exp(last)
        for h in range(GLA_H):
            ks = slice(h * GLA_DK, (h + 1) * GLA_DK)
            vs = slice(h * GLA_DV, (h + 1) * GLA_DV)
            vh = v_ref[0, rs, vs]
            attn = jnp.where(causal, _dot_nt(q_dec[:, ks], k_inv[:, ks]), 0.0)
            st_h = st[h]
            o = _dot(attn.astype(BF16), vh) + _dot_nt(q_dec[:, ks], st_h.astype(BF16))
            st[h] = st_h * cdec[:, ks] + _dot_tn(vh, k_st[:, ks])
            ms = jnp.mean(o * o, axis=-1, keepdims=True)
            o = o * lax.rsqrt(ms + EPS) * nw_ref[...]
            o_ref[0, rs, vs] = (o * _silu(g_ref[0, rs, vs].astype(F32))).astype(BF16)


def _gla(pb3, ps3, w_gk2, b_gk2, norm_w):
    B, L, _ = pb3.shape
    RB, KD, VD = GLA_RB, GLA_H * GLA_DK, GLA_H * GLA_DV
    w_pad = jnp.pad(w_gk2, ((0, LANES - w_gk2.shape[0]), (0, 0)))
    const = lambda shape: pl.BlockSpec(shape, lambda b, c: (0,) * len(shape))
    return pl.pallas_call(
        _gla_body,
        out_shape=jax.ShapeDtypeStruct((B, L, VD), BF16),
        grid=(B, L // RB),
        in_specs=[pl.BlockSpec((1, RB, KD), lambda b, c: (b, c, B_Q // KD)),
                  pl.BlockSpec((1, RB, KD), lambda b, c: (b, c, B_K // KD)),
                  pl.BlockSpec((1, RB, VD), lambda b, c: (b, c, B_V // VD)),
                  pl.BlockSpec((1, RB, VD), lambda b, c: (b, c, B_GO // VD)),
                  pl.BlockSpec((1, RB, LANES), lambda b, c: (b, c, 1)),
                  const((LANES, KD)), const((1, KD)), const((1, GLA_DV))],
        out_specs=pl.BlockSpec((1, RB, VD), lambda b, c: (b, c, 0)),
        scratch_shapes=[pltpu.VMEM((GLA_H, GLA_DV, GLA_DK), F32)],
        compiler_params=_cparams(("arbitrary", "arbitrary")),
        name="gla",
    )(pb3, pb3, pb3, pb3, ps3, w_pad, b_gk2.reshape(1, KD), norm_w.reshape(1, GLA_DV))


def _merge_body(ys_ref, yg_ref, gs_ref, gg_ref, ws_ref, wg_ref, o_ref):
    a = _dot(ys_ref[...], ws_ref[...])
    b = _dot(yg_ref[...], wg_ref[...])
    m = jax.nn.sigmoid(gs_ref[...].astype(F32)) * a + jax.nn.sigmoid(gg_ref[...].astype(F32)) * b
    o_ref[...] = m.astype(BF16)


def _merge(y_ssd, y_gla, proj_c, w_ssd_out, w_gla_out):
    T, D = y_ssd.shape
    tm, tn = 512, 1024
    return pl.pallas_call(
        _merge_body,
        out_shape=jax.ShapeDtypeStruct((T, D), BF16),
        grid=(D // tn, T // tm),
        in_specs=[pl.BlockSpec((tm, D), lambda j, i: (i, 0)),
                  pl.BlockSpec((tm, D), lambda j, i: (i, 0)),
                  pl.BlockSpec((tm, tn), lambda j, i: (i, C_GS // tn + j)),
                  pl.BlockSpec((tm, tn), lambda j, i: (i, C_GG // tn + j)),
                  pl.BlockSpec((D, tn), lambda j, i: (0, j)),
                  pl.BlockSpec((D, tn), lambda j, i: (0, j))],
        out_specs=pl.BlockSpec((tm, tn), lambda j, i: (i, j)),
        compiler_params=_cparams(("arbitrary", "arbitrary")),
        name="merge",
    )(y_ssd, y_gla, proj_c, proj_c, w_ssd_out, w_gla_out)


def _pack_bf16_pair(a, b):
    ua = lax.bitcast_convert_type(a.astype(BF16).astype(F32), U32)
    ub = lax.bitcast_convert_type(b.astype(BF16).astype(F32), U32)
    return (ua & jnp.uint32(0xFFFF0000)) | (ub >> 16)


def _unpack_bf16_pair(w):
    a = lax.bitcast_convert_type(w & jnp.uint32(0xFFFF0000), F32)
    b = lax.bitcast_convert_type(w << 16, F32)
    return a, b


def _post_body(m_ref, x_ref, g1_ref, nw_ref, sc_ref, sh_ref, wo_ref, wr_ref, br_ref,
               h_ref, n2p_ref, tw_ref, ti_ref):
    D = x_ref.shape[1]
    h = x_ref[...] + g1_ref[0] * _dot(m_ref[...], wo_ref[...])
    h_ref[...] = h
    ms = jnp.mean(h * h, axis=-1, keepdims=True)
    n2 = h * lax.rsqrt(ms + EPS) * nw_ref[...] * (1.0 + sc_ref[0]) + sh_ref[0]
    n2p_ref[...] = _pack_bf16_pair(n2[:, :D // 2], n2[:, D // 2:])

    n_hi = n2.astype(BF16)
    n_lo = (n2 - n_hi.astype(F32)).astype(BF16)
    wr = wr_ref[...]
    w_hi = wr.astype(BF16)
    w_lo = (wr - w_hi.astype(F32)).astype(BF16)
    logits = _dot(n_hi, w_hi) + _dot(n_hi, w_lo) + _dot(n_lo, w_hi) + br_ref[...]

    lane = lax.broadcasted_iota(I32, logits.shape, 1)
    cur = jnp.where(lane < N_EXPERTS, logits, -jnp.inf)
    vals, idxs = [], []
    for _ in range(TOP_K):
        mx = jnp.max(cur, axis=-1, keepdims=True)
        ix = jnp.min(jnp.where(cur == mx, lane, LANES), axis=-1, keepdims=True)
        vals.append(mx)
        idxs.append(ix)
        cur = jnp.where(lane == ix, -jnp.inf, cur)
    es = [jnp.exp(v - vals[0]) for v in vals]
    denom = es[0] + es[1] + es[2] + es[3]
    tw = jnp.zeros(logits.shape, F32)
    ti = jnp.zeros(logits.shape, I32)
    for kk in range(TOP_K):
        tw = jnp.where(lane == kk, es[kk] / denom, tw)
        ti = jnp.where(lane == kk, idxs[kk], ti)
    tw_ref[...] = tw
    ti_ref[...] = ti


def _post(merged, x2d, g1, norm_w, sc, sh, w_out, w_router, b_router, L):
    T, D = x2d.shape
    tm = 512
    tpb = L // tm
    wr = jnp.pad(w_router, ((0, 0), (0, LANES - w_router.shape[1])))
    br = jnp.pad(b_router.reshape(1, -1), ((0, 0), (0, LANES - b_router.shape[0])))
    row = lambda w: pl.BlockSpec((tm, w), lambda i: (i, 0))
    per_b = pl.BlockSpec((1, 1, D), lambda i: (i // tpb, 0, 0))
    const = lambda shape: pl.BlockSpec(shape, lambda i: (0,) * len(shape))
    return pl.pallas_call(
        _post_body,
        out_shape=(jax.ShapeDtypeStruct((T, D), F32), jax.ShapeDtypeStruct((T, D // 2), U32),
                   jax.ShapeDtypeStruct((T, LANES), F32), jax.ShapeDtypeStruct((T, LANES), I32)),
        grid=(T // tm,),
        in_specs=[row(D), row(D), per_b, const((1, D)), per_b, per_b, const((D, D)), const((D, LANES)),
                  const((1, LANES))],
        out_specs=(row(D), row(D // 2), row(LANES), row(LANES)),
        compiler_params=_cparams(("arbitrary",)),
        name="post",
    )(merged, x2d, g1, norm_w, sc, sh, w_out, wr, br)


def _routing(top_idx, n_rows, n_sb):
    T = top_idx.shape[0]
    A = T * TOP_K
    flat_e = top_idx.reshape(A)
    onehot = (flat_e[:, None] == jnp.arange(N_EXPERTS, dtype=I32)[None, :]).astype(I32)
    csum = jnp.cumsum(onehot, axis=0)
    rank = jnp.sum(onehot * csum, axis=1) - 1
    counts = csum[-1]
    padded = (counts + MOE_SUB - 1) // MOE_SUB * MOE_SUB
    pend = jnp.cumsum(padded)
    pstart = pend - padded
    dest = (pstart[flat_e] + rank).astype(I32)

    nsb = (padded + MOE_TM - 1) // MOE_TM
    sb_end = jnp.cumsum(nsb)
    sb_first = sb_end - nsb
    slot = (sb_first[flat_e] + rank // MOE_TM) * MOE_TM + rank % MOE_TM
    sb_tok = jnp.zeros((n_sb * MOE_TM,), I32).at[slot].set(jnp.arange(A, dtype=I32) // TOP_K)

    total = sb_end[-1]
    i = jnp.arange(n_sb, dtype=I32)
    e_of = jnp.minimum(jnp.searchsorted(sb_end, i, side="right"), N_EXPERTS - 1).astype(I32)
    local = i - sb_first[e_of]
    valid = i < total
    last_e = e_of[jnp.maximum(total - 1, 0)]
    sb_e = jnp.where(valid, e_of, last_e).astype(I32)
    sb_start = jnp.where(valid, pstart[e_of] + local * MOE_TM, 0).astype(I32)
    sb_start = jnp.concatenate([sb_start, pend[-1:].astype(I32)])
    sb_n = jnp.where(valid, jnp.clip(padded[e_of] - local * MOE_TM, 0, MOE_TM), 0).astype(I32)
    return dest, sb_tok.reshape(n_sb, 1, MOE_TM), sb_e, sb_start, sb_n


def _deinterleave(gu):
    rows, two_w = gu.shape
    lane = lax.broadcasted_iota(I32, (rows, LANES), 1)
    idx_e = (2 * lane) % LANES
    idx_o = idx_e + 1
    first = lane < LANES // 2
    gates, ups = [], []
    for p in range(two_w // (2 * LANES)):
        a = gu[:, (2 * p) * LANES:(2 * p + 1) * LANES]
        b = gu[:, (2 * p + 1) * LANES:(2 * p + 2) * LANES]
        gates.append(jnp.where(first, jnp.take_along_axis(a, idx_e, axis=1), jnp.take_along_axis(b, idx_e, axis=1)))
        ups.append(jnp.where(first, jnp.take_along_axis(a, idx_o, axis=1), jnp.take_along_axis(b, idx_o, axis=1)))
    return jnp.concatenate(gates, axis=1), jnp.concatenate(ups, axis=1)


MOE_DMA_UNROLL = 16


def _moe_body(sbe_ref, sbs_ref, sbn_ref, tok_ref, tokn_ref, x_hbm, wgu_ref, bgu_ref, wd_ref, bd_ref, y_hbm,
              xs_buf, xb_buf, acc, ystage, wg_s, wd_s, sem_in, sem_out):
    i = pl.program_id(0)
    hc = pl.program_id(1)
    n_sb = pl.num_programs(0)
    n_hc = pl.num_programs(1)
    SUB = MOE_SUB
    half = xs_buf.shape[1]
    nblk = sbn_ref[i] // SUB
    start_row = sbs_ref[i]

    def row_copy(t_ref, r):
        return pltpu.make_async_copy(x_hbm.at[pl.ds(t_ref[0, 0, r], 1)], xs_buf.at[pl.ds(r, 1)], sem_in)

    def start_in(t_ref, n_rows):
        def body(g, carry):
            for u in range(MOE_DMA_UNROLL):
                row_copy(t_ref, g * MOE_DMA_UNROLL + u).start()
            return carry

        lax.fori_loop(0, n_rows // MOE_DMA_UNROLL, body, 0)

    @pl.when(hc == 0)
    def _():
        @pl.when(i == 0)
        def _():
            start_in(tok_ref, sbn_ref[0])

        def wait_in(g, carry):
            for u in range(MOE_DMA_UNROLL):
                row_copy(tok_ref, g * MOE_DMA_UNROLL + u).wait()
            return carry

        lax.fori_loop(0, sbn_ref[i] // MOE_DMA_UNROLL, wait_in, 0)

        def convert(r, carry):
            rows = pl.ds(pl.multiple_of(r * SUB, SUB), SUB)
            a, b = _unpack_bf16_pair(xs_buf[rows, :])
            xb_buf[rows, 0:half] = a.astype(BF16)
            xb_buf[rows, half:2 * half] = b.astype(BF16)
            return carry

        lax.fori_loop(0, nblk, convert, 0)

        @pl.when(i + 1 < n_sb)
        def _():
            start_in(tokn_ref, sbn_ref[i + 1])

    def accumulate(blocks, first):
        rows = [pl.ds(pl.multiple_of(r * SUB, SUB), SUB) for r in blocks]
        gus = [_dot(xb_buf[rw, :], wg_s[...]) + bgu_ref[0] for rw in rows]
        hs = []
        for gu in gus:
            gate, up = _deinterleave(gu)
            gate = jnp.minimum(gate, SWIGLU_LIMIT)
            up = jnp.clip(up, -SWIGLU_LIMIT, SWIGLU_LIMIT)
            hs.append((gate * jax.nn.sigmoid(gate * SWIGLU_ALPHA) * (up + 1.0)).astype(BF16))
        for rw, hidden in zip(rows, hs):
            v = _dot(hidden, wd_s[...])
            if first:
                acc[rw, :] = v + bd_ref[0]
            else:
                acc[rw, :] += v

    def run(first):
        def quad(p, carry):
            accumulate([4 * p, 4 * p + 1, 4 * p + 2, 4 * p + 3], first)
            return carry

        lax.fori_loop(0, nblk // 4, quad, 0)
        done = nblk // 4 * 4

        @pl.when(nblk - done >= 2)
        def _():
            accumulate([done, done + 1], first)

        @pl.when(nblk % 2 == 1)
        def _():
            accumulate([nblk - 1], first)

    @pl.when(nblk > 0)
    def _():
        wg_s[...] = wgu_ref[0].astype(BF16)
        wd_s[...] = wd_ref[0].astype(BF16)

        @pl.when(hc == 0)
        def _():
            run(True)

        @pl.when(hc != 0)
        def _():
            run(False)

    @pl.when(hc == n_hc - 1)
    def _():
        def out_copy(r, slot):
            dst = pl.multiple_of(start_row + r * SUB, SUB)
            return pltpu.make_async_copy(ystage.at[slot], y_hbm.at[pl.ds(dst, SUB)], sem_out.at[slot])

        def body(r, carry):
            slot = r % 2

            @pl.when(r >= 2)
            def _():
                out_copy(r - 2, slot).wait()

            rows = pl.ds(pl.multiple_of(r * SUB, SUB), SUB)
            v = acc[rows, :]
            ystage[slot] = _pack_bf16_pair(v[:, :half], v[:, half:])
            out_copy(r, slot).start()
            return carry

        lax.fori_loop(0, nblk, body, 0)

        def drain(r, carry):
            out_copy(r, r % 2).wait()
            return carry

        lax.fori_loop(jnp.maximum(nblk - 2, 0), nblk, drain, 0)

    @pl.when((i == n_sb - 1) & (hc == n_hc - 1))
    def _():
        ystage[0] = jnp.zeros(ystage.shape[1:], U32)

        def tail_copy(b):
            return pltpu.make_async_copy(ystage.at[0], y_hbm.at[pl.ds(pl.multiple_of(b * SUB, SUB), SUB)], sem_out.at[0])

        def start(b, carry):
            tail_copy(b).start()
            return carry

        def wait(b, carry):
            tail_copy(b).wait()
            return carry

        first, stop = sbs_ref[n_sb] // SUB, y_hbm.shape[0] // SUB
        lax.fori_loop(first, stop, start, 0)
        lax.fori_loop(first, stop, wait, 0)


def _moe(n2p, n_rows, sb_tok, sb_e, sb_start, sb_n, w_gate_up, b_gate_up, w_down, b_down):
    half = n2p.shape[1]
    E, D, H2 = w_gate_up.shape
    n_sb = sb_e.shape[0]
    TH = MOE_TH
    n_hc = (H2 // 2) // TH
    hc_eff = lambda i, hc, sbn: jnp.where(sbn[i] > 0, hc, n_hc - 1)
    grid_spec = pltpu.PrefetchScalarGridSpec(
        num_scalar_prefetch=3,
        grid=(n_sb, n_hc),
        in_specs=[pl.BlockSpec((1, 1, MOE_TM), lambda i, hc, sbe, sbs, sbn: (i, 0, 0), memory_space=pltpu.SMEM),
                  pl.BlockSpec((1, 1, MOE_TM), lambda i, hc, sbe, sbs, sbn: (jnp.minimum(i + 1, n_sb - 1), 0, 0),
                               memory_space=pltpu.SMEM),
                  pl.BlockSpec(memory_space=pl.ANY),
                  pl.BlockSpec((1, D, 2 * TH), lambda i, hc, sbe, sbs, sbn: (sbe[i], 0, hc_eff(i, hc, sbn))),
                  pl.BlockSpec((1, 1, 2 * TH), lambda i, hc, sbe, sbs, sbn: (sbe[i], 0, hc_eff(i, hc, sbn))),
                  pl.BlockSpec((1, TH, D), lambda i, hc, sbe, sbs, sbn: (sbe[i], hc_eff(i, hc, sbn), 0)),
                  pl.BlockSpec((1, 1, D), lambda i, hc, sbe, sbs, sbn: (sbe[i], 0, 0))],
        out_specs=pl.BlockSpec(memory_space=pl.ANY),
        scratch_shapes=[pltpu.VMEM((MOE_TM, half), U32), pltpu.VMEM((MOE_TM, 2 * half), BF16),
                        pltpu.VMEM((MOE_TM, D), F32), pltpu.VMEM((2, MOE_SUB, half), U32),
                        pltpu.VMEM((D, 2 * TH), BF16), pltpu.VMEM((TH, D), BF16),
                        pltpu.SemaphoreType.DMA(()), pltpu.SemaphoreType.DMA((2,))],
    )
    return pl.pallas_call(
        _moe_body,
        out_shape=jax.ShapeDtypeStruct((n_rows, half), U32),
        grid_spec=grid_spec,
        compiler_params=_cparams(("arbitrary", "arbitrary")),
        name="moe",
    )(sb_e, sb_start, sb_n, sb_tok, sb_tok, n2p, w_gate_up, b_gate_up.reshape(E, 1, H2), w_down, b_down.reshape(E, 1, D))


COMBINE_TM = 256


def _combine_body(dest_ref, h_ref, tw_ref, g2_ref, nw_ref, y_hbm, o_ref, ybuf, sem):
    tm = COMBINE_TM
    n = TOP_K * tm

    def copy(j):
        return pltpu.make_async_copy(y_hbm.at[pl.ds(dest_ref[0, 0, j], 1)], ybuf.at[pl.ds(j, 1)], sem)

    def start(j, carry):
        copy(j).start()
        return carry

    def wait(j, carry):
        copy(j).wait()
        return carry

    lax.fori_loop(0, n, start, 0, unroll=8)
    lax.fori_loop(0, n, wait, 0, unroll=8)

    tw = tw_ref[...]
    lo = hi = None
    for kk in range(TOP_K):
        a, b = _unpack_bf16_pair(ybuf[kk * tm:(kk + 1) * tm, :])
        wk = tw[:, kk:kk + 1]
        lo = wk * a if lo is None else lo + wk * a
        hi = wk * b if hi is None else hi + wk * b
    ffn = jnp.concatenate([lo, hi], axis=1)
    h = h_ref[...] + g2_ref[0] * ffn
    ms = jnp.mean(h * h, axis=-1, keepdims=True)
    o_ref[...] = h * lax.rsqrt(ms + EPS) * nw_ref[...]


def _combine(h1, y_sorted, dest, top_w, g2, final_norm_w, L):
    T, D = h1.shape
    tm = COMBINE_TM
    tiles = T // tm
    tpb = L // tm
    dest_tiles = dest.reshape(tiles, tm, TOP_K).transpose(0, 2, 1).reshape(tiles, 1, TOP_K * tm)
    return pl.pallas_call(
        _combine_body,
        out_shape=jax.ShapeDtypeStruct((T, D), F32),
        grid=(tiles,),
        in_specs=[pl.BlockSpec((1, 1, TOP_K * tm), lambda i: (i, 0, 0), memory_space=pltpu.SMEM),
                  pl.BlockSpec((tm, D), lambda i: (i, 0)),
                  pl.BlockSpec((tm, LANES), lambda i: (i, 0)),
                  pl.BlockSpec((1, 1, D), lambda i: (i // tpb, 0, 0)),
                  pl.BlockSpec((1, D), lambda i: (0, 0)),
                  pl.BlockSpec(memory_space=pl.ANY)],
        out_specs=pl.BlockSpec((tm, D), lambda i: (i, 0)),
        scratch_shapes=[pltpu.VMEM((TOP_K * tm, D // 2), U32), pltpu.SemaphoreType.DMA(())],
        compiler_params=_cparams(("arbitrary",)),
        name="combine",
    )(dest_tiles, h1, top_w, g2, final_norm_w.reshape(1, D), y_sorted)


def kernel(x, c, w_ada, b_ada, norm1_w, w_in, ssd_conv_w, ssd_conv_b, ssd_dt_bias, ssd_a_log, ssd_d_skip, ssd_norm_w, gla_w_gk2, gla_b_gk2, gla_norm_w, w_ssd_out, w_gla_out, w_out, norm2_w, w_router, b_router, w_gate_up, b_gate_up, w_down, b_down, final_norm_w):
    B, L, D = x.shape
    T = B * L
    assert D == D_MODEL and w_ada.shape[0] == 1
    x2d = x.reshape(T, D)

    mod = _ada(c, w_ada[0], b_ada[0])
    sh1, sc1, g1, sh2, sc2, g2 = [mod[:, i * D:(i + 1) * D].reshape(B, 1, D) for i in range(6)]

    w = w_in[0]
    pad = lambda a: jnp.pad(a, ((0, 0), (0, LANES - a.shape[1])))
    w_b = w[:, SRC_B[0]:SRC_B[1]].astype(BF16)
    w_c = w[:, SRC_C[0]:SRC_C[1]].astype(BF16)
    w_s = jnp.concatenate([pad(w[:, SRC_DT[0]:SRC_DT[1]]), pad(w[:, SRC_GK[0]:SRC_GK[1]])], axis=1).astype(BF16)
    proj_a, n1 = _inproj_a(x2d, norm1_w[0].reshape(1, D), sc1, sh1, w, L)
    proj_b = _matmul(n1, w_b, PROJ_TN, "inproj_b")
    proj_c = _matmul(n1, w_c, PROJ_TN, "inproj_c")
    proj_s = _matmul(n1, w_s, 2 * LANES, "inproj_s")
    pa3, pb3, ps3 = [p.reshape(B, L, p.shape[1]) for p in (proj_a, proj_b, proj_s)]
    y_ssd = _ssd(pa3, ps3, ssd_conv_w[0], ssd_conv_b[0], ssd_dt_bias[0], ssd_a_log[0], ssd_d_skip[0], ssd_norm_w[0])
    y_gla = _gla(pb3, ps3, gla_w_gk2[0], gla_b_gk2[0], gla_norm_w[0])
    merged = _merge(y_ssd.reshape(T, D), y_gla.reshape(T, D), proj_c, w_ssd_out[0].astype(BF16), w_gla_out[0].astype(BF16))
    h1, n2p, top_w, top_i = _post(merged, x2d, g1, norm2_w[0].reshape(1, D), sc2, sh2, w_out[0].astype(BF16),
                                  w_router[0], b_router[0], L)

    A = T * TOP_K
    n_rows = A + N_EXPERTS * MOE_SUB
    n_sb = N_EXPERTS + n_rows // MOE_TM
    dest, sb_tok, sb_e, sb_start, sb_n = _routing(top_i[:, :TOP_K], n_rows, n_sb)
    y_sorted = _moe(n2p, n_rows, sb_tok, sb_e, sb_start, sb_n, w_gate_up[0], b_gate_up[0], w_down[0], b_down[0])
    out = _combine(h1, y_sorted, dest, top_w, g2, final_norm_w, L)
    return out.reshape(B, L, D)
```

```python
import functools

import jax
import jax.numpy as jnp
import numpy as np
from jax import lax
from jax.experimental import pallas as pl
from jax.experimental.pallas import tpu as pltpu

F32 = jnp.float32
BF16 = jnp.bfloat16
U32 = jnp.uint32
I32 = jnp.int32

EPS = 1e-6
LANES = 128
VMEM_LIMIT = 56 * 1024 * 1024

D_MODEL = 2048
SSD_HEADS, SSD_P, SSD_G, SSD_N, SSD_CONV, SSD_Q = 32, 64, 4, 128, 4, 256
SSD_INNER = SSD_HEADS * SSD_P
GLA_H, GLA_DK, GLA_DV, GLA_RANK, GLA_C = 4, 256, 512, 16, 64
GLA_GATE_NORM = 16.0
N_EXPERTS, TOP_K = 32, 4
SWIGLU_LIMIT, SWIGLU_ALPHA = 7.0, 1.702

SRC_A = (0, 5120)
SRC_DT = (5120, 5152)
SRC_B = (5152, 11296)
SRC_GK = (11296, 11312)
SRC_C = (11312, 15408)
A_Z, A_XS, A_BC = 0, 2048, 4096
B_Q, B_K, B_V, B_GO = 0, 1024, 2048, 4096
C_GS, C_GG = 0, 2048
PROJ_TN = 1024

MOE_SUB = 256
MOE_TM = 2048
MOE_TH = 256


def _cparams(sem, vmem=VMEM_LIMIT):
    return pltpu.CompilerParams(dimension_semantics=sem, vmem_limit_bytes=vmem)


def _split3(x):
    hi = x.astype(BF16)
    r1 = x - hi.astype(F32)
    mid = r1.astype(BF16)
    lo = (r1 - mid.astype(F32)).astype(BF16)
    return hi, mid, lo


def _dot(a, b):
    return jnp.dot(a, b, preferred_element_type=F32)


def _dot_nt(a, b):
    return lax.dot_general(a, b, (((1,), (1,)), ((), ())), preferred_element_type=F32)


def _dot_tn(a, b):
    return lax.dot_general(a, b, (((0,), (0,)), ((), ())), preferred_element_type=F32)


def _dot_exact_lhs(a01, x):
    hi, mid, lo = _split3(x)
    return _dot(a01, hi) + _dot(a01, mid) + _dot(a01, lo)


def _dot_exact_rhs(x, b01):
    hi, mid, lo = _split3(x)
    return _dot(hi, b01) + _dot(mid, b01) + _dot(lo, b01)


def _silu(x):
    return x * jax.nn.sigmoid(x)


def _softplus(x):
    return jnp.maximum(x, 0.0) + jnp.log1p(jnp.exp(-jnp.abs(x)))


def _ada_body(ct_ref, w_ref, b_ref, o_ref):
    ct = ct_ref[...]
    cs = _silu(ct)
    w = w_ref[...]
    for m in range(ct.shape[1]):
        o_ref[m:m + 1, :] = jnp.sum(w * cs[:, m:m + 1], axis=0, keepdims=True) + b_ref[...]


def _ada(c, w_ada, b_ada):
    B, D = c.shape
    N = w_ada.shape[1]
    tn = 1024
    return pl.pallas_call(
        _ada_body,
        out_shape=jax.ShapeDtypeStruct((B, N), F32),
        grid=(N // tn,),
        in_specs=[pl.BlockSpec((D, B), lambda j: (0, 0)),
                  pl.BlockSpec((D, tn), lambda j: (0, j)),
                  pl.BlockSpec((1, tn), lambda j: (0, j))],
        out_specs=pl.BlockSpec((B, tn), lambda j: (0, j)),
        compiler_params=_cparams(("arbitrary",)),
        name="ada",
    )(c.T, w_ada, b_ada.reshape(1, N))


def _inproj_a_body(x_ref, nw_ref, sc_ref, sh_ref, w_ref, o_ref, n_ref):
    @pl.when(pl.program_id(1) == 0)
    def _():
        x = x_ref[...]
        ms = jnp.mean(x * x, axis=-1, keepdims=True)
        y = x * lax.rsqrt(ms + EPS) * nw_ref[...]
        n_ref[...] = (y * (1.0 + sc_ref[0]) + sh_ref[0]).astype(BF16)

    o_ref[...] = _dot_nt(n_ref[...], w_ref[...].astype(BF16)).astype(BF16)


def _inproj_a(x2d, norm_w, sc, sh, w_t, L):
    T, D = x2d.shape
    tm, tn = 1024, PROJ_TN // 2
    n_cols = SRC_A[1] - SRC_A[0]
    tiles_per_batch = L // tm
    return pl.pallas_call(
        _inproj_a_body,
        out_shape=(jax.ShapeDtypeStruct((T, n_cols), BF16), jax.ShapeDtypeStruct((T, D), BF16)),
        grid=(T // tm, n_cols // tn),
        in_specs=[pl.BlockSpec((tm, D), lambda i, j: (i, 0)),
                  pl.BlockSpec((1, D), lambda i, j: (0, 0)),
                  pl.BlockSpec((1, 1, D), lambda i, j: (i // tiles_per_batch, 0, 0)),
                  pl.BlockSpec((1, 1, D), lambda i, j: (i // tiles_per_batch, 0, 0)),
                  pl.BlockSpec((tn, D), lambda i, j: (SRC_A[0] // tn + j, 0))],
        out_specs=(pl.BlockSpec((tm, tn), lambda i, j: (i, j)), pl.BlockSpec((tm, D), lambda i, j: (i, 0))),
        compiler_params=_cparams(("arbitrary", "arbitrary")),
        name="inproj_a",
    )(x2d, norm_w, sc, sh, w_t)


def _matmul_body(a_ref, w_ref, o_ref):
    o_ref[...] = _dot_nt(a_ref[...], w_ref[...].astype(BF16)).astype(BF16)


def _matmul(a, w_t, rows, tn, name):
    T, D = a.shape
    N = rows[1] - rows[0]
    tm = 1024
    return pl.pallas_call(
        _matmul_body,
        out_shape=jax.ShapeDtypeStruct((T, N), BF16),
        grid=(T // tm, N // tn),
        in_specs=[pl.BlockSpec((tm, D), lambda i, j: (i, 0)),
                  pl.BlockSpec((pl.Element(tn), pl.Element(D)),
                               lambda i, j: ((rows[0] // 8 + j * (tn // 8)) * 8, 0))],
        out_specs=pl.BlockSpec((tm, tn), lambda i, j: (i, j)),
        compiler_params=_cparams(("arbitrary", "arbitrary")),
        name=name,
    )(a, w_t)


def _ssd_body(z_ref, xs_ref, bc_ref, dt_ref, cwx_ref, cwbc_ref, cbx_ref, cbbc_ref, dtb_ref, alog_ref,
              dskip_ref, nw_ref, exp_ref, o_ref, ubx, ubbc, state):
    Q = SSD_Q
    c = pl.program_id(1)

    @pl.when(c == 0)
    def _():
        ubx[0:8, :] = jnp.zeros((8, ubx.shape[1]), F32)
        ubbc[0:8, :] = jnp.zeros((8, ubbc.shape[1]), F32)
        state[...] = jnp.zeros(state.shape, F32)

    def conv_silu(u_ref, ub, cw_ref, cb_ref):
        ub[8:8 + Q, :] = u_ref[0].astype(F32)
        acc = cb_ref[...]
        for kk in range(SSD_CONV):
            off = 8 - (SSD_CONV - 1) + kk
            acc = acc + cw_ref[kk:kk + 1, :] * ub[off:off + Q, :]
        ub[0:8, :] = ub[Q:Q + 8, :]
        return _silu(acc)

    xs = conv_silu(xs_ref, ubx, cwx_ref, cbx_ref)
    bcm = conv_silu(bc_ref, ubbc, cwbc_ref, cbbc_ref)
    GN = SSD_G * SSD_N
    bm_f, cm = bcm[:, :GN], bcm[:, GN:].astype(BF16)

    dt = _softplus(dt_ref[0].astype(F32) + dtb_ref[...])
    a_neg = -jnp.exp(alog_ref[...])
    dA = dt * a_neg

    row = lax.broadcasted_iota(I32, (Q, Q), 0)
    col = lax.broadcasted_iota(I32, (Q, Q), 1)
    causal = row >= col
    tri = jnp.where(causal, 1.0, 0.0).astype(BF16)
    acs = _dot_exact_lhs(tri, dA)
    acs_t = acs.T
    acs_last = acs[Q - 1:Q, :]
    exp_a = jnp.exp(acs)
    decay_st = jnp.exp(acs_last - acs)

    expand = exp_ref[...]
    dt_e = _dot_exact_rhs(dt, expand)
    dtd_e = _dot_exact_rhs(dt * decay_st, expand)
    expa_e = _dot_exact_rhs(exp_a, expand)

    x_dt = (xs * dt_e).astype(BF16)
    x_dd = (xs * dtd_e).astype(BF16)

    lane = lax.broadcasted_iota(I32, (Q, LANES), 1)
    lo_half = lane < SSD_P
    HG = SSD_HEADS // SSD_G
    GW = HG * SSD_P
    y_groups = []
    for g in range(SSD_G):
        cg = cm[:, g * SSD_N:(g + 1) * SSD_N]
        bg_f = bm_f[:, g * SSD_N:(g + 1) * SSD_N]
        bg = bg_f.astype(BF16)
        scores = _dot_nt(cg, bg)
        pieces = []
        for p in range(HG // 2):
            h0 = g * HG + 2 * p
            xp = x_dt[:, h0 * SSD_P:h0 * SSD_P + LANES]
            acc = None
            for s in range(2):
                h = h0 + s
                seg = acs[:, h:h + 1] - acs_t[h:h + 1, :]
                lmat = jnp.exp(jnp.where(causal, seg, -jnp.inf))
                m = (scores * lmat).astype(BF16)
                xh = jnp.where(lo_half if s == 0 else jnp.logical_not(lo_half), xp, jnp.zeros_like(xp))
                part = _dot(m, xh)
                acc = part if acc is None else acc + part
            pieces.append(acc)
        y_diag = jnp.concatenate(pieces, axis=1)
        st_old = state[g]
        y_off = _dot(cg, st_old.astype(BF16)) * expa_e[:, g * GW:(g + 1) * GW]
        st_new = _dot(bg_f.T.astype(BF16), x_dd[:, g * GW:(g + 1) * GW])
        state[g] = st_old * expa_e[Q - 1:Q, g * GW:(g + 1) * GW] + st_new
        y_groups.append(y_diag + y_off)

    y = jnp.concatenate(y_groups, axis=1) + xs * dskip_ref[...]
    y = y * _silu(z_ref[0].astype(F32))
    outs = []
    for g in range(SSD_G):
        yg = y[:, g * GW:(g + 1) * GW]
        ms = jnp.mean(yg * yg, axis=-1, keepdims=True)
        outs.append(yg * lax.rsqrt(ms + EPS) * nw_ref[:, g * GW:(g + 1) * GW])
    o_ref[0] = jnp.concatenate(outs, axis=1).astype(BF16)


def _ssd(pa3, ps3, conv_w, conv_b, dt_bias, a_log, d_skip, norm_w):
    B, L, _ = pa3.shape
    Q, DI, BCW = SSD_Q, SSD_INNER, 2 * SSD_G * SSD_N
    padl = lambda a: jnp.pad(a.reshape(1, -1), ((0, 0), (0, LANES - a.shape[-1])))
    expand = (np.arange(LANES)[:, None] == (np.arange(DI)[None, :] // SSD_P)).astype(np.float32)
    const = lambda shape: pl.BlockSpec(shape, lambda b, c: (0,) * len(shape))
    return pl.pallas_call(
        _ssd_body,
        out_shape=jax.ShapeDtypeStruct((B, L, DI), BF16),
        grid=(B, L // Q),
        in_specs=[pl.BlockSpec((1, Q, DI), lambda b, c: (b, c, A_Z // DI)),
                  pl.BlockSpec((1, Q, DI), lambda b, c: (b, c, A_XS // DI)),
                  pl.BlockSpec((1, Q, BCW), lambda b, c: (b, c, A_BC // BCW)),
                  pl.BlockSpec((1, Q, LANES), lambda b, c: (b, c, 0)),
                  const((SSD_CONV, DI)), const((SSD_CONV, BCW)), const((1, DI)), const((1, BCW)),
                  const((1, LANES)), const((1, LANES)), const((1, DI)), const((1, DI)),
                  const((LANES, DI))],
        out_specs=pl.BlockSpec((1, Q, DI), lambda b, c: (b, c, 0)),
        scratch_shapes=[pltpu.VMEM((Q + 8, DI), F32), pltpu.VMEM((Q + 8, BCW), F32),
                        pltpu.VMEM((SSD_G, SSD_N, DI // SSD_G), F32)],
        compiler_params=_cparams(("arbitrary", "arbitrary")),
        name="ssd",
    )(pa3, pa3, pa3, ps3, conv_w[:, :DI], conv_w[:, DI:], conv_b[:DI].reshape(1, DI),
      conv_b[DI:].reshape(1, BCW), padl(dt_bias), padl(a_log), jnp.repeat(d_skip, SSD_P).reshape(1, DI),
      norm_w.reshape(1, DI), jnp.asarray(expand, BF16))


GLA_RB = 256


def _gla_body(q_ref, k_ref, v_ref, g_ref, gk_ref, w_ref, b_ref, nw_ref, o_ref, st):
    RB, C = GLA_RB, GLA_C

    @pl.when(pl.program_id(1) == 0)
    def _():
        st[...] = jnp.zeros(st.shape, F32)

    w_hi, w_mid, w_lo = _split3(w_ref[...])
    gk = gk_ref[0]
    pre = _dot(gk, w_hi) + _dot(gk, w_mid) + _dot(gk, w_lo) + b_ref[...]
    log_a = (jnp.minimum(pre, 0.0) - jnp.log1p(jnp.exp(-jnp.abs(pre)))) * (1.0 / GLA_GATE_NORM)

    row = lax.broadcasted_iota(I32, (RB, RB), 0)
    col = lax.broadcasted_iota(I32, (RB, RB), 1)
    blocktri = jnp.where((row // C == col // C) & (row >= col), 1.0, 0.0).astype(BF16)
    bcs_all = _dot_exact_lhs(blocktri, log_a)

    r64 = lax.broadcasted_iota(I32, (C, C), 0)
    c64 = lax.broadcasted_iota(I32, (C, C), 1)
    causal = r64 >= c64
    scale = GLA_DK ** -0.5

    for s in range(RB // C):
        rs = slice(s * C, (s + 1) * C)
        bcs = bcs_all[rs, :]
        last = bcs[C - 1:C, :]
        qf = q_ref[0, rs, :].astype(F32)
        kf = k_ref[0, rs, :].astype(F32)
        q_dec = (qf * scale * jnp.exp(bcs)).astype(BF16)
        k_inv = (kf * jnp.exp(-bcs)).astype(BF16)
        k_st = (kf * jnp.exp(last - bcs)).astype(BF16)
        cdec = jnp.exp(last)
        for h in range(GLA_H):
            ks = slice(h * GLA_DK, (h + 1) * GLA_DK)
            vs = slice(h * GLA_DV, (h + 1) * GLA_DV)
            vh = v_ref[0, rs, vs]
            attn = jnp.where(causal, _dot_nt(q_dec[:, ks], k_inv[:, ks]), 0.0)
            st_h = st[h]
            o = _dot(attn.astype(BF16), vh) + _dot_nt(q_dec[:, ks], st_h.astype(BF16))
            st[h] = st_h * cdec[:, ks] + _dot_tn(vh, k_st[:, ks])
            ms = jnp.mean(o * o, axis=-1, keepdims=True)
            o = o * lax.rsqrt(ms + EPS) * nw_ref[...]
            o_ref[0, rs, vs] = (o * _silu(g_ref[0, rs, vs].astype(F32))).astype(BF16)


def _gla(pb3, ps3, w_gk2, b_gk2, norm_w):
    B, L, _ = pb3.shape
    RB, KD, VD = GLA_RB, GLA_H * GLA_DK, GLA_H * GLA_DV
    w_pad = jnp.pad(w_gk2, ((0, LANES - w_gk2.shape[0]), (0, 0)))
    const = lambda shape: pl.BlockSpec(shape, lambda b, c: (0,) * len(shape))
    return pl.pallas_call(
        _gla_body,
        out_shape=jax.ShapeDtypeStruct((B, L, VD), BF16),
        grid=(B, L // RB),
        in_specs=[pl.BlockSpec((1, RB, KD), lambda b, c: (b, c, B_Q // KD)),
                  pl.BlockSpec((1, RB, KD), lambda b, c: (b, c, B_K // KD)),
                  pl.BlockSpec((1, RB, VD), lambda b, c: (b, c, B_V // VD)),
                  pl.BlockSpec((1, RB, VD), lambda b, c: (b, c, B_GO // VD)),
                  pl.BlockSpec((1, RB, LANES), lambda b, c: (b, c, 1)),
                  const((LANES, KD)), const((1, KD)), const((1, GLA_DV))],
        out_specs=pl.BlockSpec((1, RB, VD), lambda b, c: (b, c, 0)),
        scratch_shapes=[pltpu.VMEM((GLA_H, GLA_DV, GLA_DK), F32)],
        compiler_params=_cparams(("arbitrary", "arbitrary")),
        name="gla",
    )(pb3, pb3, pb3, pb3, ps3, w_pad, b_gk2.reshape(1, KD), norm_w.reshape(1, GLA_DV))


def _merge_body(ys_ref, yg_ref, gs_ref, gg_ref, ws_ref, wg_ref, o_ref):
    a = _dot(ys_ref[...], ws_ref[...])
    b = _dot(yg_ref[...], wg_ref[...])
    m = jax.nn.sigmoid(gs_ref[...].astype(F32)) * a + jax.nn.sigmoid(gg_ref[...].astype(F32)) * b
    o_ref[...] = m.astype(BF16)


def _merge(y_ssd, y_gla, proj_c, w_ssd_out, w_gla_out):
    T, D = y_ssd.shape
    tm, tn = 512, 1024
    return pl.pallas_call(
        _merge_body,
        out_shape=jax.ShapeDtypeStruct((T, D), BF16),
        grid=(D // tn, T // tm),
        in_specs=[pl.BlockSpec((tm, D), lambda j, i: (i, 0)),
                  pl.BlockSpec((tm, D), lambda j, i: (i, 0)),
                  pl.BlockSpec((tm, tn), lambda j, i: (i, C_GS // tn + j)),
                  pl.BlockSpec((tm, tn), lambda j, i: (i, C_GG // tn + j)),
                  pl.BlockSpec((D, tn), lambda j, i: (0, j)),
                  pl.BlockSpec((D, tn), lambda j, i: (0, j))],
        out_specs=pl.BlockSpec((tm, tn), lambda j, i: (i, j)),
        compiler_params=_cparams(("arbitrary", "arbitrary")),
        name="merge",
    )(y_ssd, y_gla, proj_c, proj_c, w_ssd_out, w_gla_out)


def _pack_bf16_pair(a, b):
    ua = lax.bitcast_convert_type(a.astype(BF16).astype(F32), U32)
    ub = lax.bitcast_convert_type(b.astype(BF16).astype(F32), U32)
    return (ua & jnp.uint32(0xFFFF0000)) | (ub >> 16)


def _unpack_bf16_pair(w):
    a = lax.bitcast_convert_type(w & jnp.uint32(0xFFFF0000), F32)
    b = lax.bitcast_convert_type(w << 16, F32)
    return a, b


def _post_body(m_ref, x_ref, g1_ref, nw_ref, sc_ref, sh_ref, wo_ref, wr_ref, br_ref,
               h_ref, n2p_ref, tw_ref, ti_ref):
    D = x_ref.shape[1]
    h = x_ref[...] + g1_ref[0] * _dot(m_ref[...], wo_ref[...])
    h_ref[...] = h
    ms = jnp.mean(h * h, axis=-1, keepdims=True)
    n2 = h * lax.rsqrt(ms + EPS) * nw_ref[...] * (1.0 + sc_ref[0]) + sh_ref[0]
    n2p_ref[...] = _pack_bf16_pair(n2[:, :D // 2], n2[:, D // 2:])

    n_hi = n2.astype(BF16)
    n_lo = (n2 - n_hi.astype(F32)).astype(BF16)
    wr = wr_ref[...]
    w_hi = wr.astype(BF16)
    w_lo = (wr - w_hi.astype(F32)).astype(BF16)
    logits = _dot(n_hi, w_hi) + _dot(n_hi, w_lo) + _dot(n_lo, w_hi) + br_ref[...]

    lane = lax.broadcasted_iota(I32, logits.shape, 1)
    cur = jnp.where(lane < N_EXPERTS, logits, -jnp.inf)
    vals, idxs = [], []
    for _ in range(TOP_K):
        mx = jnp.max(cur, axis=-1, keepdims=True)
        ix = jnp.min(jnp.where(cur == mx, lane, LANES), axis=-1, keepdims=True)
        vals.append(mx)
        idxs.append(ix)
        cur = jnp.where(lane == ix, -jnp.inf, cur)
    es = [jnp.exp(v - vals[0]) for v in vals]
    denom = es[0] + es[1] + es[2] + es[3]
    tw = jnp.zeros(logits.shape, F32)
    ti = jnp.zeros(logits.shape, I32)
    for kk in range(TOP_K):
        tw = jnp.where(lane == kk, es[kk] / denom, tw)
        ti = jnp.where(lane == kk, idxs[kk], ti)
    tw_ref[...] = tw
    ti_ref[...] = ti


def _post(merged, x2d, g1, norm_w, sc, sh, w_out, w_router, b_router, L):
    T, D = x2d.shape
    tm = 512
    tpb = L // tm
    wr = jnp.pad(w_router, ((0, 0), (0, LANES - w_router.shape[1])))
    br = jnp.pad(b_router.reshape(1, -1), ((0, 0), (0, LANES - b_router.shape[0])))
    row = lambda w: pl.BlockSpec((tm, w), lambda i: (i, 0))
    per_b = pl.BlockSpec((1, 1, D), lambda i: (i // tpb, 0, 0))
    const = lambda shape: pl.BlockSpec(shape, lambda i: (0,) * len(shape))
    return pl.pallas_call(
        _post_body,
        out_shape=(jax.ShapeDtypeStruct((T, D), F32), jax.ShapeDtypeStruct((T, D // 2), U32),
                   jax.ShapeDtypeStruct((T, LANES), F32), jax.ShapeDtypeStruct((T, LANES), I32)),
        grid=(T // tm,),
        in_specs=[row(D), row(D), per_b, const((1, D)), per_b, per_b, const((D, D)), const((D, LANES)),
                  const((1, LANES))],
        out_specs=(row(D), row(D // 2), row(LANES), row(LANES)),
        compiler_params=_cparams(("arbitrary",)),
        name="post",
    )(merged, x2d, g1, norm_w, sc, sh, w_out, wr, br)


def _routing(top_idx, n_rows, n_sb):
    T = top_idx.shape[0]
    A = T * TOP_K
    flat_e = top_idx.reshape(A)
    onehot = (flat_e[:, None] == jnp.arange(N_EXPERTS, dtype=I32)[None, :]).astype(I32)
    csum = jnp.cumsum(onehot, axis=0)
    rank = jnp.sum(onehot * csum, axis=1) - 1
    counts = csum[-1]
    padded = (counts + MOE_SUB - 1) // MOE_SUB * MOE_SUB
    pend = jnp.cumsum(padded)
    pstart = pend - padded
    dest = (pstart[flat_e] + rank).astype(I32)

    nsb = (padded + MOE_TM - 1) // MOE_TM
    sb_end = jnp.cumsum(nsb)
    sb_first = sb_end - nsb
    slot = (sb_first[flat_e] + rank // MOE_TM) * MOE_TM + rank % MOE_TM
    sb_tok = jnp.zeros((n_sb * MOE_TM,), I32).at[slot].set(jnp.arange(A, dtype=I32) // TOP_K)

    total = sb_end[-1]
    i = jnp.arange(n_sb, dtype=I32)
    e_of = jnp.minimum(jnp.searchsorted(sb_end, i, side="right"), N_EXPERTS - 1).astype(I32)
    local = i - sb_first[e_of]
    valid = i < total
    last_e = e_of[jnp.maximum(total - 1, 0)]
    sb_e = jnp.where(valid, e_of, last_e).astype(I32)
    sb_start = jnp.where(valid, pstart[e_of] + local * MOE_TM, 0).astype(I32)
    sb_start = jnp.concatenate([sb_start, pend[-1:].astype(I32)])
    sb_n = jnp.where(valid, jnp.clip(padded[e_of] - local * MOE_TM, 0, MOE_TM), 0).astype(I32)
    return dest, sb_tok.reshape(n_sb, 1, MOE_TM), sb_e, sb_start, sb_n


def _deinterleave(gu):
    rows, two_w = gu.shape
    lane = lax.broadcasted_iota(I32, (rows, LANES), 1)
    idx_e = (2 * lane) % LANES
    idx_o = idx_e + 1
    first = lane < LANES // 2
    gates, ups = [], []
    for p in range(two_w // (2 * LANES)):
        a = gu[:, (2 * p) * LANES:(2 * p + 1) * LANES]
        b = gu[:, (2 * p + 1) * LANES:(2 * p + 2) * LANES]
        gates.append(jnp.where(first, jnp.take_along_axis(a, idx_e, axis=1), jnp.take_along_axis(b, idx_e, axis=1)))
        ups.append(jnp.where(first, jnp.take_along_axis(a, idx_o, axis=1), jnp.take_along_axis(b, idx_o, axis=1)))
    return jnp.concatenate(gates, axis=1), jnp.concatenate(ups, axis=1)


MOE_DMA_UNROLL = 16


def _moe_body(sbe_ref, sbs_ref, sbn_ref, tok_ref, tokn_ref, x_hbm, wgu_ref, bgu_ref, wd_ref, bd_ref, y_hbm,
              xs_buf, xb_buf, acc, ystage, wg_s, wd_s, sem_in, sem_out):
    i = pl.program_id(0)
    hc = pl.program_id(1)
    n_sb = pl.num_programs(0)
    n_hc = pl.num_programs(1)
    SUB = MOE_SUB
    half = xs_buf.shape[1]
    nblk = sbn_ref[i] // SUB
    start_row = sbs_ref[i]

    def row_copy(t_ref, r):
        return pltpu.make_async_copy(x_hbm.at[pl.ds(t_ref[0, 0, r], 1)], xs_buf.at[pl.ds(r, 1)], sem_in)

    def start_in(t_ref, n_rows):
        def body(g, carry):
            for u in range(MOE_DMA_UNROLL):
                row_copy(t_ref, g * MOE_DMA_UNROLL + u).start()
            return carry

        lax.fori_loop(0, n_rows // MOE_DMA_UNROLL, body, 0)

    @pl.when(hc == 0)
    def _():
        @pl.when(i == 0)
        def _():
            start_in(tok_ref, sbn_ref[0])

        def wait_in(g, carry):
            for u in range(MOE_DMA_UNROLL):
                row_copy(tok_ref, g * MOE_DMA_UNROLL + u).wait()
            return carry

        lax.fori_loop(0, sbn_ref[i] // MOE_DMA_UNROLL, wait_in, 0)

        def convert(r, carry):
            rows = pl.ds(pl.multiple_of(r * SUB, SUB), SUB)
            a, b = _unpack_bf16_pair(xs_buf[rows, :])
            xb_buf[rows, 0:half] = a.astype(BF16)
            xb_buf[rows, half:2 * half] = b.astype(BF16)
            return carry

        lax.fori_loop(0, nblk, convert, 0)

        @pl.when(i + 1 < n_sb)
        def _():
            start_in(tokn_ref, sbn_ref[i + 1])

    def accumulate(blocks, first):
        rows = [pl.ds(pl.multiple_of(r * SUB, SUB), SUB) for r in blocks]
        gus = [_dot(xb_buf[rw, :], wg_s[...]) + bgu_ref[0] for rw in rows]
        hs = []
        for gu in gus:
            gate, up = _deinterleave(gu)
            gate = jnp.minimum(gate, SWIGLU_LIMIT)
            up = jnp.clip(up, -SWIGLU_LIMIT, SWIGLU_LIMIT)
            hs.append((gate * jax.nn.sigmoid(gate * SWIGLU_ALPHA) * (up + 1.0)).astype(BF16))
        for rw, hidden in zip(rows, hs):
            v = _dot(hidden, wd_s[...])
            if first:
                acc[rw, :] = v + bd_ref[0]
            else:
                acc[rw, :] += v

    def run(first):
        def quad(p, carry):
            accumulate([4 * p, 4 * p + 1, 4 * p + 2, 4 * p + 3], first)
            return carry

        lax.fori_loop(0, nblk // 4, quad, 0)
        done = nblk // 4 * 4

        @pl.when(nblk - done >= 2)
        def _():
            accumulate([done, done + 1], first)

        @pl.when(nblk % 2 == 1)
        def _():
            accumulate([nblk - 1], first)

    @pl.when(nblk > 0)
    def _():
        wg_s[...] = wgu_ref[0].astype(BF16)
        wd_s[...] = wd_ref[0].astype(BF16)

        @pl.when(hc == 0)
        def _():
            run(True)

        @pl.when(hc != 0)
        def _():
            run(False)

    @pl.when(hc == n_hc - 1)
    def _():
        def out_copy(r, slot):
            dst = pl.multiple_of(start_row + r * SUB, SUB)
            return pltpu.make_async_copy(ystage.at[slot], y_hbm.at[pl.ds(dst, SUB)], sem_out.at[slot])

        def body(r, carry):
            slot = r % 2

            @pl.when(r >= 2)
            def _():
                out_copy(r - 2, slot).wait()

            rows = pl.ds(pl.multiple_of(r * SUB, SUB), SUB)
            v = acc[rows, :]
            ystage[slot] = _pack_bf16_pair(v[:, :half], v[:, half:])
            out_copy(r, slot).start()
            return carry

        lax.fori_loop(0, nblk, body, 0)

        def drain(r, carry):
            out_copy(r, r % 2).wait()
            return carry

        lax.fori_loop(jnp.maximum(nblk - 2, 0), nblk, drain, 0)

    @pl.when((i == n_sb - 1) & (hc == n_hc - 1))
    def _():
        ystage[0] = jnp.zeros(ystage.shape[1:], U32)

        def tail_copy(b):
            return pltpu.make_async_copy(ystage.at[0], y_hbm.at[pl.ds(pl.multiple_of(b * SUB, SUB), SUB)], sem_out.at[0])

        def start(b, carry):
            tail_copy(b).start()
            return carry

        def wait(b, carry):
            tail_copy(b).wait()
            return carry

        first, stop = sbs_ref[n_sb] // SUB, y_hbm.shape[0] // SUB
        lax.fori_loop(first, stop, start, 0)
        lax.fori_loop(first, stop, wait, 0)


def _moe(n2p, n_rows, sb_tok, sb_e, sb_start, sb_n, w_gate_up, b_gate_up, w_down, b_down):
    half = n2p.shape[1]
    E, D, H2 = w_gate_up.shape
    n_sb = sb_e.shape[0]
    TH = MOE_TH
    n_hc = (H2 // 2) // TH
    hc_eff = lambda i, hc, sbn: jnp.where(sbn[i] > 0, hc, n_hc - 1)
    grid_spec = pltpu.PrefetchScalarGridSpec(
        num_scalar_prefetch=3,
        grid=(n_sb, n_hc),
        in_specs=[pl.BlockSpec((1, 1, MOE_TM), lambda i, hc, sbe, sbs, sbn: (i, 0, 0), memory_space=pltpu.SMEM),
                  pl.BlockSpec((1, 1, MOE_TM), lambda i, hc, sbe, sbs, sbn: (jnp.minimum(i + 1, n_sb - 1), 0, 0),
                               memory_space=pltpu.SMEM),
                  pl.BlockSpec(memory_space=pl.ANY),
                  pl.BlockSpec((1, D, 2 * TH), lambda i, hc, sbe, sbs, sbn: (sbe[i], 0, hc_eff(i, hc, sbn))),
                  pl.BlockSpec((1, 1, 2 * TH), lambda i, hc, sbe, sbs, sbn: (sbe[i], 0, hc_eff(i, hc, sbn))),
                  pl.BlockSpec((1, TH, D), lambda i, hc, sbe, sbs, sbn: (sbe[i], hc_eff(i, hc, sbn), 0)),
                  pl.BlockSpec((1, 1, D), lambda i, hc, sbe, sbs, sbn: (sbe[i], 0, 0))],
        out_specs=pl.BlockSpec(memory_space=pl.ANY),
        scratch_shapes=[pltpu.VMEM((MOE_TM, half), U32), pltpu.VMEM((MOE_TM, 2 * half), BF16),
                        pltpu.VMEM((MOE_TM, D), F32), pltpu.VMEM((2, MOE_SUB, half), U32),
                        pltpu.VMEM((D, 2 * TH), BF16), pltpu.VMEM((TH, D), BF16),
                        pltpu.SemaphoreType.DMA(()), pltpu.SemaphoreType.DMA((2,))],
    )
    return pl.pallas_call(
        _moe_body,
        out_shape=jax.ShapeDtypeStruct((n_rows, half), U32),
        grid_spec=grid_spec,
        compiler_params=_cparams(("arbitrary", "arbitrary")),
        name="moe",
    )(sb_e, sb_start, sb_n, sb_tok, sb_tok, n2p, w_gate_up, b_gate_up.reshape(E, 1, H2), w_down, b_down.reshape(E, 1, D))


COMBINE_TM = 256


def _combine_body(dest_ref, h_ref, tw_ref, g2_ref, nw_ref, y_hbm, o_ref, ybuf, sem):
    tm = COMBINE_TM
    n = TOP_K * tm

    def copy(j):
        return pltpu.make_async_copy(y_hbm.at[pl.ds(dest_ref[0, 0, j], 1)], ybuf.at[pl.ds(j, 1)], sem)

    def start(j, carry):
        copy(j).start()
        return carry

    def wait(j, carry):
        copy(j).wait()
        return carry

    lax.fori_loop(0, n, start, 0, unroll=8)
    lax.fori_loop(0, n, wait, 0, unroll=8)

    tw = tw_ref[...]
    lo = hi = None
    for kk in range(TOP_K):
        a, b = _unpack_bf16_pair(ybuf[kk * tm:(kk + 1) * tm, :])
        wk = tw[:, kk:kk + 1]
        lo = wk * a if lo is None else lo + wk * a
        hi = wk * b if hi is None else hi + wk * b
    ffn = jnp.concatenate([lo, hi], axis=1)
    h = h_ref[...] + g2_ref[0] * ffn
    ms = jnp.mean(h * h, axis=-1, keepdims=True)
    o_ref[...] = h * lax.rsqrt(ms + EPS) * nw_ref[...]


def _combine(h1, y_sorted, dest, top_w, g2, final_norm_w, L):
    T, D = h1.shape
    tm = COMBINE_TM
    tiles = T // tm
    tpb = L // tm
    dest_tiles = dest.reshape(tiles, tm, TOP_K).transpose(0, 2, 1).reshape(tiles, 1, TOP_K * tm)
    return pl.pallas_call(
        _combine_body,
        out_shape=jax.ShapeDtypeStruct((T, D), F32),
        grid=(tiles,),
        in_specs=[pl.BlockSpec((1, 1, TOP_K * tm), lambda i: (i, 0, 0), memory_space=pltpu.SMEM),
                  pl.BlockSpec((tm, D), lambda i: (i, 0)),
                  pl.BlockSpec((tm, LANES), lambda i: (i, 0)),
                  pl.BlockSpec((1, 1, D), lambda i: (i // tpb, 0, 0)),
                  pl.BlockSpec((1, D), lambda i: (0, 0)),
                  pl.BlockSpec(memory_space=pl.ANY)],
        out_specs=pl.BlockSpec((tm, D), lambda i: (i, 0)),
        scratch_shapes=[pltpu.VMEM((TOP_K * tm, D // 2), U32), pltpu.SemaphoreType.DMA(())],
        compiler_params=_cparams(("arbitrary",)),
        name="combine",
    )(dest_tiles, h1, top_w, g2, final_norm_w.reshape(1, D), y_sorted)


def kernel(x, c, w_ada, b_ada, norm1_w, w_in, ssd_conv_w, ssd_conv_b, ssd_dt_bias, ssd_a_log, ssd_d_skip, ssd_norm_w, gla_w_gk2, gla_b_gk2, gla_norm_w, w_ssd_out, w_gla_out, w_out, norm2_w, w_router, b_router, w_gate_up, b_gate_up, w_down, b_down, final_norm_w):
    B, L, D = x.shape
    T = B * L
    assert D == D_MODEL and w_ada.shape[0] == 1
    x2d = x.reshape(T, D)

    mod = _ada(c, w_ada[0], b_ada[0])
    sh1, sc1, g1, sh2, sc2, g2 = [mod[:, i * D:(i + 1) * D].reshape(B, 1, D) for i in range(6)]

    w_t = w_in[0].T
    pad = lambda a: jnp.pad(a, ((0, LANES - a.shape[0]), (0, 0)))
    w_s = jnp.concatenate([pad(w_t[SRC_DT[0]:SRC_DT[1]]), pad(w_t[SRC_GK[0]:SRC_GK[1]])], axis=0)
    proj_a, n1 = _inproj_a(x2d, norm1_w[0].reshape(1, D), sc1, sh1, w_t, L)
    proj_b = _matmul(n1, w_t, SRC_B, PROJ_TN, "inproj_b")
    proj_c = _matmul(n1, w_t, SRC_C, PROJ_TN, "inproj_c")
    proj_s = _matmul(n1, w_s, (0, 2 * LANES), 2 * LANES, "inproj_s")
    pa3, pb3, ps3 = [p.reshape(B, L, p.shape[1]) for p in (proj_a, proj_b, proj_s)]
    y_ssd = _ssd(pa3, ps3, ssd_conv_w[0], ssd_conv_b[0], ssd_dt_bias[0], ssd_a_log[0], ssd_d_skip[0], ssd_norm_w[0])
    y_gla = _gla(pb3, ps3, gla_w_gk2[0], gla_b_gk2[0], gla_norm_w[0])
    merged = _merge(y_ssd.reshape(T, D), y_gla.reshape(T, D), proj_c, w_ssd_out[0].astype(BF16), w_gla_out[0].astype(BF16))
    h1, n2p, top_w, top_i = _post(merged, x2d, g1, norm2_w[0].reshape(1, D), sc2, sh2, w_out[0].astype(BF16),
                                  w_router[0], b_router[0], L)

    A = T * TOP_K
    n_rows = A + N_EXPERTS * MOE_SUB
    n_sb = N_EXPERTS + n_rows // MOE_TM
    dest, sb_tok, sb_e, sb_start, sb_n = _routing(top_i[:, :TOP_K], n_rows, n_sb)
    y_sorted = _moe(n2p, n_rows, sb_tok, sb_e, sb_start, sb_n, w_gate_up[0], b_gate_up[0], w_down[0], b_down[0])
    out = _combine(h1, y_sorted, dest, top_w, g2, final_norm_w, L)
    return out.reshape(B, L, D)
```

```python
import functools

import jax
import jax.numpy as jnp
import numpy as np
from jax import lax
from jax.experimental import pallas as pl
from jax.experimental.pallas import tpu as pltpu

F32 = jnp.float32
BF16 = jnp.bfloat16
U32 = jnp.uint32
I32 = jnp.int32

EPS = 1e-6
LANES = 128
VMEM_LIMIT = 56 * 1024 * 1024

D_MODEL = 2048
SSD_HEADS, SSD_P, SSD_G, SSD_N, SSD_CONV, SSD_Q = 32, 64, 4, 128, 4, 256
SSD_INNER = SSD_HEADS * SSD_P
SSD_TAIL = 16
GLA_H, GLA_DK, GLA_DV, GLA_RANK, GLA_C = 4, 256, 512, 16, 64
GLA_GATE_NORM = 16.0
N_EXPERTS, TOP_K = 32, 4
SWIGLU_LIMIT, SWIGLU_ALPHA = 7.0, 1.702

SRC_A = (0, 5120)
SRC_DT = (5120, 5152)
SRC_B = (5152, 11296)
SRC_GK = (11296, 11312)
SRC_C = (11312, 15408)
A_Z, A_XS, A_BC = 0, 2048, 4096
B_Q, B_K, B_V, B_GO = 0, 1024, 2048, 4096
C_GS, C_GG = 0, 2048
PROJ_TN = 1024

MOE_SUB = 256
MOE_TM = 2048
MOE_TH = 256


def _cparams(sem, vmem=VMEM_LIMIT):
    return pltpu.CompilerParams(dimension_semantics=sem, vmem_limit_bytes=vmem)


def _split3(x):
    hi = x.astype(BF16)
    r1 = x - hi.astype(F32)
    mid = r1.astype(BF16)
    lo = (r1 - mid.astype(F32)).astype(BF16)
    return hi, mid, lo


def _dot(a, b):
    return jnp.dot(a, b, preferred_element_type=F32)


def _dot_nt(a, b):
    return lax.dot_general(a, b, (((1,), (1,)), ((), ())), preferred_element_type=F32)


def _dot_tn(a, b):
    return lax.dot_general(a, b, (((0,), (0,)), ((), ())), preferred_element_type=F32)


def _dot_exact_lhs(a01, x):
    hi, mid, lo = _split3(x)
    return _dot(a01, hi) + _dot(a01, mid) + _dot(a01, lo)


def _dot_hilo_rhs(x, b01):
    hi = x.astype(BF16)
    lo = (x - hi.astype(F32)).astype(BF16)
    return _dot(hi, b01) + _dot(lo, b01)


def _silu(x):
    return x * jax.nn.sigmoid(x)


def _softplus(x):
    return jnp.maximum(x, 0.0) + jnp.log1p(jnp.exp(-jnp.abs(x)))


def _ada_body(ct_ref, w_ref, b_ref, o_ref):
    ct = ct_ref[...]
    cs = _silu(ct)
    w = w_ref[...]
    for m in range(ct.shape[1]):
        o_ref[m:m + 1, :] = jnp.sum(w * cs[:, m:m + 1], axis=0, keepdims=True) + b_ref[...]


def _ada(c, w_ada, b_ada):
    B, D = c.shape
    N = w_ada.shape[1]
    tn = 1024
    return pl.pallas_call(
        _ada_body,
        out_shape=jax.ShapeDtypeStruct((B, N), F32),
        grid=(N // tn,),
        in_specs=[pl.BlockSpec((D, B), lambda j: (0, 0)),
                  pl.BlockSpec((D, tn), lambda j: (0, j)),
                  pl.BlockSpec((1, tn), lambda j: (0, j))],
        out_specs=pl.BlockSpec((B, tn), lambda j: (0, j)),
        compiler_params=_cparams(("arbitrary",)),
        name="ada",
    )(c.T, w_ada, b_ada.reshape(1, N))


def _inproj_a_body(x_ref, nw_ref, sc_ref, sh_ref, w_ref, o_ref, n_ref):
    @pl.when(pl.program_id(1) == 0)
    def _():
        x = x_ref[...]
        ms = jnp.mean(x * x, axis=-1, keepdims=True)
        y = x * lax.rsqrt(ms + EPS) * nw_ref[...]
        n_ref[...] = (y * (1.0 + sc_ref[0]) + sh_ref[0]).astype(BF16)

    o_ref[...] = _dot_nt(n_ref[...], w_ref[...].astype(BF16)).astype(BF16)


def _inproj_a(x2d, norm_w, sc, sh, w_t, L):
    T, D = x2d.shape
    tm, tn = 1024, PROJ_TN // 2
    n_cols = SRC_A[1] - SRC_A[0]
    tiles_per_batch = L // tm
    return pl.pallas_call(
        _inproj_a_body,
        out_shape=(jax.ShapeDtypeStruct((T, n_cols), BF16), jax.ShapeDtypeStruct((T, D), BF16)),
        grid=(T // tm, n_cols // tn),
        in_specs=[pl.BlockSpec((tm, D), lambda i, j: (i, 0)),
                  pl.BlockSpec((1, D), lambda i, j: (0, 0)),
                  pl.BlockSpec((1, 1, D), lambda i, j: (i // tiles_per_batch, 0, 0)),
                  pl.BlockSpec((1, 1, D), lambda i, j: (i // tiles_per_batch, 0, 0)),
                  pl.BlockSpec((tn, D), lambda i, j: (SRC_A[0] // tn + j, 0))],
        out_specs=(pl.BlockSpec((tm, tn), lambda i, j: (i, j)), pl.BlockSpec((tm, D), lambda i, j: (i, 0))),
        compiler_params=_cparams(("arbitrary", "arbitrary")),
        name="inproj_a",
    )(x2d, norm_w, sc, sh, w_t)


def _matmul_body(a_ref, w_ref, o_ref):
    o_ref[...] = _dot_nt(a_ref[...], w_ref[...].astype(BF16)).astype(BF16)


def _matmul(a, w_t, rows, tn, name):
    T, D = a.shape
    N = rows[1] - rows[0]
    tm = 1024
    return pl.pallas_call(
        _matmul_body,
        out_shape=jax.ShapeDtypeStruct((T, N), BF16),
        grid=(T // tm, N // tn),
        in_specs=[pl.BlockSpec((tm, D), lambda i, j: (i, 0)),
                  pl.BlockSpec((pl.Element(tn), pl.Element(D)),
                               lambda i, j: ((rows[0] // 8 + j * (tn // 8)) * 8, 0))],
        out_specs=pl.BlockSpec((tm, tn), lambda i, j: (i, j)),
        compiler_params=_cparams(("arbitrary", "arbitrary")),
        name=name,
    )(a, w_t)


def _ssd_body(z_ref, xs_ref, bc_ref, dt_ref, cwx_ref, cwbc_ref, cbx_ref, cbbc_ref, dtb_ref, alog_ref,
              dskip_ref, nw_ref, exp_ref, sh_ref, o_ref, ubx, ubbc, state):
    Q = SSD_Q
    c = pl.program_id(1)

    @pl.when(c == 0)
    def _():
        ubx[...] = jnp.zeros(ubx.shape, F32)
        ubbc[...] = jnp.zeros(ubbc.shape, F32)
        state[...] = jnp.zeros(state.shape, F32)

    def conv_silu(u_ref, tail, cw_ref, cb_ref):
        u = u_ref[0]
        head = u_ref[0, 0:SSD_TAIL, :].astype(F32)[0:8, :]
        prev = tail[...]
        r8 = lax.broadcasted_iota(I32, head.shape, 0)
        w_last = cw_ref[SSD_CONV - 1:SSD_CONV, :]
        acc = cb_ref[...] + w_last * u.astype(F32)
        acc8 = cb_ref[...] + w_last * head
        for s in range(1, SSD_CONV):
            w_s = cw_ref[SSD_CONV - 1 - s:SSD_CONV - s, :]
            acc = acc + w_s * _dot(sh_ref[s - 1], u)
            acc8 = acc8 + w_s * jnp.where(r8 < s, pltpu.roll(prev, s, 0), pltpu.roll(head, s, 0))
        tail[...] = u_ref[0, Q - SSD_TAIL:Q, :].astype(F32)[SSD_TAIL - 8:, :]
        return _silu(jnp.concatenate([acc8, acc[8:, :]], axis=0))

    xs = conv_silu(xs_ref, ubx, cwx_ref, cbx_ref)
    bcm = conv_silu(bc_ref, ubbc, cwbc_ref, cbbc_ref)
    GN = SSD_G * SSD_N
    bm_f, cm = bcm[:, :GN], bcm[:, GN:].astype(BF16)

    dt = _softplus(dt_ref[0].astype(F32) + dtb_ref[...])
    a_neg = -jnp.exp(alog_ref[...])
    dA = dt * a_neg

    row = lax.broadcasted_iota(I32, (Q, Q), 0)
    col = lax.broadcasted_iota(I32, (Q, Q), 1)
    causal = row >= col
    tri = jnp.where(causal, 1.0, 0.0).astype(BF16)
    acs = _dot_exact_lhs(tri, dA)
    acs_t = acs.T
    acs_last = acs[Q - 1:Q, :]
    exp_a = jnp.exp(acs)
    decay_st = jnp.exp(acs_last - acs)

    expand = exp_ref[...]
    dt_e = _dot_hilo_rhs(dt, expand)
    dtd_e = _dot_hilo_rhs(dt * decay_st, expand)
    expa_e = _dot_hilo_rhs(exp_a, expand)

    x_dt = (xs * dt_e).astype(BF16)
    x_dd = (xs * dtd_e).astype(BF16)

    lane = lax.broadcasted_iota(I32, (Q, LANES), 1)
    lo_half = lane < SSD_P
    HG = SSD_HEADS // SSD_G
    GW = HG * SSD_P
    y_groups = []
    for g in range(SSD_G):
        cg = cm[:, g * SSD_N:(g + 1) * SSD_N]
        bg_f = bm_f[:, g * SSD_N:(g + 1) * SSD_N]
        bg = bg_f.astype(BF16)
        scores = _dot_nt(cg, bg)
        pieces = []
        for p in range(HG // 2):
            h0 = g * HG + 2 * p
            xp = x_dt[:, h0 * SSD_P:h0 * SSD_P + LANES]
            acc = None
            for s in range(2):
                h = h0 + s
                seg = acs[:, h:h + 1] - acs_t[h:h + 1, :]
                lmat = jnp.exp(jnp.where(causal, seg, -jnp.inf))
                m = (scores * lmat).astype(BF16)
                xh = jnp.where(lo_half if s == 0 else jnp.logical_not(lo_half), xp, jnp.zeros_like(xp))
                part = _dot(m, xh)
                acc = part if acc is None else acc + part
            pieces.append(acc)
        y_diag = jnp.concatenate(pieces, axis=1)
        st_old = state[g]
        y_off = _dot(cg, st_old.astype(BF16)) * expa_e[:, g * GW:(g + 1) * GW]
        st_new = _dot(bg_f.T.astype(BF16), x_dd[:, g * GW:(g + 1) * GW])
        state[g] = st_old * expa_e[Q - 1:Q, g * GW:(g + 1) * GW] + st_new
        y_groups.append(y_diag + y_off)

    y = jnp.concatenate(y_groups, axis=1) + xs * dskip_ref[...]
    y = y * _silu(z_ref[0].astype(F32))
    outs = []
    for g in range(SSD_G):
        yg = y[:, g * GW:(g + 1) * GW]
        ms = jnp.mean(yg * yg, axis=-1, keepdims=True)
        outs.append(yg * lax.rsqrt(ms + EPS) * nw_ref[:, g * GW:(g + 1) * GW])
    o_ref[0] = jnp.concatenate(outs, axis=1).astype(BF16)


def _ssd(pa3, ps3, conv_w, conv_b, dt_bias, a_log, d_skip, norm_w):
    B, L, _ = pa3.shape
    Q, DI, BCW = SSD_Q, SSD_INNER, 2 * SSD_G * SSD_N
    padl = lambda a: jnp.pad(a.reshape(1, -1), ((0, 0), (0, LANES - a.shape[-1])))
    expand = (np.arange(LANES)[:, None] == (np.arange(DI)[None, :] // SSD_P)).astype(np.float32)
    t = np.arange(Q)[:, None]
    shifts = np.stack([(t - s == np.arange(Q)[None, :]) for s in range(1, SSD_CONV)]).astype(np.float32)
    const = lambda shape: pl.BlockSpec(shape, lambda b, c: (0,) * len(shape))
    return pl.pallas_call(
        _ssd_body,
        out_shape=jax.ShapeDtypeStruct((B, L, DI), BF16),
        grid=(B, L // Q),
        in_specs=[pl.BlockSpec((1, Q, DI), lambda b, c: (b, c, A_Z // DI)),
                  pl.BlockSpec((1, Q, DI), lambda b, c: (b, c, A_XS // DI)),
                  pl.BlockSpec((1, Q, BCW), lambda b, c: (b, c, A_BC // BCW)),
                  pl.BlockSpec((1, Q, LANES), lambda b, c: (b, c, 0)),
                  const((SSD_CONV, DI)), const((SSD_CONV, BCW)), const((1, DI)), const((1, BCW)),
                  const((1, LANES)), const((1, LANES)), const((1, DI)), const((1, DI)),
                  const((LANES, DI)), const((SSD_CONV - 1, Q, Q))],
        out_specs=pl.BlockSpec((1, Q, DI), lambda b, c: (b, c, 0)),
        scratch_shapes=[pltpu.VMEM((8, DI), F32), pltpu.VMEM((8, BCW), F32),
                        pltpu.VMEM((SSD_G, SSD_N, DI // SSD_G), F32)],
        compiler_params=_cparams(("arbitrary", "arbitrary")),
        name="ssd",
    )(pa3, pa3, pa3, ps3, conv_w[:, :DI], conv_w[:, DI:], conv_b[:DI].reshape(1, DI),
      conv_b[DI:].reshape(1, BCW), padl(dt_bias), padl(a_log), jnp.repeat(d_skip, SSD_P).reshape(1, DI),
      norm_w.reshape(1, DI), jnp.asarray(expand, BF16), jnp.asarray(shifts, BF16))


GLA_RB = 256


def _gla_body(q_ref, k_ref, v_ref, g_ref, gk_ref, w_ref, b_ref, nw_ref, o_ref, st):
    RB, C = GLA_RB, GLA_C

    @pl.when(pl.program_id(1) == 0)
    def _():
        st[...] = jnp.zeros(st.shape, F32)

    w_hi, w_mid, w_lo = _split3(w_ref[...])
    gk = gk_ref[0]
    pre = _dot(gk, w_hi) + _dot(gk, w_mid) + _dot(gk, w_lo) + b_ref[...]
    log_a = (jnp.minimum(pre, 0.0) - jnp.log1p(jnp.exp(-jnp.abs(pre)))) * (1.0 / GLA_GATE_NORM)

    row = lax.broadcasted_iota(I32, (RB, RB), 0)
    col = lax.broadcasted_iota(I32, (RB, RB), 1)
    blocktri = jnp.where((row // C == col // C) & (row >= col), 1.0, 0.0).astype(BF16)
    bcs_all = _dot_exact_lhs(blocktri, log_a)

    r64 = lax.broadcasted_iota(I32, (C, C), 0)
    c64 = lax.broadcasted_iota(I32, (C, C), 1)
    causal = r64 >= c64
    scale = GLA_DK ** -0.5

    for s in range(RB // C):
        rs = slice(s * C, (s + 1) * C)
        bcs = bcs_all[rs, :]
        last = bcs[C - 1:C, :]
        qf = q_ref[0, rs, :].astype(F32)
        kf = k_ref[0, rs, :].astype(F32)
        q_dec = (qf * scale * jnp.exp(bcs)).astype(BF16)
        k_inv = (kf * jnp.exp(-bcs)).astype(BF16)
        k_st = (kf * jnp.exp(last - bcs)).astype(BF16)
        cdec = jnp.exp(last)
        for h in range(GLA_H):
            ks = slice(h * GLA_DK, (h + 1) * GLA_DK)
            vs = slice(h * GLA_DV, (h + 1) * GLA_DV)
            vh = v_ref[0, rs, vs]
            attn = jnp.where(causal, _dot_nt(q_dec[:, ks], k_inv[:, ks]), 0.0)
            st_h = st[h]
            o = _dot(attn.astype(BF16), vh) + _dot_nt(q_dec[:, ks], st_h.astype(BF16))
            st[h] = st_h * cdec[:, ks] + _dot_tn(vh, k_st[:, ks])
            ms = jnp.mean(o * o, axis=-1, keepdims=True)
            o = o * lax.rsqrt(ms + EPS) * nw_ref[...]
            o_ref[0, rs, vs] = (o * _silu(g_ref[0, rs, vs].astype(F32))).astype(BF16)


def _gla(pb3, ps3, w_gk2, b_gk2, norm_w):
    B, L, _ = pb3.shape
    RB, KD, VD = GLA_RB, GLA_H * GLA_DK, GLA_H * GLA_DV
    w_pad = jnp.pad(w_gk2, ((0, LANES - w_gk2.shape[0]), (0, 0)))
    const = lambda shape: pl.BlockSpec(shape, lambda b, c: (0,) * len(shape))
    return pl.pallas_call(
        _gla_body,
        out_shape=jax.ShapeDtypeStruct((B, L, VD), BF16),
        grid=(B, L // RB),
        in_specs=[pl.BlockSpec((1, RB, KD), lambda b, c: (b, c, B_Q // KD)),
                  pl.BlockSpec((1, RB, KD), lambda b, c: (b, c, B_K // KD)),
                  pl.BlockSpec((1, RB, VD), lambda b, c: (b, c, B_V // VD)),
                  pl.BlockSpec((1, RB, VD), lambda b, c: (b, c, B_GO // VD)),
                  pl.BlockSpec((1, RB, LANES), lambda b, c: (b, c, 1)),
                  const((LANES, KD)), const((1, KD)), const((1, GLA_DV))],
        out_specs=pl.BlockSpec((1, RB, VD), lambda b, c: (b, c, 0)),
        scratch_shapes=[pltpu.VMEM((GLA_H, GLA_DV, GLA_DK), F32)],
        compiler_params=_cparams(("arbitrary", "arbitrary")),
        name="gla",
    )(pb3, pb3, pb3, pb3, ps3, w_pad, b_gk2.reshape(1, KD), norm_w.reshape(1, GLA_DV))


def _merge_body(ys_ref, yg_ref, gs_ref, gg_ref, ws_ref, wg_ref, o_ref):
    a = _dot(ys_ref[...], ws_ref[...])
    b = _dot(yg_ref[...], wg_ref[...])
    m = jax.nn.sigmoid(gs_ref[...].astype(F32)) * a + jax.nn.sigmoid(gg_ref[...].astype(F32)) * b
    o_ref[...] = m.astype(BF16)


def _merge(y_ssd, y_gla, proj_c, w_ssd_out, w_gla_out):
    T, D = y_ssd.shape
    tm, tn = 512, 1024
    return pl.pallas_call(
        _merge_body,
        out_shape=jax.ShapeDtypeStruct((T, D), BF16),
        grid=(D // tn, T // tm),
        in_specs=[pl.BlockSpec((tm, D), lambda j, i: (i, 0)),
                  pl.BlockSpec((tm, D), lambda j, i: (i, 0)),
                  pl.BlockSpec((tm, tn), lambda j, i: (i, C_GS // tn + j)),
                  pl.BlockSpec((tm, tn), lambda j, i: (i, C_GG // tn + j)),
                  pl.BlockSpec((D, tn), lambda j, i: (0, j)),
                  pl.BlockSpec((D, tn), lambda j, i: (0, j))],
        out_specs=pl.BlockSpec((tm, tn), lambda j, i: (i, j)),
        compiler_params=_cparams(("arbitrary", "arbitrary")),
        name="merge",
    )(y_ssd, y_gla, proj_c, proj_c, w_ssd_out, w_gla_out)


def _pack_bf16_pair(a, b):
    ua = lax.bitcast_convert_type(a.astype(BF16).astype(F32), U32)
    ub = lax.bitcast_convert_type(b.astype(BF16).astype(F32), U32)
    return (ua & jnp.uint32(0xFFFF0000)) | (ub >> 16)


def _unpack_bf16_pair(w):
    a = lax.bitcast_convert_type(w & jnp.uint32(0xFFFF0000), F32)
    b = lax.bitcast_convert_type(w << 16, F32)
    return a, b


def _post_body(m_ref, x_ref, g1_ref, nw_ref, sc_ref, sh_ref, wo_ref, wr_ref, br_ref, tri_ref,
               h_ref, n2p_ref, tw_ref, ti_ref, cnt_ref):
    D = x_ref.shape[1]
    h = x_ref[...] + g1_ref[0] * _dot(m_ref[...], wo_ref[...])
    h_ref[...] = h
    ms = jnp.mean(h * h, axis=-1, keepdims=True)
    n2 = h * lax.rsqrt(ms + EPS) * nw_ref[...] * (1.0 + sc_ref[0]) + sh_ref[0]
    n2p_ref[...] = _pack_bf16_pair(n2[:, :D // 2], n2[:, D // 2:])

    n_hi = n2.astype(BF16)
    n_lo = (n2 - n_hi.astype(F32)).astype(BF16)
    wr = wr_ref[...]
    w_hi = wr.astype(BF16)
    w_lo = (wr - w_hi.astype(F32)).astype(BF16)
    logits = _dot(n_hi, w_hi) + _dot(n_hi, w_lo) + _dot(n_lo, w_hi) + br_ref[...]

    lane = lax.broadcasted_iota(I32, logits.shape, 1)
    cur = jnp.where(lane < N_EXPERTS, logits, -jnp.inf)
    vals, idxs = [], []
    for _ in range(TOP_K):
        mx = jnp.max(cur, axis=-1, keepdims=True)
        ix = jnp.min(jnp.where(cur == mx, lane, LANES), axis=-1, keepdims=True)
        vals.append(mx)
        idxs.append(ix)
        cur = jnp.where(lane == ix, -jnp.inf, cur)
    es = [jnp.exp(v - vals[0]) for v in vals]
    denom = es[0] + es[1] + es[2] + es[3]

    @pl.when(pl.program_id(0) == 0)
    def _():
        cnt_ref[...] = jnp.zeros(cnt_ref.shape, F32)

    hits = [lane == ix for ix in idxs]
    onehot = jnp.where(hits[0] | hits[1] | hits[2] | hits[3], 1.0, 0.0)
    before = _dot(tri_ref[...], onehot.astype(BF16)) + cnt_ref[0:1, :]
    cnt_ref[...] = cnt_ref[...] + jnp.sum(onehot, axis=0, keepdims=True)

    tw = jnp.zeros(logits.shape, F32)
    ti = jnp.zeros(logits.shape, I32)
    for kk in range(TOP_K):
        rank = jnp.sum(jnp.where(hits[kk], before, 0.0), axis=-1, keepdims=True).astype(I32)
        tw = jnp.where(lane == kk, es[kk] / denom, tw)
        ti = jnp.where(lane == kk, idxs[kk], jnp.where(lane == TOP_K + kk, rank, ti))
    tw_ref[...] = tw
    ti_ref[...] = ti


def _post(merged, x2d, g1, norm_w, sc, sh, w_out, w_router, b_router, L):
    T, D = x2d.shape
    tm = 512
    tpb = L // tm
    wr = jnp.pad(w_router, ((0, 0), (0, LANES - w_router.shape[1])))
    br = jnp.pad(b_router.reshape(1, -1), ((0, 0), (0, LANES - b_router.shape[0])))
    row = lambda w: pl.BlockSpec((tm, w), lambda i: (i, 0))
    per_b = pl.BlockSpec((1, 1, D), lambda i: (i // tpb, 0, 0))
    const = lambda shape: pl.BlockSpec(shape, lambda i: (0,) * len(shape))
    tri = np.tril(np.ones((tm, tm), np.float32), -1)
    return pl.pallas_call(
        _post_body,
        out_shape=(jax.ShapeDtypeStruct((T, D), F32), jax.ShapeDtypeStruct((T, D // 2), U32),
                   jax.ShapeDtypeStruct((T, LANES), F32), jax.ShapeDtypeStruct((T, LANES), I32),
                   jax.ShapeDtypeStruct((8, LANES), F32)),
        grid=(T // tm,),
        in_specs=[row(D), row(D), per_b, const((1, D)), per_b, per_b, const((D, D)), const((D, LANES)),
                  const((1, LANES)), const((tm, tm))],
        out_specs=(row(D), row(D // 2), row(LANES), row(LANES), const((8, LANES))),
        compiler_params=_cparams(("arbitrary",)),
        name="post",
    )(merged, x2d, g1, norm_w, sc, sh, w_out, wr, br, jnp.asarray(tri, BF16))


def _routing(top_idx, top_rank, counts, n_sb):
    T = top_idx.shape[0]
    A = T * TOP_K
    flat_e = top_idx.reshape(A)
    rank = top_rank.reshape(A)
    padded = (counts + MOE_SUB - 1) // MOE_SUB * MOE_SUB
    pend = jnp.cumsum(padded)
    pstart = pend - padded
    dest = (pstart[flat_e] + rank).astype(I32)

    nsb = (padded + MOE_TM - 1) // MOE_TM
    sb_end = jnp.cumsum(nsb)
    sb_first = sb_end - nsb
    slot = (sb_first[flat_e] + rank // MOE_TM) * MOE_TM + rank % MOE_TM
    sb_tok = jnp.zeros((n_sb * MOE_TM,), I32).at[slot].set(jnp.arange(A, dtype=I32) // TOP_K)

    total = sb_end[-1]
    i = jnp.arange(n_sb, dtype=I32)
    e_of = jnp.minimum(jnp.searchsorted(sb_end, i, side="right"), N_EXPERTS - 1).astype(I32)
    local = i - sb_first[e_of]
    valid = i < total
    last_e = e_of[jnp.maximum(total - 1, 0)]
    sb_e = jnp.where(valid, e_of, last_e).astype(I32)
    sb_start = jnp.where(valid, pstart[e_of] + local * MOE_TM, 0).astype(I32)
    sb_start = jnp.concatenate([sb_start, pend[-1:].astype(I32)])
    sb_n = jnp.where(valid, jnp.clip(padded[e_of] - local * MOE_TM, 0, MOE_TM), 0).astype(I32)
    return dest, sb_tok.reshape(n_sb, 1, MOE_TM), sb_e, sb_start, sb_n


def _deinterleave(gu):
    rows, two_w = gu.shape
    lane = lax.broadcasted_iota(I32, (rows, LANES), 1)
    idx_e = (2 * lane) % LANES
    idx_o = idx_e + 1
    first = lane < LANES // 2
    gates, ups = [], []
    for p in range(two_w // (2 * LANES)):
        a = gu[:, (2 * p) * LANES:(2 * p + 1) * LANES]
        b = gu[:, (2 * p + 1) * LANES:(2 * p + 2) * LANES]
        gates.append(jnp.where(first, jnp.take_along_axis(a, idx_e, axis=1), jnp.take_along_axis(b, idx_e, axis=1)))
        ups.append(jnp.where(first, jnp.take_along_axis(a, idx_o, axis=1), jnp.take_along_axis(b, idx_o, axis=1)))
    return jnp.concatenate(gates, axis=1), jnp.concatenate(ups, axis=1)


MOE_DMA_UNROLL = 16


def _moe_body(sbe_ref, sbs_ref, sbn_ref, tok_ref, tokn_ref, x_hbm, wgu_ref, bgu_ref, wd_ref, bd_ref, y_hbm,
              xs_buf, xb_buf, acc, ystage, wg_s, wd_s, sem_in, sem_out):
    i = pl.program_id(0)
    hc = pl.program_id(1)
    n_sb = pl.num_programs(0)
    n_hc = pl.num_programs(1)
    SUB = MOE_SUB
    half = xs_buf.shape[1]
    nblk = sbn_ref[i] // SUB
    start_row = sbs_ref[i]

    def row_copy(t_ref, r):
        return pltpu.make_async_copy(x_hbm.at[pl.ds(t_ref[0, 0, r], 1)], xs_buf.at[pl.ds(r, 1)], sem_in)

    def start_in(t_ref, n_rows):
        def body(g, carry):
            for u in range(MOE_DMA_UNROLL):
                row_copy(t_ref, g * MOE_DMA_UNROLL + u).start()
            return carry

        lax.fori_loop(0, n_rows // MOE_DMA_UNROLL, body, 0)

    @pl.when(hc == 0)
    def _():
        @pl.when(i == 0)
        def _():
            start_in(tok_ref, sbn_ref[0])

        def wait_in(g, carry):
            for u in range(MOE_DMA_UNROLL):
                row_copy(tok_ref, g * MOE_DMA_UNROLL + u).wait()
            return carry

        lax.fori_loop(0, sbn_ref[i] // MOE_DMA_UNROLL, wait_in, 0)

        def convert(r, carry):
            rows = pl.ds(pl.multiple_of(r * SUB, SUB), SUB)
            a, b = _unpack_bf16_pair(xs_buf[rows, :])
            xb_buf[rows, 0:half] = a.astype(BF16)
            xb_buf[rows, half:2 * half] = b.astype(BF16)
            return carry

        lax.fori_loop(0, nblk, convert, 0)

        @pl.when(i + 1 < n_sb)
        def _():
            start_in(tokn_ref, sbn_ref[i + 1])

    def accumulate(blocks, first):
        rows = [pl.ds(pl.multiple_of(r * SUB, SUB), SUB) for r in blocks]
        wg = wgu_ref[0].astype(BF16)
        wd = wd_ref[0].astype(BF16)
        gus = [_dot(xb_buf[rw, :], wg) + bgu_ref[0] for rw in rows]
        hs = []
        for gu in gus:
            gate, up = _deinterleave(gu)
            gate = jnp.minimum(gate, SWIGLU_LIMIT)
            up = jnp.clip(up, -SWIGLU_LIMIT, SWIGLU_LIMIT)
            hs.append((gate * jax.nn.sigmoid(gate * SWIGLU_ALPHA) * (up + 1.0)).astype(BF16))
        for rw, hidden in zip(rows, hs):
            v = _dot(hidden, wd)
            if first:
                acc[rw, :] = v + bd_ref[0]
            else:
                acc[rw, :] += v

    def run(first):
        def quad(p, carry):
            accumulate([4 * p, 4 * p + 1, 4 * p + 2, 4 * p + 3], first)
            return carry

        lax.fori_loop(0, nblk // 4, quad, 0)
        done = nblk // 4 * 4

        @pl.when(nblk - done >= 2)
        def _():
            accumulate([done, done + 1], first)

        @pl.when(nblk % 2 == 1)
        def _():
            accumulate([nblk - 1], first)

    @pl.when(nblk > 0)
    def _():
        @pl.when(hc == 0)
        def _():
            run(True)

        @pl.when(hc != 0)
        def _():
            run(False)

    @pl.when(hc == n_hc - 1)
    def _():
        def out_copy(r, slot):
            dst = pl.multiple_of(start_row + r * SUB, SUB)
            return pltpu.make_async_copy(ystage.at[slot], y_hbm.at[pl.ds(dst, SUB)], sem_out.at[slot])

        def body(r, carry):
            slot = r % 2

            @pl.when(r >= 2)
            def _():
                out_copy(r - 2, slot).wait()

            rows = pl.ds(pl.multiple_of(r * SUB, SUB), SUB)
            v = acc[rows, :]
            ystage[slot] = _pack_bf16_pair(v[:, :half], v[:, half:])
            out_copy(r, slot).start()
            return carry

        lax.fori_loop(0, nblk, body, 0)

        def drain(r, carry):
            out_copy(r, r % 2).wait()
            return carry

        lax.fori_loop(jnp.maximum(nblk - 2, 0), nblk, drain, 0)

    @pl.when((i == n_sb - 1) & (hc == n_hc - 1))
    def _():
        ystage[0] = jnp.zeros(ystage.shape[1:], U32)

        def tail_copy(b):
            return pltpu.make_async_copy(ystage.at[0], y_hbm.at[pl.ds(pl.multiple_of(b * SUB, SUB), SUB)], sem_out.at[0])

        def start(b, carry):
            tail_copy(b).start()
            return carry

        def wait(b, carry):
            tail_copy(b).wait()
            return carry

        first, stop = sbs_ref[n_sb] // SUB, y_hbm.shape[0] // SUB
        lax.fori_loop(first, stop, start, 0)
        lax.fori_loop(first, stop, wait, 0)


def _moe(n2p, n_rows, sb_tok, sb_e, sb_start, sb_n, w_gate_up, b_gate_up, w_down, b_down):
    half = n2p.shape[1]
    E, D, H2 = w_gate_up.shape
    n_sb = sb_e.shape[0]
    TH = MOE_TH
    n_hc = (H2 // 2) // TH
    hc_eff = lambda i, hc, sbn: jnp.where(sbn[i] > 0, hc, n_hc - 1)
    grid_spec = pltpu.PrefetchScalarGridSpec(
        num_scalar_prefetch=3,
        grid=(n_sb, n_hc),
        in_specs=[pl.BlockSpec((1, 1, MOE_TM), lambda i, hc, sbe, sbs, sbn: (i, 0, 0), memory_space=pltpu.SMEM),
                  pl.BlockSpec((1, 1, MOE_TM), lambda i, hc, sbe, sbs, sbn: (jnp.minimum(i + 1, n_sb - 1), 0, 0),
                               memory_space=pltpu.SMEM),
                  pl.BlockSpec(memory_space=pl.ANY),
                  pl.BlockSpec((1, D, 2 * TH), lambda i, hc, sbe, sbs, sbn: (sbe[i], 0, hc_eff(i, hc, sbn))),
                  pl.BlockSpec((1, 1, 2 * TH), lambda i, hc, sbe, sbs, sbn: (sbe[i], 0, hc_eff(i, hc, sbn))),
                  pl.BlockSpec((1, TH, D), lambda i, hc, sbe, sbs, sbn: (sbe[i], hc_eff(i, hc, sbn), 0)),
                  pl.BlockSpec((1, 1, D), lambda i, hc, sbe, sbs, sbn: (sbe[i], 0, 0))],
        out_specs=pl.BlockSpec(memory_space=pl.ANY),
        scratch_shapes=[pltpu.VMEM((MOE_TM, half), U32), pltpu.VMEM((MOE_TM, 2 * half), BF16),
                        pltpu.VMEM((MOE_TM, D), F32), pltpu.VMEM((2, MOE_SUB, half), U32),
                        pltpu.VMEM((D, 2 * TH), BF16), pltpu.VMEM((TH, D), BF16),
                        pltpu.SemaphoreType.DMA(()), pltpu.SemaphoreType.DMA((2,))],
    )
    return pl.pallas_call(
        _moe_body,
        out_shape=jax.ShapeDtypeStruct((n_rows, half), U32),
        grid_spec=grid_spec,
        compiler_params=_cparams(("arbitrary", "arbitrary")),
        name="moe",
    )(sb_e, sb_start, sb_n, sb_tok, sb_tok, n2p, w_gate_up, b_gate_up.reshape(E, 1, H2), w_down, b_down.reshape(E, 1, D))


COMBINE_TM = 256


def _combine_body(dest_ref, h_ref, tw_ref, g2_ref, nw_ref, y_hbm, o_ref, ybuf, sem):
    tm = COMBINE_TM
    n = TOP_K * tm

    def copy(j):
        return pltpu.make_async_copy(y_hbm.at[pl.ds(dest_ref[0, 0, j], 1)], ybuf.at[pl.ds(j, 1)], sem)

    def start(j, carry):
        copy(j).start()
        return carry

    def wait(j, carry):
        copy(j).wait()
        return carry

    lax.fori_loop(0, n, start, 0, unroll=8)
    lax.fori_loop(0, n, wait, 0, unroll=8)

    tw = tw_ref[...]
    lo = hi = None
    for kk in range(TOP_K):
        a, b = _unpack_bf16_pair(ybuf[kk * tm:(kk + 1) * tm, :])
        wk = tw[:, kk:kk + 1]
        lo = wk * a if lo is None else lo + wk * a
        hi = wk * b if hi is None else hi + wk * b
    ffn = jnp.concatenate([lo, hi], axis=1)
    h = h_ref[...] + g2_ref[0] * ffn
    ms = jnp.mean(h * h, axis=-1, keepdims=True)
    o_ref[...] = h * lax.rsqrt(ms + EPS) * nw_ref[...]


def _combine(h1, y_sorted, dest, top_w, g2, final_norm_w, L):
    T, D = h1.shape
    tm = COMBINE_TM
    tiles = T // tm
    tpb = L // tm
    dest_tiles = dest.reshape(tiles, tm, TOP_K).transpose(0, 2, 1).reshape(tiles, 1, TOP_K * tm)
    return pl.pallas_call(
        _combine_body,
        out_shape=jax.ShapeDtypeStruct((T, D), F32),
        grid=(tiles,),
        in_specs=[pl.BlockSpec((1, 1, TOP_K * tm), lambda i: (i, 0, 0), memory_space=pltpu.SMEM),
                  pl.BlockSpec((tm, D), lambda i: (i, 0)),
                  pl.BlockSpec((tm, LANES), lambda i: (i, 0)),
                  pl.BlockSpec((1, 1, D), lambda i: (i // tpb, 0, 0)),
                  pl.BlockSpec((1, D), lambda i: (0, 0)),
                  pl.BlockSpec(memory_space=pl.ANY)],
        out_specs=pl.BlockSpec((tm, D), lambda i: (i, 0)),
        scratch_shapes=[pltpu.VMEM((TOP_K * tm, D // 2), U32), pltpu.SemaphoreType.DMA(())],
        compiler_params=_cparams(("arbitrary",)),
        name="combine",
    )(dest_tiles, h1, top_w, g2, final_norm_w.reshape(1, D), y_sorted)


def kernel(x, c, w_ada, b_ada, norm1_w, w_in, ssd_conv_w, ssd_conv_b, ssd_dt_bias, ssd_a_log, ssd_d_skip, ssd_norm_w, gla_w_gk2, gla_b_gk2, gla_norm_w, w_ssd_out, w_gla_out, w_out, norm2_w, w_router, b_router, w_gate_up, b_gate_up, w_down, b_down, final_norm_w):
    B, L, D = x.shape
    T = B * L
    assert D == D_MODEL and w_ada.shape[0] == 1
    x2d = x.reshape(T, D)

    mod = _ada(c, w_ada[0], b_ada[0])
    sh1, sc1, g1, sh2, sc2, g2 = [mod[:, i * D:(i + 1) * D].reshape(B, 1, D) for i in range(6)]

    w_t = w_in[0].T
    pad = lambda a: jnp.pad(a, ((0, LANES - a.shape[0]), (0, 0)))
    w_s = jnp.concatenate([pad(w_t[SRC_DT[0]:SRC_DT[1]]), pad(w_t[SRC_GK[0]:SRC_GK[1]])], axis=0)
    proj_a, n1 = _inproj_a(x2d, norm1_w[0].reshape(1, D), sc1, sh1, w_t, L)
    proj_b = _matmul(n1, w_t, SRC_B, PROJ_TN, "inproj_b")
    proj_c = _matmul(n1, w_t, SRC_C, PROJ_TN, "inproj_c")
    proj_s = _matmul(n1, w_s, (0, 2 * LANES), 2 * LANES, "inproj_s")
    pa3, pb3, ps3 = [p.reshape(B, L, p.shape[1]) for p in (proj_a, proj_b, proj_s)]
    y_ssd = _ssd(pa3, ps3, ssd_conv_w[0], ssd_conv_b[0], ssd_dt_bias[0], ssd_a_log[0], ssd_d_skip[0], ssd_norm_w[0])
    y_gla = _gla(pb3, ps3, gla_w_gk2[0], gla_b_gk2[0], gla_norm_w[0])
    merged = _merge(y_ssd.reshape(T, D), y_gla.reshape(T, D), proj_c, w_ssd_out[0].astype(BF16), w_gla_out[0].astype(BF16))
    h1, n2p, top_w, top_i, cnt = _post(merged, x2d, g1, norm2_w[0].reshape(1, D), sc2, sh2, w_out[0].astype(BF16),
                                       w_router[0], b_router[0], L)

    A = T * TOP_K
    n_rows = A + N_EXPERTS * MOE_SUB
    n_sb = N_EXPERTS + n_rows // MOE_TM
    dest, sb_tok, sb_e, sb_start, sb_n = _routing(top_i[:, :TOP_K], top_i[:, TOP_K:2 * TOP_K],
                                                  cnt[0, :N_EXPERTS].astype(I32), n_sb)
    y_sorted = _moe(n2p, n_rows, sb_tok, sb_e, sb_start, sb_n, w_gate_up[0], b_gate_up[0], w_down[0], b_down[0])
    out = _combine(h1, y_sorted, dest, top_w, g2, final_norm_w, L)
    return out.reshape(B, L, D)
```

```python
import functools

import jax
import jax.numpy as jnp
import numpy as np
from jax import lax
from jax.experimental import pallas as pl
from jax.experimental.pallas import tpu as pltpu

F32 = jnp.float32
BF16 = jnp.bfloat16
U32 = jnp.uint32
I32 = jnp.int32

EPS = 1e-6
LANES = 128
VMEM_LIMIT = 56 * 1024 * 1024

D_MODEL = 2048
SSD_HEADS, SSD_P, SSD_G, SSD_N, SSD_CONV, SSD_Q = 32, 64, 4, 128, 4, 256
SSD_INNER = SSD_HEADS * SSD_P
SSD_TAIL = 16
GLA_H, GLA_DK, GLA_DV, GLA_RANK, GLA_C = 4, 256, 512, 16, 64
GLA_GATE_NORM = 16.0
N_EXPERTS, TOP_K = 32, 4
SWIGLU_LIMIT, SWIGLU_ALPHA = 7.0, 1.702

SRC_A = (0, 5120)
SRC_DT = (5120, 5152)
SRC_B = (5152, 11296)
SRC_GK = (11296, 11312)
SRC_C = (11312, 15408)
A_Z, A_XS, A_BC = 0, 2048, 4096
B_Q, B_K, B_V, B_GO = 0, 1024, 2048, 4096
C_GS, C_GG = 0, 2048
PROJ_TN = 1024

MOE_SUB = 256
MOE_TM = 2048
MOE_TH = 256


def _cparams(sem, vmem=VMEM_LIMIT):
    return pltpu.CompilerParams(dimension_semantics=sem, vmem_limit_bytes=vmem)


def _split3(x):
    hi = x.astype(BF16)
    r1 = x - hi.astype(F32)
    mid = r1.astype(BF16)
    lo = (r1 - mid.astype(F32)).astype(BF16)
    return hi, mid, lo


def _dot(a, b):
    return jnp.dot(a, b, preferred_element_type=F32)


def _dot_nt(a, b):
    return lax.dot_general(a, b, (((1,), (1,)), ((), ())), preferred_element_type=F32)


def _dot_tn(a, b):
    return lax.dot_general(a, b, (((0,), (0,)), ((), ())), preferred_element_type=F32)


def _dot_exact_lhs(a01, x):
    hi, mid, lo = _split3(x)
    return _dot(a01, hi) + _dot(a01, mid) + _dot(a01, lo)


def _dot_hilo_rhs(x, b01):
    hi = x.astype(BF16)
    lo = (x - hi.astype(F32)).astype(BF16)
    return _dot(hi, b01) + _dot(lo, b01)


def _silu(x):
    return x * jax.nn.sigmoid(x)


def _softplus(x):
    return jnp.maximum(x, 0.0) + jnp.log1p(jnp.exp(-jnp.abs(x)))


def _ada_body(ct_ref, w_ref, b_ref, o_ref):
    ct = ct_ref[...]
    cs = _silu(ct)
    w = w_ref[...]
    for m in range(ct.shape[1]):
        o_ref[m:m + 1, :] = jnp.sum(w * cs[:, m:m + 1], axis=0, keepdims=True) + b_ref[...]


def _ada(c, w_ada, b_ada):
    B, D = c.shape
    N = w_ada.shape[1]
    tn = 1024
    return pl.pallas_call(
        _ada_body,
        out_shape=jax.ShapeDtypeStruct((B, N), F32),
        grid=(N // tn,),
        in_specs=[pl.BlockSpec((D, B), lambda j: (0, 0)),
                  pl.BlockSpec((D, tn), lambda j: (0, j)),
                  pl.BlockSpec((1, tn), lambda j: (0, j))],
        out_specs=pl.BlockSpec((B, tn), lambda j: (0, j)),
        compiler_params=_cparams(("arbitrary",)),
        name="ada",
    )(c.T, w_ada, b_ada.reshape(1, N))


def _inproj_a_body(x_ref, nw_ref, sc_ref, sh_ref, w_ref, o_ref, n_ref):
    @pl.when(pl.program_id(1) == 0)
    def _():
        x = x_ref[...]
        ms = jnp.mean(x * x, axis=-1, keepdims=True)
        y = x * lax.rsqrt(ms + EPS) * nw_ref[...]
        n_ref[...] = (y * (1.0 + sc_ref[0]) + sh_ref[0]).astype(BF16)

    o_ref[...] = _dot_nt(n_ref[...], w_ref[...].astype(BF16)).astype(BF16)


def _inproj_a(x2d, norm_w, sc, sh, w_t, L):
    T, D = x2d.shape
    tm, tn = 1024, PROJ_TN // 2
    n_cols = SRC_A[1] - SRC_A[0]
    tiles_per_batch = L // tm
    return pl.pallas_call(
        _inproj_a_body,
        out_shape=(jax.ShapeDtypeStruct((T, n_cols), BF16), jax.ShapeDtypeStruct((T, D), BF16)),
        grid=(T // tm, n_cols // tn),
        in_specs=[pl.BlockSpec((tm, D), lambda i, j: (i, 0)),
                  pl.BlockSpec((1, D), lambda i, j: (0, 0)),
                  pl.BlockSpec((1, 1, D), lambda i, j: (i // tiles_per_batch, 0, 0)),
                  pl.BlockSpec((1, 1, D), lambda i, j: (i // tiles_per_batch, 0, 0)),
                  pl.BlockSpec((tn, D), lambda i, j: (SRC_A[0] // tn + j, 0))],
        out_specs=(pl.BlockSpec((tm, tn), lambda i, j: (i, j)), pl.BlockSpec((tm, D), lambda i, j: (i, 0))),
        compiler_params=_cparams(("arbitrary", "arbitrary")),
        name="inproj_a",
    )(x2d, norm_w, sc, sh, w_t)


def _matmul_body(a_ref, w_ref, o_ref):
    o_ref[...] = _dot_nt(a_ref[...], w_ref[...].astype(BF16)).astype(BF16)


def _matmul(a, w_t, rows, tn, name):
    T, D = a.shape
    N = rows[1] - rows[0]
    tm = 1024
    return pl.pallas_call(
        _matmul_body,
        out_shape=jax.ShapeDtypeStruct((T, N), BF16),
        grid=(T // tm, N // tn),
        in_specs=[pl.BlockSpec((tm, D), lambda i, j: (i, 0)),
                  pl.BlockSpec((pl.Element(tn), pl.Element(D)),
                               lambda i, j: ((rows[0] // 8 + j * (tn // 8)) * 8, 0))],
        out_specs=pl.BlockSpec((tm, tn), lambda i, j: (i, j)),
        compiler_params=_cparams(("arbitrary", "arbitrary")),
        name=name,
    )(a, w_t)


def _ssd_body(z_ref, xs_ref, bc_ref, dt_ref, cwx_ref, cwbc_ref, cbx_ref, cbbc_ref, dtb_ref, alog_ref,
              dskip_ref, nw_ref, exp_ref, sh_ref, o_ref, ubx, ubbc, state):
    Q = SSD_Q
    c = pl.program_id(1)

    @pl.when(c == 0)
    def _():
        ubx[...] = jnp.zeros(ubx.shape, F32)
        ubbc[...] = jnp.zeros(ubbc.shape, F32)
        state[...] = jnp.zeros(state.shape, F32)

    def conv_silu(u_ref, tail, cw_ref, cb_ref):
        u = u_ref[0]
        head = u_ref[0, 0:SSD_TAIL, :].astype(F32)[0:8, :]
        prev = tail[...]
        r8 = lax.broadcasted_iota(I32, head.shape, 0)
        w_last = cw_ref[SSD_CONV - 1:SSD_CONV, :]
        acc = cb_ref[...] + w_last * u.astype(F32)
        acc8 = cb_ref[...] + w_last * head
        for s in range(1, SSD_CONV):
            w_s = cw_ref[SSD_CONV - 1 - s:SSD_CONV - s, :]
            acc = acc + w_s * _dot(sh_ref[s - 1], u)
            acc8 = acc8 + w_s * jnp.where(r8 < s, pltpu.roll(prev, s, 0), pltpu.roll(head, s, 0))
        tail[...] = u_ref[0, Q - SSD_TAIL:Q, :].astype(F32)[SSD_TAIL - 8:, :]
        return _silu(jnp.concatenate([acc8, acc[8:, :]], axis=0))

    xs = conv_silu(xs_ref, ubx, cwx_ref, cbx_ref)
    bcm = conv_silu(bc_ref, ubbc, cwbc_ref, cbbc_ref)
    GN = SSD_G * SSD_N
    bm_f, cm = bcm[:, :GN], bcm[:, GN:].astype(BF16)

    dt = _softplus(dt_ref[0].astype(F32) + dtb_ref[...])
    a_neg = -jnp.exp(alog_ref[...])
    dA = dt * a_neg

    row = lax.broadcasted_iota(I32, (Q, Q), 0)
    col = lax.broadcasted_iota(I32, (Q, Q), 1)
    causal = row >= col
    tri = jnp.where(causal, 1.0, 0.0).astype(BF16)
    acs = _dot_exact_lhs(tri, dA)
    acs_t = acs.T
    acs_last = acs[Q - 1:Q, :]
    exp_a = jnp.exp(acs)
    decay_st = jnp.exp(acs_last - acs)

    expand = exp_ref[...]
    dt_e = _dot_hilo_rhs(dt, expand)
    dtd_e = _dot_hilo_rhs(dt * decay_st, expand)
    expa_e = _dot_hilo_rhs(exp_a, expand)

    x_dt = (xs * dt_e).astype(BF16)
    x_dd = (xs * dtd_e).astype(BF16)

    lane = lax.broadcasted_iota(I32, (Q, LANES), 1)
    lo_half = lane < SSD_P
    HG = SSD_HEADS // SSD_G
    GW = HG * SSD_P
    y_groups = []
    for g in range(SSD_G):
        cg = cm[:, g * SSD_N:(g + 1) * SSD_N]
        bg_f = bm_f[:, g * SSD_N:(g + 1) * SSD_N]
        bg = bg_f.astype(BF16)
        scores = _dot_nt(cg, bg)
        pieces = []
        for p in range(HG // 2):
            h0 = g * HG + 2 * p
            xp = x_dt[:, h0 * SSD_P:h0 * SSD_P + LANES]
            acc = None
            for s in range(2):
                h = h0 + s
                seg = acs[:, h:h + 1] - acs_t[h:h + 1, :]
                lmat = jnp.exp(jnp.where(causal, seg, -jnp.inf))
                m = (scores * lmat).astype(BF16)
                xh = jnp.where(lo_half if s == 0 else jnp.logical_not(lo_half), xp, jnp.zeros_like(xp))
                part = _dot(m, xh)
                acc = part if acc is None else acc + part
            pieces.append(acc)
        y_diag = jnp.concatenate(pieces, axis=1)
        st_old = state[g]
        y_off = _dot(cg, st_old.astype(BF16)) * expa_e[:, g * GW:(g + 1) * GW]
        st_new = _dot(bg_f.T.astype(BF16), x_dd[:, g * GW:(g + 1) * GW])
        state[g] = st_old * expa_e[Q - 1:Q, g * GW:(g + 1) * GW] + st_new
        y_groups.append(y_diag + y_off)

    y = jnp.concatenate(y_groups, axis=1) + xs * dskip_ref[...]
    y = y * _silu(z_ref[0].astype(F32))
    outs = []
    for g in range(SSD_G):
        yg = y[:, g * GW:(g + 1) * GW]
        ms = jnp.mean(yg * yg, axis=-1, keepdims=True)
        outs.append(yg * lax.rsqrt(ms + EPS) * nw_ref[:, g * GW:(g + 1) * GW])
    o_ref[0] = jnp.concatenate(outs, axis=1).astype(BF16)


def _ssd(pa3, ps3, conv_w, conv_b, dt_bias, a_log, d_skip, norm_w):
    B, L, _ = pa3.shape
    Q, DI, BCW = SSD_Q, SSD_INNER, 2 * SSD_G * SSD_N
    padl = lambda a: jnp.pad(a.reshape(1, -1), ((0, 0), (0, LANES - a.shape[-1])))
    expand = (np.arange(LANES)[:, None] == (np.arange(DI)[None, :] // SSD_P)).astype(np.float32)
    t = np.arange(Q)[:, None]
    shifts = np.stack([(t - s == np.arange(Q)[None, :]) for s in range(1, SSD_CONV)]).astype(np.float32)
    const = lambda shape: pl.BlockSpec(shape, lambda b, c: (0,) * len(shape))
    return pl.pallas_call(
        _ssd_body,
        out_shape=jax.ShapeDtypeStruct((B, L, DI), BF16),
        grid=(B, L // Q),
        in_specs=[pl.BlockSpec((1, Q, DI), lambda b, c: (b, c, A_Z // DI)),
                  pl.BlockSpec((1, Q, DI), lambda b, c: (b, c, A_XS // DI)),
                  pl.BlockSpec((1, Q, BCW), lambda b, c: (b, c, A_BC // BCW)),
                  pl.BlockSpec((1, Q, LANES), lambda b, c: (b, c, 0)),
                  const((SSD_CONV, DI)), const((SSD_CONV, BCW)), const((1, DI)), const((1, BCW)),
                  const((1, LANES)), const((1, LANES)), const((1, DI)), const((1, DI)),
                  const((LANES, DI)), const((SSD_CONV - 1, Q, Q))],
        out_specs=pl.BlockSpec((1, Q, DI), lambda b, c: (b, c, 0)),
        scratch_shapes=[pltpu.VMEM((8, DI), F32), pltpu.VMEM((8, BCW), F32),
                        pltpu.VMEM((SSD_G, SSD_N, DI // SSD_G), F32)],
        compiler_params=_cparams(("arbitrary", "arbitrary")),
        name="ssd",
    )(pa3, pa3, pa3, ps3, conv_w[:, :DI], conv_w[:, DI:], conv_b[:DI].reshape(1, DI),
      conv_b[DI:].reshape(1, BCW), padl(dt_bias), padl(a_log), jnp.repeat(d_skip, SSD_P).reshape(1, DI),
      norm_w.reshape(1, DI), jnp.asarray(expand, BF16), jnp.asarray(shifts, BF16))


GLA_RB = 256


def _gla_body(q_ref, k_ref, v_ref, g_ref, gk_ref, w_ref, b_ref, nw_ref, o_ref, st):
    RB, C = GLA_RB, GLA_C

    @pl.when(pl.program_id(1) == 0)
    def _():
        st[...] = jnp.zeros(st.shape, F32)

    w_hi, w_mid, w_lo = _split3(w_ref[...])
    gk = gk_ref[0]
    pre = _dot(gk, w_hi) + _dot(gk, w_mid) + _dot(gk, w_lo) + b_ref[...]
    log_a = (jnp.minimum(pre, 0.0) - jnp.log1p(jnp.exp(-jnp.abs(pre)))) * (1.0 / GLA_GATE_NORM)

    row = lax.broadcasted_iota(I32, (RB, RB), 0)
    col = lax.broadcasted_iota(I32, (RB, RB), 1)
    blocktri = jnp.where((row // C == col // C) & (row >= col), 1.0, 0.0).astype(BF16)
    bcs_all = _dot_exact_lhs(blocktri, log_a)

    r64 = lax.broadcasted_iota(I32, (C, C), 0)
    c64 = lax.broadcasted_iota(I32, (C, C), 1)
    causal = r64 >= c64
    scale = GLA_DK ** -0.5

    for s in range(RB // C):
        rs = slice(s * C, (s + 1) * C)
        bcs = bcs_all[rs, :]
        last = bcs[C - 1:C, :]
        qf = q_ref[0, rs, :].astype(F32)
        kf = k_ref[0, rs, :].astype(F32)
        q_dec = (qf * scale * jnp.exp(bcs)).astype(BF16)
        k_inv = (kf * jnp.exp(-bcs)).astype(BF16)
        k_st = (kf * jnp.exp(last - bcs)).astype(BF16)
        cdec = jnp.exp(last)
        for h in range(GLA_H):
            ks = slice(h * GLA_DK, (h + 1) * GLA_DK)
            vs = slice(h * GLA_DV, (h + 1) * GLA_DV)
            vh = v_ref[0, rs, vs]
            attn = jnp.where(causal, _dot_nt(q_dec[:, ks], k_inv[:, ks]), 0.0)
            st_h = st[h]
            o = _dot(attn.astype(BF16), vh) + _dot_nt(q_dec[:, ks], st_h.astype(BF16))
            st[h] = st_h * cdec[:, ks] + _dot_tn(vh, k_st[:, ks])
            ms = jnp.mean(o * o, axis=-1, keepdims=True)
            o = o * lax.rsqrt(ms + EPS) * nw_ref[...]
            o_ref[0, rs, vs] = (o * _silu(g_ref[0, rs, vs].astype(F32))).astype(BF16)


def _gla(pb3, ps3, w_gk2, b_gk2, norm_w):
    B, L, _ = pb3.shape
    RB, KD, VD = GLA_RB, GLA_H * GLA_DK, GLA_H * GLA_DV
    w_pad = jnp.pad(w_gk2, ((0, LANES - w_gk2.shape[0]), (0, 0)))
    const = lambda shape: pl.BlockSpec(shape, lambda b, c: (0,) * len(shape))
    return pl.pallas_call(
        _gla_body,
        out_shape=jax.ShapeDtypeStruct((B, L, VD), BF16),
        grid=(B, L // RB),
        in_specs=[pl.BlockSpec((1, RB, KD), lambda b, c: (b, c, B_Q // KD)),
                  pl.BlockSpec((1, RB, KD), lambda b, c: (b, c, B_K // KD)),
                  pl.BlockSpec((1, RB, VD), lambda b, c: (b, c, B_V // VD)),
                  pl.BlockSpec((1, RB, VD), lambda b, c: (b, c, B_GO // VD)),
                  pl.BlockSpec((1, RB, LANES), lambda b, c: (b, c, 1)),
                  const((LANES, KD)), const((1, KD)), const((1, GLA_DV))],
        out_specs=pl.BlockSpec((1, RB, VD), lambda b, c: (b, c, 0)),
        scratch_shapes=[pltpu.VMEM((GLA_H, GLA_DV, GLA_DK), F32)],
        compiler_params=_cparams(("arbitrary", "arbitrary")),
        name="gla",
    )(pb3, pb3, pb3, pb3, ps3, w_pad, b_gk2.reshape(1, KD), norm_w.reshape(1, GLA_DV))


def _merge_body(ys_ref, yg_ref, gs_ref, gg_ref, ws_ref, wg_ref, o_ref):
    a = _dot(ys_ref[...], ws_ref[...])
    b = _dot(yg_ref[...], wg_ref[...])
    m = jax.nn.sigmoid(gs_ref[...].astype(F32)) * a + jax.nn.sigmoid(gg_ref[...].astype(F32)) * b
    o_ref[...] = m.astype(BF16)


def _merge(y_ssd, y_gla, proj_c, w_ssd_out, w_gla_out):
    T, D = y_ssd.shape
    tm, tn = 512, 1024
    return pl.pallas_call(
        _merge_body,
        out_shape=jax.ShapeDtypeStruct((T, D), BF16),
        grid=(D // tn, T // tm),
        in_specs=[pl.BlockSpec((tm, D), lambda j, i: (i, 0)),
                  pl.BlockSpec((tm, D), lambda j, i: (i, 0)),
                  pl.BlockSpec((tm, tn), lambda j, i: (i, C_GS // tn + j)),
                  pl.BlockSpec((tm, tn), lambda j, i: (i, C_GG // tn + j)),
                  pl.BlockSpec((D, tn), lambda j, i: (0, j)),
                  pl.BlockSpec((D, tn), lambda j, i: (0, j))],
        out_specs=pl.BlockSpec((tm, tn), lambda j, i: (i, j)),
        compiler_params=_cparams(("arbitrary", "arbitrary")),
        name="merge",
    )(y_ssd, y_gla, proj_c, proj_c, w_ssd_out, w_gla_out)


def _pack_bf16_pair(a, b):
    ua = lax.bitcast_convert_type(a.astype(BF16).astype(F32), U32)
    ub = lax.bitcast_convert_type(b.astype(BF16).astype(F32), U32)
    return (ua & jnp.uint32(0xFFFF0000)) | (ub >> 16)


def _unpack_bf16_pair(w):
    a = lax.bitcast_convert_type(w & jnp.uint32(0xFFFF0000), F32)
    b = lax.bitcast_convert_type(w << 16, F32)
    return a, b


def _post_body(m_ref, x_ref, g1_ref, nw_ref, sc_ref, sh_ref, wo_ref, wr_ref, br_ref, tri_ref,
               h_ref, n2p_ref, tw_ref, ti_ref, cnt_ref):
    D = x_ref.shape[1]
    h = x_ref[...] + g1_ref[0] * _dot(m_ref[...], wo_ref[...])
    h_ref[...] = h
    ms = jnp.mean(h * h, axis=-1, keepdims=True)
    n2 = h * lax.rsqrt(ms + EPS) * nw_ref[...] * (1.0 + sc_ref[0]) + sh_ref[0]
    n2p_ref[...] = _pack_bf16_pair(n2[:, :D // 2], n2[:, D // 2:])

    n_hi = n2.astype(BF16)
    n_lo = (n2 - n_hi.astype(F32)).astype(BF16)
    wr = wr_ref[...]
    w_hi = wr.astype(BF16)
    w_lo = (wr - w_hi.astype(F32)).astype(BF16)
    logits = _dot(n_hi, w_hi) + _dot(n_hi, w_lo) + _dot(n_lo, w_hi) + br_ref[...]

    lane = lax.broadcasted_iota(I32, logits.shape, 1)
    cur = jnp.where(lane < N_EXPERTS, logits, -jnp.inf)
    vals, idxs = [], []
    for _ in range(TOP_K):
        mx = jnp.max(cur, axis=-1, keepdims=True)
        ix = jnp.min(jnp.where(cur == mx, lane, LANES), axis=-1, keepdims=True)
        vals.append(mx)
        idxs.append(ix)
        cur = jnp.where(lane == ix, -jnp.inf, cur)
    es = [jnp.exp(v - vals[0]) for v in vals]
    denom = es[0] + es[1] + es[2] + es[3]

    @pl.when(pl.program_id(0) == 0)
    def _():
        cnt_ref[...] = jnp.zeros(cnt_ref.shape, F32)

    hits = [lane == ix for ix in idxs]
    onehot = jnp.where(hits[0] | hits[1] | hits[2] | hits[3], 1.0, 0.0)
    before = _dot(tri_ref[...], onehot.astype(BF16)) + cnt_ref[0:1, :]
    cnt_ref[...] = cnt_ref[...] + jnp.sum(onehot, axis=0, keepdims=True)

    tw = jnp.zeros(logits.shape, F32)
    ti = jnp.zeros(logits.shape, I32)
    for kk in range(TOP_K):
        rank = jnp.sum(jnp.where(hits[kk], before, 0.0), axis=-1, keepdims=True).astype(I32)
        tw = jnp.where(lane == kk, es[kk] / denom, tw)
        ti = jnp.where(lane == kk, idxs[kk], jnp.where(lane == TOP_K + kk, rank, ti))
    tw_ref[...] = tw
    ti_ref[...] = ti


def _post(merged, x2d, g1, norm_w, sc, sh, w_out, w_router, b_router, L):
    T, D = x2d.shape
    tm = 512
    tpb = L // tm
    wr = jnp.pad(w_router, ((0, 0), (0, LANES - w_router.shape[1])))
    br = jnp.pad(b_router.reshape(1, -1), ((0, 0), (0, LANES - b_router.shape[0])))
    row = lambda w: pl.BlockSpec((tm, w), lambda i: (i, 0))
    per_b = pl.BlockSpec((1, 1, D), lambda i: (i // tpb, 0, 0))
    const = lambda shape: pl.BlockSpec(shape, lambda i: (0,) * len(shape))
    tri = np.tril(np.ones((tm, tm), np.float32), -1)
    return pl.pallas_call(
        _post_body,
        out_shape=(jax.ShapeDtypeStruct((T, D), F32), jax.ShapeDtypeStruct((T, D // 2), U32),
                   jax.ShapeDtypeStruct((T, LANES), F32), jax.ShapeDtypeStruct((T, LANES), I32),
                   jax.ShapeDtypeStruct((8, LANES), F32)),
        grid=(T // tm,),
        in_specs=[row(D), row(D), per_b, const((1, D)), per_b, per_b, const((D, D)), const((D, LANES)),
                  const((1, LANES)), const((tm, tm))],
        out_specs=(row(D), row(D // 2), row(LANES), row(LANES), const((8, LANES))),
        compiler_params=_cparams(("arbitrary",)),
        name="post",
    )(merged, x2d, g1, norm_w, sc, sh, w_out, wr, br, jnp.asarray(tri, BF16))


def _routing(top_idx, top_rank, counts, n_sb):
    T = top_idx.shape[0]
    A = T * TOP_K
    flat_e = top_idx.reshape(A)
    rank = top_rank.reshape(A)
    padded = (counts + MOE_SUB - 1) // MOE_SUB * MOE_SUB
    pend = jnp.cumsum(padded)
    pstart = pend - padded
    dest = (pstart[flat_e] + rank).astype(I32)

    nsb = (padded + MOE_TM - 1) // MOE_TM
    sb_end = jnp.cumsum(nsb)
    sb_first = sb_end - nsb
    slot = (sb_first[flat_e] + rank // MOE_TM) * MOE_TM + rank % MOE_TM
    sb_tok = jnp.zeros((n_sb * MOE_TM,), I32).at[slot].set(jnp.arange(A, dtype=I32) // TOP_K)

    total = sb_end[-1]
    i = jnp.arange(n_sb, dtype=I32)
    e_of = jnp.minimum(jnp.searchsorted(sb_end, i, side="right"), N_EXPERTS - 1).astype(I32)
    local = i - sb_first[e_of]
    valid = i < total
    last_e = e_of[jnp.maximum(total - 1, 0)]
    sb_e = jnp.where(valid, e_of, last_e).astype(I32)
    sb_start = jnp.where(valid, pstart[e_of] + local * MOE_TM, 0).astype(I32)
    sb_start = jnp.concatenate([sb_start, pend[-1:].astype(I32)])
    sb_n = jnp.where(valid, jnp.clip(padded[e_of] - local * MOE_TM, 0, MOE_TM), 0).astype(I32)
    return dest, sb_tok.reshape(n_sb, 1, MOE_TM), sb_e, sb_start, sb_n


def _deinterleave(gu):
    rows, two_w = gu.shape
    lane = lax.broadcasted_iota(I32, (rows, LANES), 1)
    idx_e = (2 * lane) % LANES
    idx_o = idx_e + 1
    first = lane < LANES // 2
    gates, ups = [], []
    for p in range(two_w // (2 * LANES)):
        a = gu[:, (2 * p) * LANES:(2 * p + 1) * LANES]
        b = gu[:, (2 * p + 1) * LANES:(2 * p + 2) * LANES]
        gates.append(jnp.where(first, jnp.take_along_axis(a, idx_e, axis=1), jnp.take_along_axis(b, idx_e, axis=1)))
        ups.append(jnp.where(first, jnp.take_along_axis(a, idx_o, axis=1), jnp.take_along_axis(b, idx_o, axis=1)))
    return jnp.concatenate(gates, axis=1), jnp.concatenate(ups, axis=1)


MOE_DMA_UNROLL = 16


def _moe_body(sbe_ref, sbs_ref, sbn_ref, tok_ref, tokn_ref, x_hbm, wgu_ref, bgu_ref, wd_ref, bd_ref, y_hbm,
              xs_buf, xb_buf, acc, ystage, pend, sem_in, sem_out):
    i = pl.program_id(0)
    hc = pl.program_id(1)
    n_sb = pl.num_programs(0)
    n_hc = pl.num_programs(1)
    SUB = MOE_SUB
    half = xs_buf.shape[1]
    nblk = sbn_ref[i] // SUB
    start_row = sbs_ref[i]

    def row_copy(t_ref, r):
        return pltpu.make_async_copy(x_hbm.at[pl.ds(t_ref[0, 0, r], 1)], xs_buf.at[pl.ds(r, 1)], sem_in)

    def start_in(t_ref, n_rows):
        def body(g, carry):
            for u in range(MOE_DMA_UNROLL):
                row_copy(t_ref, g * MOE_DMA_UNROLL + u).start()
            return carry

        lax.fori_loop(0, n_rows // MOE_DMA_UNROLL, body, 0)

    @pl.when(hc == 0)
    def _():
        @pl.when(i == 0)
        def _():
            pend[0] = 0
            pend[1] = 0
            start_in(tok_ref, sbn_ref[0])

        def wait_in(g, carry):
            for u in range(MOE_DMA_UNROLL):
                row_copy(tok_ref, g * MOE_DMA_UNROLL + u).wait()
            return carry

        lax.fori_loop(0, sbn_ref[i] // MOE_DMA_UNROLL, wait_in, 0)

        def convert(r, carry):
            rows = pl.ds(pl.multiple_of(r * SUB, SUB), SUB)
            a, b = _unpack_bf16_pair(xs_buf[rows, :])
            xb_buf[rows, 0:half] = a.astype(BF16)
            xb_buf[rows, half:2 * half] = b.astype(BF16)
            return carry

        lax.fori_loop(0, nblk, convert, 0)

        @pl.when(i + 1 < n_sb)
        def _():
            start_in(tokn_ref, sbn_ref[i + 1])

    def accumulate(blocks, first):
        rows = [pl.ds(pl.multiple_of(r * SUB, SUB), SUB) for r in blocks]
        wg = wgu_ref[0].astype(BF16)
        wd = wd_ref[0].astype(BF16)
        gus = [_dot(xb_buf[rw, :], wg) + bgu_ref[0] for rw in rows]
        hs = []
        for gu in gus:
            gate, up = _deinterleave(gu)
            gate = jnp.minimum(gate, SWIGLU_LIMIT)
            up = jnp.clip(up, -SWIGLU_LIMIT, SWIGLU_LIMIT)
            hs.append((gate * jax.nn.sigmoid(gate * SWIGLU_ALPHA) * (up + 1.0)).astype(BF16))
        for rw, hidden in zip(rows, hs):
            v = _dot(hidden, wd)
            if first:
                acc[rw, :] = v + bd_ref[0]
            else:
                acc[rw, :] += v

    def run(first):
        def quad(p, carry):
            accumulate([4 * p, 4 * p + 1, 4 * p + 2, 4 * p + 3], first)
            return carry

        lax.fori_loop(0, nblk // 4, quad, 0)
        done = nblk // 4 * 4

        @pl.when(nblk - done >= 2)
        def _():
            accumulate([done, done + 1], first)

        @pl.when(nblk % 2 == 1)
        def _():
            accumulate([nblk - 1], first)

    @pl.when(nblk > 0)
    def _():
        @pl.when(hc == 0)
        def _():
            run(True)

        @pl.when(hc != 0)
        def _():
            run(False)

    @pl.when(hc == n_hc - 1)
    def _():
        def out_copy(r, slot):
            dst = pl.multiple_of(start_row + r * SUB, SUB)
            return pltpu.make_async_copy(ystage.at[slot], y_hbm.at[pl.ds(dst, SUB)], sem_out.at[slot])

        def body(r, carry):
            slot = r % 2

            @pl.when(pend[slot] == 1)
            def _():
                out_copy(r, slot).wait()

            rows = pl.ds(pl.multiple_of(r * SUB, SUB), SUB)
            v = acc[rows, :]
            ystage[slot] = _pack_bf16_pair(v[:, :half], v[:, half:])
            out_copy(r, slot).start()
            pend[slot] = 1
            return carry

        lax.fori_loop(0, nblk, body, 0)

    @pl.when((i == n_sb - 1) & (hc == n_hc - 1))
    def _():
        for slot in range(2):
            @pl.when(pend[slot] == 1)
            def _():
                pltpu.make_async_copy(ystage.at[slot], y_hbm.at[pl.ds(0, SUB)], sem_out.at[slot]).wait()
                pend[slot] = 0

        ystage[0] = jnp.zeros(ystage.shape[1:], U32)

        def tail_copy(b):
            return pltpu.make_async_copy(ystage.at[0], y_hbm.at[pl.ds(pl.multiple_of(b * SUB, SUB), SUB)], sem_out.at[0])

        def start(b, carry):
            tail_copy(b).start()
            return carry

        def wait(b, carry):
            tail_copy(b).wait()
            return carry

        first, stop = sbs_ref[n_sb] // SUB, y_hbm.shape[0] // SUB
        lax.fori_loop(first, stop, start, 0)
        lax.fori_loop(first, stop, wait, 0)


def _moe(n2p, n_rows, sb_tok, sb_e, sb_start, sb_n, w_gate_up, b_gate_up, w_down, b_down):
    half = n2p.shape[1]
    E, D, H2 = w_gate_up.shape
    n_sb = sb_e.shape[0]
    TH = MOE_TH
    n_hc = (H2 // 2) // TH
    hc_eff = lambda i, hc, sbn: jnp.where(sbn[i] > 0, hc, n_hc - 1)
    grid_spec = pltpu.PrefetchScalarGridSpec(
        num_scalar_prefetch=3,
        grid=(n_sb, n_hc),
        in_specs=[pl.BlockSpec((1, 1, MOE_TM), lambda i, hc, sbe, sbs, sbn: (i, 0, 0), memory_space=pltpu.SMEM),
                  pl.BlockSpec((1, 1, MOE_TM), lambda i, hc, sbe, sbs, sbn: (jnp.minimum(i + 1, n_sb - 1), 0, 0),
                               memory_space=pltpu.SMEM),
                  pl.BlockSpec(memory_space=pl.ANY),
                  pl.BlockSpec((1, D, 2 * TH), lambda i, hc, sbe, sbs, sbn: (sbe[i], 0, hc_eff(i, hc, sbn))),
                  pl.BlockSpec((1, 1, 2 * TH), lambda i, hc, sbe, sbs, sbn: (sbe[i], 0, hc_eff(i, hc, sbn))),
                  pl.BlockSpec((1, TH, D), lambda i, hc, sbe, sbs, sbn: (sbe[i], hc_eff(i, hc, sbn), 0)),
                  pl.BlockSpec((1, 1, D), lambda i, hc, sbe, sbs, sbn: (sbe[i], 0, 0))],
        out_specs=pl.BlockSpec(memory_space=pl.ANY),
        scratch_shapes=[pltpu.VMEM((MOE_TM, half), U32), pltpu.VMEM((MOE_TM, 2 * half), BF16),
                        pltpu.VMEM((MOE_TM, D), F32), pltpu.VMEM((2, MOE_SUB, half), U32),
                        pltpu.SMEM((2,), I32),
                        pltpu.SemaphoreType.DMA(()), pltpu.SemaphoreType.DMA((2,))],
    )
    return pl.pallas_call(
        _moe_body,
        out_shape=jax.ShapeDtypeStruct((n_rows, half), U32),
        grid_spec=grid_spec,
        compiler_params=_cparams(("arbitrary", "arbitrary")),
        name="moe",
    )(sb_e, sb_start, sb_n, sb_tok, sb_tok, n2p, w_gate_up, b_gate_up.reshape(E, 1, H2), w_down, b_down.reshape(E, 1, D))


COMBINE_TM = 256


def _combine_body(dest_ref, destn_ref, h_ref, tw_ref, g2_ref, nw_ref, y_hbm, o_ref, ybuf, sem):
    tm = COMBINE_TM
    n = TOP_K * tm
    i = pl.program_id(0)
    slot = i % 2

    def copy(d_ref, s, j):
        return pltpu.make_async_copy(y_hbm.at[pl.ds(d_ref[0, 0, j], 1)], ybuf.at[s, pl.ds(j, 1)], sem.at[s])

    def issue(d_ref, s):
        def start(j, carry):
            copy(d_ref, s, j).start()
            return carry

        lax.fori_loop(0, n, start, 0, unroll=8)

    @pl.when(i == 0)
    def _():
        issue(dest_ref, 0)

    @pl.when(i + 1 < pl.num_programs(0))
    def _():
        issue(destn_ref, 1 - slot)

    def wait(j, carry):
        copy(dest_ref, slot, j).wait()
        return carry

    lax.fori_loop(0, n, wait, 0, unroll=8)

    tw = tw_ref[...]
    lo = hi = None
    for kk in range(TOP_K):
        a, b = _unpack_bf16_pair(ybuf[slot, kk * tm:(kk + 1) * tm, :])
        wk = tw[:, kk:kk + 1]
        lo = wk * a if lo is None else lo + wk * a
        hi = wk * b if hi is None else hi + wk * b
    ffn = jnp.concatenate([lo, hi], axis=1)
    h = h_ref[...] + g2_ref[0] * ffn
    ms = jnp.mean(h * h, axis=-1, keepdims=True)
    o_ref[...] = h * lax.rsqrt(ms + EPS) * nw_ref[...]


def _combine(h1, y_sorted, dest, top_w, g2, final_norm_w, L):
    T, D = h1.shape
    tm = COMBINE_TM
    tiles = T // tm
    tpb = L // tm
    dest_tiles = dest.reshape(tiles, tm, TOP_K).transpose(0, 2, 1).reshape(tiles, 1, TOP_K * tm)
    return pl.pallas_call(
        _combine_body,
        out_shape=jax.ShapeDtypeStruct((T, D), F32),
        grid=(tiles,),
        in_specs=[pl.BlockSpec((1, 1, TOP_K * tm), lambda i: (i, 0, 0), memory_space=pltpu.SMEM),
                  pl.BlockSpec((1, 1, TOP_K * tm), lambda i: (jnp.minimum(i + 1, tiles - 1), 0, 0),
                               memory_space=pltpu.SMEM),
                  pl.BlockSpec((tm, D), lambda i: (i, 0)),
                  pl.BlockSpec((tm, LANES), lambda i: (i, 0)),
                  pl.BlockSpec((1, 1, D), lambda i: (i // tpb, 0, 0)),
                  pl.BlockSpec((1, D), lambda i: (0, 0)),
                  pl.BlockSpec(memory_space=pl.ANY)],
        out_specs=pl.BlockSpec((tm, D), lambda i: (i, 0)),
        scratch_shapes=[pltpu.VMEM((2, TOP_K * tm, D // 2), U32), pltpu.SemaphoreType.DMA((2,))],
        compiler_params=_cparams(("arbitrary",)),
        name="combine",
    )(dest_tiles, dest_tiles, h1, top_w, g2, final_norm_w.reshape(1, D), y_sorted)


def kernel(x, c, w_ada, b_ada, norm1_w, w_in, ssd_conv_w, ssd_conv_b, ssd_dt_bias, ssd_a_log, ssd_d_skip, ssd_norm_w, gla_w_gk2, gla_b_gk2, gla_norm_w, w_ssd_out, w_gla_out, w_out, norm2_w, w_router, b_router, w_gate_up, b_gate_up, w_down, b_down, final_norm_w):
    B, L, D = x.shape
    T = B * L
    assert D == D_MODEL and w_ada.shape[0] == 1
    x2d = x.reshape(T, D)

    mod = _ada(c, w_ada[0], b_ada[0])
    sh1, sc1, g1, sh2, sc2, g2 = [mod[:, i * D:(i + 1) * D].reshape(B, 1, D) for i in range(6)]

    w_t = w_in[0].T
    pad = lambda a: jnp.pad(a, ((0, LANES - a.shape[0]), (0, 0)))
    w_s = jnp.concatenate([pad(w_t[SRC_DT[0]:SRC_DT[1]]), pad(w_t[SRC_GK[0]:SRC_GK[1]])], axis=0)
    proj_a, n1 = _inproj_a(x2d, norm1_w[0].reshape(1, D), sc1, sh1, w_t, L)
    proj_b = _matmul(n1, w_t, SRC_B, PROJ_TN, "inproj_b")
    proj_c = _matmul(n1, w_t, SRC_C, PROJ_TN, "inproj_c")
    proj_s = _matmul(n1, w_s, (0, 2 * LANES), 2 * LANES, "inproj_s")
    pa3, pb3, ps3 = [p.reshape(B, L, p.shape[1]) for p in (proj_a, proj_b, proj_s)]
    y_ssd = _ssd(pa3, ps3, ssd_conv_w[0], ssd_conv_b[0], ssd_dt_bias[0], ssd_a_log[0], ssd_d_skip[0], ssd_norm_w[0])
    y_gla = _gla(pb3, ps3, gla_w_gk2[0], gla_b_gk2[0], gla_norm_w[0])
    merged = _merge(y_ssd.reshape(T, D), y_gla.reshape(T, D), proj_c, w_ssd_out[0].astype(BF16), w_gla_out[0].astype(BF16))
    h1, n2p, top_w, top_i, cnt = _post(merged, x2d, g1, norm2_w[0].reshape(1, D), sc2, sh2, w_out[0].astype(BF16),
                                       w_router[0], b_router[0], L)

    A = T * TOP_K
    n_rows = A + N_EXPERTS * MOE_SUB
    n_sb = N_EXPERTS + n_rows // MOE_TM
    dest, sb_tok, sb_e, sb_start, sb_n = _routing(top_i[:, :TOP_K], top_i[:, TOP_K:2 * TOP_K],
                                                  cnt[0, :N_EXPERTS].astype(I32), n_sb)
    y_sorted = _moe(n2p, n_rows, sb_tok, sb_e, sb_start, sb_n, w_gate_up[0], b_gate_up[0], w_down[0], b_down[0])
    out = _combine(h1, y_sorted, dest, top_w, g2, final_norm_w, L)
    return out.reshape(B, L, D)
```

```python
import functools

import jax
import jax.numpy as jnp
import numpy as np
from jax import lax
from jax.experimental import pallas as pl
from jax.experimental.pallas import tpu as pltpu

F32 = jnp.float32
BF16 = jnp.bfloat16
U32 = jnp.uint32
I32 = jnp.int32

EPS = 1e-6
LANES = 128
VMEM_LIMIT = 56 * 1024 * 1024

D_MODEL = 2048
SSD_HEADS, SSD_P, SSD_G, SSD_N, SSD_CONV, SSD_Q = 32, 64, 4, 128, 4, 256
SSD_INNER = SSD_HEADS * SSD_P
SSD_TAIL = 16
GLA_H, GLA_DK, GLA_DV, GLA_RANK, GLA_C = 4, 256, 512, 16, 64
GLA_GATE_NORM = 16.0
N_EXPERTS, TOP_K = 32, 4
SWIGLU_LIMIT, SWIGLU_ALPHA = 7.0, 1.702

SRC_A = (0, 5120)
SRC_DT = (5120, 5152)
SRC_B = (5152, 11296)
SRC_GK = (11296, 11312)
SRC_C = (11312, 15408)
A_Z, A_XS, A_BC = 0, 2048, 4096
B_Q, B_K, B_V, B_GO = 0, 1024, 2048, 4096
C_GS, C_GG = 0, 2048
PROJ_TN = 1024

MOE_SUB = 256
MOE_TM = 2048
MOE_TH = 256


def _cparams(sem, vmem=VMEM_LIMIT):
    return pltpu.CompilerParams(dimension_semantics=sem, vmem_limit_bytes=vmem)


def _split3(x):
    hi = x.astype(BF16)
    r1 = x - hi.astype(F32)
    mid = r1.astype(BF16)
    lo = (r1 - mid.astype(F32)).astype(BF16)
    return hi, mid, lo


def _dot(a, b):
    return jnp.dot(a, b, preferred_element_type=F32)


def _dot_nt(a, b):
    return lax.dot_general(a, b, (((1,), (1,)), ((), ())), preferred_element_type=F32)


def _dot_tn(a, b):
    return lax.dot_general(a, b, (((0,), (0,)), ((), ())), preferred_element_type=F32)


def _dot_exact_lhs(a01, x):
    hi, mid, lo = _split3(x)
    return _dot(a01, hi) + _dot(a01, mid) + _dot(a01, lo)


def _dot_hilo_rhs(x, b01):
    hi = x.astype(BF16)
    lo = (x - hi.astype(F32)).astype(BF16)
    return _dot(hi, b01) + _dot(lo, b01)


def _silu(x):
    return x * jax.nn.sigmoid(x)


def _softplus(x):
    return jnp.maximum(x, 0.0) + jnp.log1p(jnp.exp(-jnp.abs(x)))


def _ada_body(ct_ref, w_ref, b_ref, o_ref):
    ct = ct_ref[...]
    cs = _silu(ct)
    w = w_ref[...]
    for m in range(ct.shape[1]):
        o_ref[m:m + 1, :] = jnp.sum(w * cs[:, m:m + 1], axis=0, keepdims=True) + b_ref[...]


def _ada(c, w_ada, b_ada):
    B, D = c.shape
    N = w_ada.shape[1]
    tn = 1024
    return pl.pallas_call(
        _ada_body,
        out_shape=jax.ShapeDtypeStruct((B, N), F32),
        grid=(N // tn,),
        in_specs=[pl.BlockSpec((D, B), lambda j: (0, 0)),
                  pl.BlockSpec((D, tn), lambda j: (0, j)),
                  pl.BlockSpec((1, tn), lambda j: (0, j))],
        out_specs=pl.BlockSpec((B, tn), lambda j: (0, j)),
        compiler_params=_cparams(("arbitrary",)),
        name="ada",
    )(c.T, w_ada, b_ada.reshape(1, N))


def _inproj_a_body(x_ref, nw_ref, sc_ref, sh_ref, w_ref, o_ref, n_ref):
    @pl.when(pl.program_id(1) == 0)
    def _():
        x = x_ref[...]
        ms = jnp.mean(x * x, axis=-1, keepdims=True)
        y = x * lax.rsqrt(ms + EPS) * nw_ref[...]
        n_ref[...] = (y * (1.0 + sc_ref[0]) + sh_ref[0]).astype(BF16)

    o_ref[...] = _dot_nt(n_ref[...], w_ref[...].astype(BF16)).astype(BF16)


def _inproj_a(x2d, norm_w, sc, sh, w_t, L):
    T, D = x2d.shape
    tm, tn = 1024, PROJ_TN // 2
    n_cols = SRC_A[1] - SRC_A[0]
    tiles_per_batch = L // tm
    return pl.pallas_call(
        _inproj_a_body,
        out_shape=(jax.ShapeDtypeStruct((T, n_cols), BF16), jax.ShapeDtypeStruct((T, D), BF16)),
        grid=(T // tm, n_cols // tn),
        in_specs=[pl.BlockSpec((tm, D), lambda i, j: (i, 0)),
                  pl.BlockSpec((1, D), lambda i, j: (0, 0)),
                  pl.BlockSpec((1, 1, D), lambda i, j: (i // tiles_per_batch, 0, 0)),
                  pl.BlockSpec((1, 1, D), lambda i, j: (i // tiles_per_batch, 0, 0)),
                  pl.BlockSpec((tn, D), lambda i, j: (SRC_A[0] // tn + j, 0))],
        out_specs=(pl.BlockSpec((tm, tn), lambda i, j: (i, j)), pl.BlockSpec((tm, D), lambda i, j: (i, 0))),
        compiler_params=_cparams(("arbitrary", "arbitrary")),
        name="inproj_a",
    )(x2d, norm_w, sc, sh, w_t)


def _matmul_body(a_ref, w_ref, o_ref):
    o_ref[...] = _dot_nt(a_ref[...], w_ref[...].astype(BF16)).astype(BF16)


def _matmul(a, w_t, rows, tn, name):
    T, D = a.shape
    N = rows[1] - rows[0]
    tm = 1024
    return pl.pallas_call(
        _matmul_body,
        out_shape=jax.ShapeDtypeStruct((T, N), BF16),
        grid=(T // tm, N // tn),
        in_specs=[pl.BlockSpec((tm, D), lambda i, j: (i, 0)),
                  pl.BlockSpec((pl.Element(tn), pl.Element(D)),
                               lambda i, j: ((rows[0] // 8 + j * (tn // 8)) * 8, 0))],
        out_specs=pl.BlockSpec((tm, tn), lambda i, j: (i, j)),
        compiler_params=_cparams(("arbitrary", "arbitrary")),
        name=name,
    )(a, w_t)


def _ssd_body(z_ref, xs_ref, bc_ref, dt_ref, cwx_ref, cwbc_ref, cbx_ref, cbbc_ref, dtb_ref, alog_ref,
              dskip_ref, nw_ref, exp_ref, sh_ref, o_ref, ubx, ubbc, state):
    Q = SSD_Q
    c = pl.program_id(1)

    @pl.when(c == 0)
    def _():
        ubx[...] = jnp.zeros(ubx.shape, F32)
        ubbc[...] = jnp.zeros(ubbc.shape, F32)
        state[...] = jnp.zeros(state.shape, F32)

    def conv_silu(u_ref, tail, cw_ref, cb_ref):
        u = u_ref[0]
        head = u_ref[0, 0:SSD_TAIL, :].astype(F32)[0:8, :]
        prev = tail[...]
        r8 = lax.broadcasted_iota(I32, head.shape, 0)
        w_last = cw_ref[SSD_CONV - 1:SSD_CONV, :]
        acc = cb_ref[...] + w_last * u.astype(F32)
        acc8 = cb_ref[...] + w_last * head
        for s in range(1, SSD_CONV):
            w_s = cw_ref[SSD_CONV - 1 - s:SSD_CONV - s, :]
            acc = acc + w_s * _dot(sh_ref[s - 1], u)
            acc8 = acc8 + w_s * jnp.where(r8 < s, pltpu.roll(prev, s, 0), pltpu.roll(head, s, 0))
        tail[...] = u_ref[0, Q - SSD_TAIL:Q, :].astype(F32)[SSD_TAIL - 8:, :]
        return _silu(jnp.concatenate([acc8, acc[8:, :]], axis=0))

    xs = conv_silu(xs_ref, ubx, cwx_ref, cbx_ref)
    bcm = conv_silu(bc_ref, ubbc, cwbc_ref, cbbc_ref)
    GN = SSD_G * SSD_N
    bm_f, cm = bcm[:, :GN], bcm[:, GN:].astype(BF16)

    dt = _softplus(dt_ref[0].astype(F32) + dtb_ref[...])
    a_neg = -jnp.exp(alog_ref[...])
    dA = dt * a_neg

    row = lax.broadcasted_iota(I32, (Q, Q), 0)
    col = lax.broadcasted_iota(I32, (Q, Q), 1)
    causal = row >= col
    tri = jnp.where(causal, 1.0, 0.0).astype(BF16)
    acs = _dot_exact_lhs(tri, dA)
    acs_t = acs.T
    acs_last = acs[Q - 1:Q, :]
    exp_a = jnp.exp(acs)
    decay_st = jnp.exp(acs_last - acs)

    expand = exp_ref[...]
    dt_e = _dot_hilo_rhs(dt, expand)
    dtd_e = _dot_hilo_rhs(dt * decay_st, expand)
    expa_e = _dot_hilo_rhs(exp_a, expand)

    x_dt = (xs * dt_e).astype(BF16)
    x_dd = (xs * dtd_e).astype(BF16)

    lane = lax.broadcasted_iota(I32, (Q, LANES), 1)
    lo_half = lane < SSD_P
    HG = SSD_HEADS // SSD_G
    GW = HG * SSD_P
    y_groups = []
    for g in range(SSD_G):
        cg = cm[:, g * SSD_N:(g + 1) * SSD_N]
        bg_f = bm_f[:, g * SSD_N:(g + 1) * SSD_N]
        bg = bg_f.astype(BF16)
        scores = _dot_nt(cg, bg)
        pieces = []
        for p in range(HG // 2):
            h0 = g * HG + 2 * p
            xp = x_dt[:, h0 * SSD_P:h0 * SSD_P + LANES]
            acc = None
            for s in range(2):
                h = h0 + s
                seg = acs[:, h:h + 1] - acs_t[h:h + 1, :]
                lmat = jnp.exp(jnp.where(causal, seg, -jnp.inf))
                m = (scores * lmat).astype(BF16)
                xh = jnp.where(lo_half if s == 0 else jnp.logical_not(lo_half), xp, jnp.zeros_like(xp))
                part = _dot(m, xh)
                acc = part if acc is None else acc + part
            pieces.append(acc)
        y_diag = jnp.concatenate(pieces, axis=1)
        st_old = state[g]
        y_off = _dot(cg, st_old.astype(BF16)) * expa_e[:, g * GW:(g + 1) * GW]
        st_new = _dot(bg_f.T.astype(BF16), x_dd[:, g * GW:(g + 1) * GW])
        state[g] = st_old * expa_e[Q - 1:Q, g * GW:(g + 1) * GW] + st_new
        y_groups.append(y_diag + y_off)

    y = jnp.concatenate(y_groups, axis=1) + xs * dskip_ref[...]
    y = y * _silu(z_ref[0].astype(F32))
    outs = []
    for g in range(SSD_G):
        yg = y[:, g * GW:(g + 1) * GW]
        ms = jnp.mean(yg * yg, axis=-1, keepdims=True)
        outs.append(yg * lax.rsqrt(ms + EPS) * nw_ref[:, g * GW:(g + 1) * GW])
    o_ref[0] = jnp.concatenate(outs, axis=1).astype(BF16)


def _ssd(pa3, ps3, conv_w, conv_b, dt_bias, a_log, d_skip, norm_w):
    B, L, _ = pa3.shape
    Q, DI, BCW = SSD_Q, SSD_INNER, 2 * SSD_G * SSD_N
    padl = lambda a: jnp.pad(a.reshape(1, -1), ((0, 0), (0, LANES - a.shape[-1])))
    expand = (np.arange(LANES)[:, None] == (np.arange(DI)[None, :] // SSD_P)).astype(np.float32)
    t = np.arange(Q)[:, None]
    shifts = np.stack([(t - s == np.arange(Q)[None, :]) for s in range(1, SSD_CONV)]).astype(np.float32)
    const = lambda shape: pl.BlockSpec(shape, lambda b, c: (0,) * len(shape))
    return pl.pallas_call(
        _ssd_body,
        out_shape=jax.ShapeDtypeStruct((B, L, DI), BF16),
        grid=(B, L // Q),
        in_specs=[pl.BlockSpec((1, Q, DI), lambda b, c: (b, c, A_Z // DI)),
                  pl.BlockSpec((1, Q, DI), lambda b, c: (b, c, A_XS // DI)),
                  pl.BlockSpec((1, Q, BCW), lambda b, c: (b, c, A_BC // BCW)),
                  pl.BlockSpec((1, Q, LANES), lambda b, c: (b, c, 0)),
                  const((SSD_CONV, DI)), const((SSD_CONV, BCW)), const((1, DI)), const((1, BCW)),
                  const((1, LANES)), const((1, LANES)), const((1, DI)), const((1, DI)),
                  const((LANES, DI)), const((SSD_CONV - 1, Q, Q))],
        out_specs=pl.BlockSpec((1, Q, DI), lambda b, c: (b, c, 0)),
        scratch_shapes=[pltpu.VMEM((8, DI), F32), pltpu.VMEM((8, BCW), F32),
                        pltpu.VMEM((SSD_G, SSD_N, DI // SSD_G), F32)],
        compiler_params=_cparams(("arbitrary", "arbitrary")),
        name="ssd",
    )(pa3, pa3, pa3, ps3, conv_w[:, :DI], conv_w[:, DI:], conv_b[:DI].reshape(1, DI),
      conv_b[DI:].reshape(1, BCW), padl(dt_bias), padl(a_log), jnp.repeat(d_skip, SSD_P).reshape(1, DI),
      norm_w.reshape(1, DI), jnp.asarray(expand, BF16), jnp.asarray(shifts, BF16))


GLA_RB = 256


def _gla_body(q_ref, k_ref, v_ref, g_ref, gk_ref, w_ref, b_ref, nw_ref, o_ref, st):
    RB, C = GLA_RB, GLA_C

    NB = q_ref.shape[0]

    @pl.when(pl.program_id(0) == 0)
    def _():
        st[...] = jnp.zeros(st.shape, F32)

    w_hi, w_mid, w_lo = _split3(w_ref[...])
    row = lax.broadcasted_iota(I32, (RB, RB), 0)
    col = lax.broadcasted_iota(I32, (RB, RB), 1)
    blocktri = jnp.where((row // C == col // C) & (row >= col), 1.0, 0.0).astype(BF16)
    bcs_all = []
    for b in range(NB):
        gk = gk_ref[b]
        pre = _dot(gk, w_hi) + _dot(gk, w_mid) + _dot(gk, w_lo) + b_ref[...]
        log_a = (jnp.minimum(pre, 0.0) - jnp.log1p(jnp.exp(-jnp.abs(pre)))) * (1.0 / GLA_GATE_NORM)
        bcs_all.append(_dot_exact_lhs(blocktri, log_a))

    r64 = lax.broadcasted_iota(I32, (C, C), 0)
    c64 = lax.broadcasted_iota(I32, (C, C), 1)
    causal = r64 >= c64
    scale = GLA_DK ** -0.5

    for s in range(RB // C):
        rs = slice(s * C, (s + 1) * C)
        q_dec, k_inv, k_st, cdec = [], [], [], []
        for b in range(NB):
            bcs = bcs_all[b][rs, :]
            last = bcs[C - 1:C, :]
            qf = q_ref[b, rs, :].astype(F32)
            kf = k_ref[b, rs, :].astype(F32)
            q_dec.append((qf * scale * jnp.exp(bcs)).astype(BF16))
            k_inv.append((kf * jnp.exp(-bcs)).astype(BF16))
            k_st.append((kf * jnp.exp(last - bcs)).astype(BF16))
            cdec.append(jnp.exp(last))
        for h in range(GLA_H):
            ks = slice(h * GLA_DK, (h + 1) * GLA_DK)
            vs = slice(h * GLA_DV, (h + 1) * GLA_DV)
            for b in range(NB):
                vh = v_ref[b, rs, vs]
                attn = jnp.where(causal, _dot_nt(q_dec[b][:, ks], k_inv[b][:, ks]), 0.0)
                st_h = st[b, h]
                o = _dot(attn.astype(BF16), vh) + _dot_nt(q_dec[b][:, ks], st_h.astype(BF16))
                st[b, h] = st_h * cdec[b][:, ks] + _dot_tn(vh, k_st[b][:, ks])
                ms = jnp.mean(o * o, axis=-1, keepdims=True)
                o = o * lax.rsqrt(ms + EPS) * nw_ref[...]
                o_ref[b, rs, vs] = (o * _silu(g_ref[b, rs, vs].astype(F32))).astype(BF16)


def _gla(pb3, ps3, w_gk2, b_gk2, norm_w):
    B, L, _ = pb3.shape
    RB, KD, VD = GLA_RB, GLA_H * GLA_DK, GLA_H * GLA_DV
    w_pad = jnp.pad(w_gk2, ((0, LANES - w_gk2.shape[0]), (0, 0)))
    const = lambda shape: pl.BlockSpec(shape, lambda c: (0,) * len(shape))
    return pl.pallas_call(
        _gla_body,
        out_shape=jax.ShapeDtypeStruct((B, L, VD), BF16),
        grid=(L // RB,),
        in_specs=[pl.BlockSpec((B, RB, KD), lambda c: (0, c, B_Q // KD)),
                  pl.BlockSpec((B, RB, KD), lambda c: (0, c, B_K // KD)),
                  pl.BlockSpec((B, RB, VD), lambda c: (0, c, B_V // VD)),
                  pl.BlockSpec((B, RB, VD), lambda c: (0, c, B_GO // VD)),
                  pl.BlockSpec((B, RB, LANES), lambda c: (0, c, 1)),
                  const((LANES, KD)), const((1, KD)), const((1, GLA_DV))],
        out_specs=pl.BlockSpec((B, RB, VD), lambda c: (0, c, 0)),
        scratch_shapes=[pltpu.VMEM((B, GLA_H, GLA_DV, GLA_DK), F32)],
        compiler_params=_cparams(("arbitrary",)),
        name="gla",
    )(pb3, pb3, pb3, pb3, ps3, w_pad, b_gk2.reshape(1, KD), norm_w.reshape(1, GLA_DV))


def _merge_body(ys_ref, yg_ref, gs_ref, gg_ref, ws_ref, wg_ref, o_ref):
    a = _dot(ys_ref[...], ws_ref[...])
    b = _dot(yg_ref[...], wg_ref[...])
    m = jax.nn.sigmoid(gs_ref[...].astype(F32)) * a + jax.nn.sigmoid(gg_ref[...].astype(F32)) * b
    o_ref[...] = m.astype(BF16)


def _merge(y_ssd, y_gla, proj_c, w_ssd_out, w_gla_out):
    T, D = y_ssd.shape
    tm, tn = 512, 1024
    return pl.pallas_call(
        _merge_body,
        out_shape=jax.ShapeDtypeStruct((T, D), BF16),
        grid=(D // tn, T // tm),
        in_specs=[pl.BlockSpec((tm, D), lambda j, i: (i, 0)),
                  pl.BlockSpec((tm, D), lambda j, i: (i, 0)),
                  pl.BlockSpec((tm, tn), lambda j, i: (i, C_GS // tn + j)),
                  pl.BlockSpec((tm, tn), lambda j, i: (i, C_GG // tn + j)),
                  pl.BlockSpec((D, tn), lambda j, i: (0, j)),
                  pl.BlockSpec((D, tn), lambda j, i: (0, j))],
        out_specs=pl.BlockSpec((tm, tn), lambda j, i: (i, j)),
        compiler_params=_cparams(("arbitrary", "arbitrary")),
        name="merge",
    )(y_ssd, y_gla, proj_c, proj_c, w_ssd_out, w_gla_out)


def _pack_bf16_pair(a, b):
    ua = lax.bitcast_convert_type(a.astype(BF16).astype(F32), U32)
    ub = lax.bitcast_convert_type(b.astype(BF16).astype(F32), U32)
    return (ua & jnp.uint32(0xFFFF0000)) | (ub >> 16)


def _unpack_bf16_pair(w):
    a = lax.bitcast_convert_type(w & jnp.uint32(0xFFFF0000), F32)
    b = lax.bitcast_convert_type(w << 16, F32)
    return a, b


def _post_body(m_ref, x_ref, g1_ref, nw_ref, sc_ref, sh_ref, wo_ref, wr_ref, br_ref, tri_ref,
               h_ref, n2p_ref, tw_ref, ti_ref, cnt_ref):
    D = x_ref.shape[1]
    h = x_ref[...] + g1_ref[0] * _dot(m_ref[...], wo_ref[...])
    h_ref[...] = h
    ms = jnp.mean(h * h, axis=-1, keepdims=True)
    n2 = h * lax.rsqrt(ms + EPS) * nw_ref[...] * (1.0 + sc_ref[0]) + sh_ref[0]
    n2p_ref[...] = _pack_bf16_pair(n2[:, :D // 2], n2[:, D // 2:])

    n_hi = n2.astype(BF16)
    n_lo = (n2 - n_hi.astype(F32)).astype(BF16)
    wr = wr_ref[...]
    w_hi = wr.astype(BF16)
    w_lo = (wr - w_hi.astype(F32)).astype(BF16)
    logits = _dot(n_hi, w_hi) + _dot(n_hi, w_lo) + _dot(n_lo, w_hi) + br_ref[...]

    lane = lax.broadcasted_iota(I32, logits.shape, 1)
    cur = jnp.where(lane < N_EXPERTS, logits, -jnp.inf)
    vals, idxs = [], []
    for _ in range(TOP_K):
        mx = jnp.max(cur, axis=-1, keepdims=True)
        ix = jnp.min(jnp.where(cur == mx, lane, LANES), axis=-1, keepdims=True)
        vals.append(mx)
        idxs.append(ix)
        cur = jnp.where(lane == ix, -jnp.inf, cur)
    es = [jnp.exp(v - vals[0]) for v in vals]
    denom = es[0] + es[1] + es[2] + es[3]

    @pl.when(pl.program_id(0) == 0)
    def _():
        cnt_ref[...] = jnp.zeros(cnt_ref.shape, F32)

    hits = [lane == ix for ix in idxs]
    onehot = jnp.where(hits[0] | hits[1] | hits[2] | hits[3], 1.0, 0.0)
    before = _dot(tri_ref[...], onehot.astype(BF16)) + cnt_ref[0:1, :]
    cnt_ref[...] = cnt_ref[...] + jnp.sum(onehot, axis=0, keepdims=True)

    tw = jnp.zeros(logits.shape, F32)
    ti = jnp.zeros(logits.shape, I32)
    for kk in range(TOP_K):
        rank = jnp.sum(jnp.where(hits[kk], before, 0.0), axis=-1, keepdims=True).astype(I32)
        tw = jnp.where(lane == kk, es[kk] / denom, tw)
        ti = jnp.where(lane == kk, idxs[kk], jnp.where(lane == TOP_K + kk, rank, ti))
    tw_ref[...] = tw
    ti_ref[...] = ti


def _post(merged, x2d, g1, norm_w, sc, sh, w_out, w_router, b_router, L):
    T, D = x2d.shape
    tm = 512
    tpb = L // tm
    wr = jnp.pad(w_router, ((0, 0), (0, LANES - w_router.shape[1])))
    br = jnp.pad(b_router.reshape(1, -1), ((0, 0), (0, LANES - b_router.shape[0])))
    row = lambda w: pl.BlockSpec((tm, w), lambda i: (i, 0))
    per_b = pl.BlockSpec((1, 1, D), lambda i: (i // tpb, 0, 0))
    const = lambda shape: pl.BlockSpec(shape, lambda i: (0,) * len(shape))
    tri = np.tril(np.ones((tm, tm), np.float32), -1)
    return pl.pallas_call(
        _post_body,
        out_shape=(jax.ShapeDtypeStruct((T, D), F32), jax.ShapeDtypeStruct((T, D // 2), U32),
                   jax.ShapeDtypeStruct((T, LANES), F32), jax.ShapeDtypeStruct((T, LANES), I32),
                   jax.ShapeDtypeStruct((8, LANES), F32)),
        grid=(T // tm,),
        in_specs=[row(D), row(D), per_b, const((1, D)), per_b, per_b, const((D, D)), const((D, LANES)),
                  const((1, LANES)), const((tm, tm))],
        out_specs=(row(D), row(D // 2), row(LANES), row(LANES), const((8, LANES))),
        compiler_params=_cparams(("arbitrary",)),
        name="post",
    )(merged, x2d, g1, norm_w, sc, sh, w_out, wr, br, jnp.asarray(tri, BF16))


def _routing(top_idx, top_rank, counts, n_sb):
    T = top_idx.shape[0]
    A = T * TOP_K
    flat_e = top_idx.reshape(A)
    rank = top_rank.reshape(A)
    padded = (counts + MOE_SUB - 1) // MOE_SUB * MOE_SUB
    pend = jnp.cumsum(padded)
    pstart = pend - padded
    dest = (pstart[flat_e] + rank).astype(I32)

    nsb = (padded + MOE_TM - 1) // MOE_TM
    sb_end = jnp.cumsum(nsb)
    sb_first = sb_end - nsb
    slot = (sb_first[flat_e] + rank // MOE_TM) * MOE_TM + rank % MOE_TM
    sb_tok = jnp.zeros((n_sb * MOE_TM,), I32).at[slot].set(jnp.arange(A, dtype=I32) // TOP_K)

    total = sb_end[-1]
    i = jnp.arange(n_sb, dtype=I32)
    e_of = jnp.minimum(jnp.searchsorted(sb_end, i, side="right"), N_EXPERTS - 1).astype(I32)
    local = i - sb_first[e_of]
    valid = i < total
    last_e = e_of[jnp.maximum(total - 1, 0)]
    sb_e = jnp.where(valid, e_of, last_e).astype(I32)
    sb_start = jnp.where(valid, pstart[e_of] + local * MOE_TM, 0).astype(I32)
    sb_start = jnp.concatenate([sb_start, pend[-1:].astype(I32)])
    sb_n = jnp.where(valid, jnp.clip(padded[e_of] - local * MOE_TM, 0, MOE_TM), 0).astype(I32)
    return dest, sb_tok.reshape(n_sb, 1, MOE_TM), sb_e, sb_start, sb_n


def _deinterleave(gu):
    rows, two_w = gu.shape
    lane = lax.broadcasted_iota(I32, (rows, LANES), 1)
    idx_e = (2 * lane) % LANES
    idx_o = idx_e + 1
    first = lane < LANES // 2
    gates, ups = [], []
    for p in range(two_w // (2 * LANES)):
        a = gu[:, (2 * p) * LANES:(2 * p + 1) * LANES]
        b = gu[:, (2 * p + 1) * LANES:(2 * p + 2) * LANES]
        gates.append(jnp.where(first, jnp.take_along_axis(a, idx_e, axis=1), jnp.take_along_axis(b, idx_e, axis=1)))
        ups.append(jnp.where(first, jnp.take_along_axis(a, idx_o, axis=1), jnp.take_along_axis(b, idx_o, axis=1)))
    return jnp.concatenate(gates, axis=1), jnp.concatenate(ups, axis=1)


MOE_DMA_UNROLL = 16


def _moe_body(sbe_ref, sbs_ref, sbn_ref, tok_ref, tokn_ref, x_hbm, wgu_ref, bgu_ref, wd_ref, bd_ref,
              y_hbm, xs_buf, xb_buf, acc, ystage, pend, sem_in, sem_out):
    i = pl.program_id(0)
    hc = pl.program_id(1)
    n_sb = pl.num_programs(0)
    n_hc = pl.num_programs(1)
    SUB = MOE_SUB
    half = xs_buf.shape[1]
    nblk = sbn_ref[i] // SUB
    start_row = sbs_ref[i]

    def row_copy(t_ref, g, u):
        base = pl.multiple_of(g * MOE_DMA_UNROLL, MOE_DMA_UNROLL)
        return pltpu.make_async_copy(x_hbm.at[pl.ds(t_ref[0, 0, base + u], 1)], xs_buf.at[pl.ds(base + u, 1)], sem_in)

    def start_in(t_ref, n_rows):
        def body(g, carry):
            for u in range(MOE_DMA_UNROLL):
                row_copy(t_ref, g, u).start()
            return carry

        lax.fori_loop(0, n_rows // MOE_DMA_UNROLL, body, 0)

    @pl.when(hc == 0)
    def _():
        @pl.when(i == 0)
        def _():
            pend[0] = 0
            pend[1] = 0
            start_in(tok_ref, sbn_ref[0])

        def wait_in(g, carry):
            for u in range(MOE_DMA_UNROLL):
                row_copy(tok_ref, g, u).wait()
            return carry

        lax.fori_loop(0, sbn_ref[i] // MOE_DMA_UNROLL, wait_in, 0)

        def convert(r, carry):
            rows = pl.ds(pl.multiple_of(r * SUB, SUB), SUB)
            a, b = _unpack_bf16_pair(xs_buf[rows, :])
            xb_buf[rows, 0:half] = a.astype(BF16)
            xb_buf[rows, half:2 * half] = b.astype(BF16)
            return carry

        lax.fori_loop(0, nblk, convert, 0)

        @pl.when(i + 1 < n_sb)
        def _():
            start_in(tokn_ref, sbn_ref[i + 1])

    def accumulate(blocks, first):
        rows = [pl.ds(pl.multiple_of(r * SUB, SUB), SUB) for r in blocks]
        wg = wgu_ref[0].astype(BF16)
        wd = wd_ref[0].astype(BF16)
        gus = [_dot(xb_buf[rw, :], wg) + bgu_ref[0] for rw in rows]
        hs = []
        for gu in gus:
            gate, up = _deinterleave(gu)
            gate = jnp.minimum(gate, SWIGLU_LIMIT)
            up = jnp.clip(up, -SWIGLU_LIMIT, SWIGLU_LIMIT)
            hs.append((gate * jax.nn.sigmoid(gate * SWIGLU_ALPHA) * (up + 1.0)).astype(BF16))
        for rw, hidden in zip(rows, hs):
            v = _dot(hidden, wd)
            if first:
                acc[rw, :] = v + bd_ref[0]
            else:
                acc[rw, :] += v

    def run(first):
        def quad(p, carry):
            accumulate([4 * p, 4 * p + 1, 4 * p + 2, 4 * p + 3], first)
            return carry

        lax.fori_loop(0, nblk // 4, quad, 0)
        done = nblk // 4 * 4

        @pl.when(nblk - done >= 2)
        def _():
            accumulate([done, done + 1], first)

        @pl.when(nblk % 2 == 1)
        def _():
            accumulate([nblk - 1], first)

    @pl.when(nblk > 0)
    def _():
        @pl.when(hc == 0)
        def _():
            run(True)

        @pl.when(hc != 0)
        def _():
            run(False)

    @pl.when(hc == n_hc - 1)
    def _():
        def out_copy(r, slot):
            dst = pl.multiple_of(start_row + r * SUB, SUB)
            return pltpu.make_async_copy(ystage.at[slot], y_hbm.at[pl.ds(dst, SUB)], sem_out.at[slot])

        def body(r, carry):
            slot = r % 2

            @pl.when(pend[slot] == 1)
            def _():
                out_copy(r, slot).wait()

            rows = pl.ds(pl.multiple_of(r * SUB, SUB), SUB)
            v = acc[rows, :]
            ystage[slot] = _pack_bf16_pair(v[:, :half], v[:, half:])
            out_copy(r, slot).start()
            pend[slot] = 1
            return carry

        lax.fori_loop(0, nblk, body, 0)

    @pl.when((i == n_sb - 1) & (hc == n_hc - 1))
    def _():
        for slot in range(2):
            @pl.when(pend[slot] == 1)
            def _():
                pltpu.make_async_copy(ystage.at[slot], y_hbm.at[pl.ds(0, SUB)], sem_out.at[slot]).wait()
                pend[slot] = 0

        ystage[0] = jnp.zeros(ystage.shape[1:], U32)

        def tail_copy(b):
            return pltpu.make_async_copy(ystage.at[0], y_hbm.at[pl.ds(pl.multiple_of(b * SUB, SUB), SUB)], sem_out.at[0])

        def start(b, carry):
            tail_copy(b).start()
            return carry

        def wait(b, carry):
            tail_copy(b).wait()
            return carry

        first, stop = sbs_ref[n_sb] // SUB, y_hbm.shape[0] // SUB
        lax.fori_loop(first, stop, start, 0)
        lax.fori_loop(first, stop, wait, 0)


def _moe(n2p, n_rows, sb_tok, sb_e, sb_start, sb_n, w_gate_up, b_gate_up, w_down, b_down):
    half = n2p.shape[1]
    E, D, H2 = w_gate_up.shape
    n_sb = sb_e.shape[0]
    TH = MOE_TH
    n_hc = (H2 // 2) // TH
    hc_eff = lambda i, hc, sbn: jnp.where(sbn[i] > 0, hc, n_hc - 1)
    grid_spec = pltpu.PrefetchScalarGridSpec(
        num_scalar_prefetch=3,
        grid=(n_sb, n_hc),
        in_specs=[pl.BlockSpec((1, 1, MOE_TM), lambda i, hc, sbe, sbs, sbn: (i, 0, 0), memory_space=pltpu.SMEM),
                  pl.BlockSpec((1, 1, MOE_TM), lambda i, hc, sbe, sbs, sbn: (jnp.minimum(i + 1, n_sb - 1), 0, 0),
                               memory_space=pltpu.SMEM),
                  pl.BlockSpec(memory_space=pl.ANY),
                  pl.BlockSpec((1, D, 2 * TH), lambda i, hc, sbe, sbs, sbn: (sbe[i], 0, hc_eff(i, hc, sbn))),
                  pl.BlockSpec((1, 1, 2 * TH), lambda i, hc, sbe, sbs, sbn: (sbe[i], 0, hc_eff(i, hc, sbn))),
                  pl.BlockSpec((1, TH, D), lambda i, hc, sbe, sbs, sbn: (sbe[i], hc_eff(i, hc, sbn), 0)),
                  pl.BlockSpec((1, 1, D), lambda i, hc, sbe, sbs, sbn: (sbe[i], 0, 0))],
        out_specs=pl.BlockSpec(memory_space=pl.ANY),
        scratch_shapes=[pltpu.VMEM((MOE_TM, half), U32), pltpu.VMEM((MOE_TM, 2 * half), BF16),
                        pltpu.VMEM((MOE_TM, D), F32), pltpu.VMEM((2, MOE_SUB, half), U32),
                        pltpu.SMEM((2,), I32),
                        pltpu.SemaphoreType.DMA(()), pltpu.SemaphoreType.DMA((2,))],
    )
    return pl.pallas_call(
        _moe_body,
        out_shape=jax.ShapeDtypeStruct((n_rows, half), U32),
        grid_spec=grid_spec,
        compiler_params=_cparams(("arbitrary", "arbitrary")),
        name="moe",
    )(sb_e, sb_start, sb_n, sb_tok, sb_tok, n2p, w_gate_up, b_gate_up.reshape(E, 1, H2), w_down, b_down.reshape(E, 1, D))


COMBINE_TM = 256


def _combine_body(dest_ref, h_ref, tw_ref, g2_ref, nw_ref, y_hbm, o_ref, ybuf, sem):
    tm = COMBINE_TM
    n = TOP_K * tm

    def copy(j):
        return pltpu.make_async_copy(y_hbm.at[pl.ds(dest_ref[0, 0, j], 1)], ybuf.at[pl.ds(j, 1)], sem)

    def start(j, carry):
        copy(j).start()
        return carry

    def wait(j, carry):
        copy(j).wait()
        return carry

    lax.fori_loop(0, n, start, 0, unroll=8)
    lax.fori_loop(0, n, wait, 0, unroll=8)

    tw = tw_ref[...]
    lo = hi = None
    for kk in range(TOP_K):
        a, b = _unpack_bf16_pair(ybuf[kk * tm:(kk + 1) * tm, :])
        wk = tw[:, kk:kk + 1]
        lo = wk * a if lo is None else lo + wk * a
        hi = wk * b if hi is None else hi + wk * b
    ffn = jnp.concatenate([lo, hi], axis=1)
    h = h_ref[...] + g2_ref[0] * ffn
    ms = jnp.mean(h * h, axis=-1, keepdims=True)
    o_ref[...] = h * lax.rsqrt(ms + EPS) * nw_ref[...]


def _combine(h1, y_sorted, dest, top_w, g2, final_norm_w, L):
    T, D = h1.shape
    tm = COMBINE_TM
    tiles = T // tm
    tpb = L // tm
    dest_tiles = dest.reshape(tiles, tm, TOP_K).transpose(0, 2, 1).reshape(tiles, 1, TOP_K * tm)
    return pl.pallas_call(
        _combine_body,
        out_shape=jax.ShapeDtypeStruct((T, D), F32),
        grid=(tiles,),
        in_specs=[pl.BlockSpec((1, 1, TOP_K * tm), lambda i: (i, 0, 0), memory_space=pltpu.SMEM),
                  pl.BlockSpec((tm, D), lambda i: (i, 0)),
                  pl.BlockSpec((tm, LANES), lambda i: (i, 0)),
                  pl.BlockSpec((1, 1, D), lambda i: (i // tpb, 0, 0)),
                  pl.BlockSpec((1, D), lambda i: (0, 0)),
                  pl.BlockSpec(memory_space=pl.ANY)],
        out_specs=pl.BlockSpec((tm, D), lambda i: (i, 0)),
        scratch_shapes=[pltpu.VMEM((TOP_K * tm, D // 2), U32), pltpu.SemaphoreType.DMA(())],
        compiler_params=_cparams(("arbitrary",)),
        name="combine",
    )(dest_tiles, h1, top_w, g2, final_norm_w.reshape(1, D), y_sorted)


def kernel(x, c, w_ada, b_ada, norm1_w, w_in, ssd_conv_w, ssd_conv_b, ssd_dt_bias, ssd_a_log, ssd_d_skip, ssd_norm_w, gla_w_gk2, gla_b_gk2, gla_norm_w, w_ssd_out, w_gla_out, w_out, norm2_w, w_router, b_router, w_gate_up, b_gate_up, w_down, b_down, final_norm_w):
    B, L, D = x.shape
    T = B * L
    assert D == D_MODEL and w_ada.shape[0] == 1
    x2d = x.reshape(T, D)

    mod = _ada(c, w_ada[0], b_ada[0])
    sh1, sc1, g1, sh2, sc2, g2 = [mod[:, i * D:(i + 1) * D].reshape(B, 1, D) for i in range(6)]

    w_t = w_in[0].T
    pad = lambda a: jnp.pad(a, ((0, LANES - a.shape[0]), (0, 0)))
    w_s = jnp.concatenate([pad(w_t[SRC_DT[0]:SRC_DT[1]]), pad(w_t[SRC_GK[0]:SRC_GK[1]])], axis=0)
    proj_a, n1 = _inproj_a(x2d, norm1_w[0].reshape(1, D), sc1, sh1, w_t, L)
    proj_b = _matmul(n1, w_t, SRC_B, PROJ_TN, "inproj_b")
    proj_c = _matmul(n1, w_t, SRC_C, PROJ_TN, "inproj_c")
    proj_s = _matmul(n1, w_s, (0, 2 * LANES), 2 * LANES, "inproj_s")
    pa3, pb3, ps3 = [p.reshape(B, L, p.shape[1]) for p in (proj_a, proj_b, proj_s)]
    y_ssd = _ssd(pa3, ps3, ssd_conv_w[0], ssd_conv_b[0], ssd_dt_bias[0], ssd_a_log[0], ssd_d_skip[0], ssd_norm_w[0])
    y_gla = _gla(pb3, ps3, gla_w_gk2[0], gla_b_gk2[0], gla_norm_w[0])
    merged = _merge(y_ssd.reshape(T, D), y_gla.reshape(T, D), proj_c, w_ssd_out[0].astype(BF16), w_gla_out[0].astype(BF16))
    h1, n2p, top_w, top_i, cnt = _post(merged, x2d, g1, norm2_w[0].reshape(1, D), sc2, sh2, w_out[0].astype(BF16),
                                       w_router[0], b_router[0], L)

    A = T * TOP_K
    n_rows = A + N_EXPERTS * MOE_SUB
    n_sb = N_EXPERTS + n_rows // MOE_TM
    dest, sb_tok, sb_e, sb_start, sb_n = _routing(top_i[:, :TOP_K], top_i[:, TOP_K:2 * TOP_K],
                                                  cnt[0, :N_EXPERTS].astype(I32), n_sb)
    y_sorted = _moe(n2p, n_rows, sb_tok, sb_e, sb_start, sb_n, w_gate_up[0], b_gate_up[0], w_down[0], b_down[0])
    out = _combine(h1, y_sorted, dest, top_w, g2, final_norm_w, L)
    return out.reshape(B, L, D)
```

```python
import functools

import jax
import jax.numpy as jnp
import numpy as np
from jax import lax
from jax.experimental import pallas as pl
from jax.experimental.pallas import tpu as pltpu

F32 = jnp.float32
BF16 = jnp.bfloat16
U32 = jnp.uint32
I32 = jnp.int32

EPS = 1e-6
LANES = 128
VMEM_LIMIT = 56 * 1024 * 1024

D_MODEL = 2048
SSD_HEADS, SSD_P, SSD_G, SSD_N, SSD_CONV, SSD_Q = 32, 64, 4, 128, 4, 256
SSD_INNER = SSD_HEADS * SSD_P
SSD_TAIL = 16
GLA_H, GLA_DK, GLA_DV, GLA_RANK, GLA_C = 4, 256, 512, 16, 64
GLA_GATE_NORM = 16.0
N_EXPERTS, TOP_K = 32, 4
SWIGLU_LIMIT, SWIGLU_ALPHA = 7.0, 1.702

SRC_A = (0, 5120)
SRC_DT = (5120, 5152)
SRC_B = (5152, 11296)
SRC_GK = (11296, 11312)
SRC_C = (11312, 15408)
A_Z, A_XS, A_BC = 0, 2048, 4096
B_Q, B_K, B_V, B_GO = 0, 1024, 2048, 4096
C_GS, C_GG = 0, 2048
PROJ_TN = 1024

MOE_SUB = 256
MOE_TM = 2048
MOE_TH = 256


def _cparams(sem, vmem=VMEM_LIMIT):
    return pltpu.CompilerParams(dimension_semantics=sem, vmem_limit_bytes=vmem)


def _split3(x):
    hi = x.astype(BF16)
    r1 = x - hi.astype(F32)
    mid = r1.astype(BF16)
    lo = (r1 - mid.astype(F32)).astype(BF16)
    return hi, mid, lo


def _dot(a, b):
    return jnp.dot(a, b, preferred_element_type=F32)


def _dot_nt(a, b):
    return lax.dot_general(a, b, (((1,), (1,)), ((), ())), preferred_element_type=F32)


def _dot_tn(a, b):
    return lax.dot_general(a, b, (((0,), (0,)), ((), ())), preferred_element_type=F32)


def _dot_exact_lhs(a01, x):
    hi, mid, lo = _split3(x)
    return _dot(a01, hi) + _dot(a01, mid) + _dot(a01, lo)


def _dot_hilo_rhs(x, b01):
    hi = x.astype(BF16)
    lo = (x - hi.astype(F32)).astype(BF16)
    return _dot(hi, b01) + _dot(lo, b01)


def _silu(x):
    return x * jax.nn.sigmoid(x)


def _softplus(x):
    return jnp.maximum(x, 0.0) + jnp.log1p(jnp.exp(-jnp.abs(x)))


def _ada_body(ct_ref, w_ref, b_ref, o_ref):
    ct = ct_ref[...]
    cs = _silu(ct)
    w = w_ref[...]
    for m in range(ct.shape[1]):
        o_ref[m:m + 1, :] = jnp.sum(w * cs[:, m:m + 1], axis=0, keepdims=True) + b_ref[...]


def _ada(c, w_ada, b_ada):
    B, D = c.shape
    N = w_ada.shape[1]
    tn = 1024
    return pl.pallas_call(
        _ada_body,
        out_shape=jax.ShapeDtypeStruct((B, N), F32),
        grid=(N // tn,),
        in_specs=[pl.BlockSpec((D, B), lambda j: (0, 0)),
                  pl.BlockSpec((D, tn), lambda j: (0, j)),
                  pl.BlockSpec((1, tn), lambda j: (0, j))],
        out_specs=pl.BlockSpec((B, tn), lambda j: (0, j)),
        compiler_params=_cparams(("arbitrary",)),
        name="ada",
    )(c.T, w_ada, b_ada.reshape(1, N))


def _inproj_a_body(x_ref, nw_ref, sc_ref, sh_ref, w_ref, o_ref, n_ref):
    @pl.when(pl.program_id(1) == 0)
    def _():
        x = x_ref[...]
        ms = jnp.mean(x * x, axis=-1, keepdims=True)
        y = x * lax.rsqrt(ms + EPS) * nw_ref[...]
        n_ref[...] = (y * (1.0 + sc_ref[0]) + sh_ref[0]).astype(BF16)

    o_ref[...] = _dot_nt(n_ref[...], w_ref[...].astype(BF16)).astype(BF16)


def _inproj_a(x2d, norm_w, sc, sh, w_t, L):
    T, D = x2d.shape
    tm, tn = 1024, PROJ_TN // 2
    n_cols = SRC_A[1] - SRC_A[0]
    tiles_per_batch = L // tm
    return pl.pallas_call(
        _inproj_a_body,
        out_shape=(jax.ShapeDtypeStruct((T, n_cols), BF16), jax.ShapeDtypeStruct((T, D), BF16)),
        grid=(T // tm, n_cols // tn),
        in_specs=[pl.BlockSpec((tm, D), lambda i, j: (i, 0)),
                  pl.BlockSpec((1, D), lambda i, j: (0, 0)),
                  pl.BlockSpec((1, 1, D), lambda i, j: (i // tiles_per_batch, 0, 0)),
                  pl.BlockSpec((1, 1, D), lambda i, j: (i // tiles_per_batch, 0, 0)),
                  pl.BlockSpec((tn, D), lambda i, j: (SRC_A[0] // tn + j, 0))],
        out_specs=(pl.BlockSpec((tm, tn), lambda i, j: (i, j)), pl.BlockSpec((tm, D), lambda i, j: (i, 0))),
        compiler_params=_cparams(("arbitrary", "arbitrary")),
        name="inproj_a",
    )(x2d, norm_w, sc, sh, w_t)


def _matmul_body(a_ref, w_ref, o_ref):
    o_ref[...] = _dot_nt(a_ref[...], w_ref[...].astype(BF16)).astype(BF16)


def _matmul(a, w_t, rows, tn, name):
    T, D = a.shape
    N = rows[1] - rows[0]
    tm = 1024
    return pl.pallas_call(
        _matmul_body,
        out_shape=jax.ShapeDtypeStruct((T, N), BF16),
        grid=(T // tm, N // tn),
        in_specs=[pl.BlockSpec((tm, D), lambda i, j: (i, 0)),
                  pl.BlockSpec((pl.Element(tn), pl.Element(D)),
                               lambda i, j: ((rows[0] // 8 + j * (tn // 8)) * 8, 0))],
        out_specs=pl.BlockSpec((tm, tn), lambda i, j: (i, j)),
        compiler_params=_cparams(("arbitrary", "arbitrary")),
        name=name,
    )(a, w_t)


def _ssd_body(z_ref, xs_ref, bc_ref, dt_ref, cwx_ref, cwbc_ref, cbx_ref, cbbc_ref, dtb_ref, alog_ref,
              dskip_ref, nw_ref, exp_ref, sh_ref, o_ref, ubx, ubbc, state):
    Q = SSD_Q
    c = pl.program_id(1)

    @pl.when(c == 0)
    def _():
        ubx[...] = jnp.zeros(ubx.shape, F32)
        ubbc[...] = jnp.zeros(ubbc.shape, F32)
        state[...] = jnp.zeros(state.shape, F32)

    def conv_silu(u_ref, tail, cw_ref, cb_ref):
        u = u_ref[0]
        head = u_ref[0, 0:SSD_TAIL, :].astype(F32)[0:8, :]
        prev = tail[...]
        r8 = lax.broadcasted_iota(I32, head.shape, 0)
        w_last = cw_ref[SSD_CONV - 1:SSD_CONV, :]
        acc = cb_ref[...] + w_last * u.astype(F32)
        acc8 = cb_ref[...] + w_last * head
        for s in range(1, SSD_CONV):
            w_s = cw_ref[SSD_CONV - 1 - s:SSD_CONV - s, :]
            acc = acc + w_s * _dot(sh_ref[s - 1], u)
            acc8 = acc8 + w_s * jnp.where(r8 < s, pltpu.roll(prev, s, 0), pltpu.roll(head, s, 0))
        tail[...] = u_ref[0, Q - SSD_TAIL:Q, :].astype(F32)[SSD_TAIL - 8:, :]
        return _silu(jnp.concatenate([acc8, acc[8:, :]], axis=0))

    xs = conv_silu(xs_ref, ubx, cwx_ref, cbx_ref)
    bcm = conv_silu(bc_ref, ubbc, cwbc_ref, cbbc_ref)
    GN = SSD_G * SSD_N
    bm_f, cm = bcm[:, :GN], bcm[:, GN:].astype(BF16)

    dt = _softplus(dt_ref[0].astype(F32) + dtb_ref[...])
    a_neg = -jnp.exp(alog_ref[...])
    dA = dt * a_neg

    row = lax.broadcasted_iota(I32, (Q, Q), 0)
    col = lax.broadcasted_iota(I32, (Q, Q), 1)
    causal = row >= col
    tri = jnp.where(causal, 1.0, 0.0).astype(BF16)
    acs = _dot_exact_lhs(tri, dA)
    acs_t = acs.T
    acs_last = acs[Q - 1:Q, :]
    exp_a = jnp.exp(acs)
    decay_st = jnp.exp(acs_last - acs)

    expand = exp_ref[...]
    dt_e = _dot_hilo_rhs(dt, expand)
    dtd_e = _dot_hilo_rhs(dt * decay_st, expand)
    expa_e = _dot_hilo_rhs(exp_a, expand)

    x_dt = (xs * dt_e).astype(BF16)
    x_dd = (xs * dtd_e).astype(BF16)

    lane = lax.broadcasted_iota(I32, (Q, LANES), 1)
    lo_half = lane < SSD_P
    HG = SSD_HEADS // SSD_G
    GW = HG * SSD_P
    y_groups = []
    for g in range(SSD_G):
        cg = cm[:, g * SSD_N:(g + 1) * SSD_N]
        bg_f = bm_f[:, g * SSD_N:(g + 1) * SSD_N]
        bg = bg_f.astype(BF16)
        scores = _dot_nt(cg, bg)
        pieces = []
        for p in range(HG // 2):
            h0 = g * HG + 2 * p
            xp = x_dt[:, h0 * SSD_P:h0 * SSD_P + LANES]
            acc = None
            for s in range(2):
                h = h0 + s
                seg = acs[:, h:h + 1] - acs_t[h:h + 1, :]
                lmat = jnp.exp(jnp.where(causal, seg, -jnp.inf))
                m = (scores * lmat).astype(BF16)
                xh = jnp.where(lo_half if s == 0 else jnp.logical_not(lo_half), xp, jnp.zeros_like(xp))
                part = _dot(m, xh)
                acc = part if acc is None else acc + part
            pieces.append(acc)
        y_diag = jnp.concatenate(pieces, axis=1)
        st_old = state[g]
        y_off = _dot(cg, st_old.astype(BF16)) * expa_e[:, g * GW:(g + 1) * GW]
        st_new = _dot(bg_f.T.astype(BF16), x_dd[:, g * GW:(g + 1) * GW])
        state[g] = st_old * expa_e[Q - 1:Q, g * GW:(g + 1) * GW] + st_new
        y_groups.append(y_diag + y_off)

    y = jnp.concatenate(y_groups, axis=1) + xs * dskip_ref[...]
    y = y * _silu(z_ref[0].astype(F32))
    outs = []
    for g in range(SSD_G):
        yg = y[:, g * GW:(g + 1) * GW]
        ms = jnp.mean(yg * yg, axis=-1, keepdims=True)
        outs.append(yg * lax.rsqrt(ms + EPS) * nw_ref[:, g * GW:(g + 1) * GW])
    o_ref[0] = jnp.concatenate(outs, axis=1).astype(BF16)


def _ssd(pa3, ps3, conv_w, conv_b, dt_bias, a_log, d_skip, norm_w):
    B, L, _ = pa3.shape
    Q, DI, BCW = SSD_Q, SSD_INNER, 2 * SSD_G * SSD_N
    padl = lambda a: jnp.pad(a.reshape(1, -1), ((0, 0), (0, LANES - a.shape[-1])))
    expand = (np.arange(LANES)[:, None] == (np.arange(DI)[None, :] // SSD_P)).astype(np.float32)
    t = np.arange(Q)[:, None]
    shifts = np.stack([(t - s == np.arange(Q)[None, :]) for s in range(1, SSD_CONV)]).astype(np.float32)
    const = lambda shape: pl.BlockSpec(shape, lambda b, c: (0,) * len(shape))
    return pl.pallas_call(
        _ssd_body,
        out_shape=jax.ShapeDtypeStruct((B, L, DI), BF16),
        grid=(B, L // Q),
        in_specs=[pl.BlockSpec((1, Q, DI), lambda b, c: (b, c, A_Z // DI)),
                  pl.BlockSpec((1, Q, DI), lambda b, c: (b, c, A_XS // DI)),
                  pl.BlockSpec((1, Q, BCW), lambda b, c: (b, c, A_BC // BCW)),
                  pl.BlockSpec((1, Q, LANES), lambda b, c: (b, c, 0)),
                  const((SSD_CONV, DI)), const((SSD_CONV, BCW)), const((1, DI)), const((1, BCW)),
                  const((1, LANES)), const((1, LANES)), const((1, DI)), const((1, DI)),
                  const((LANES, DI)), const((SSD_CONV - 1, Q, Q))],
        out_specs=pl.BlockSpec((1, Q, DI), lambda b, c: (b, c, 0)),
        scratch_shapes=[pltpu.VMEM((8, DI), F32), pltpu.VMEM((8, BCW), F32),
                        pltpu.VMEM((SSD_G, SSD_N, DI // SSD_G), F32)],
        compiler_params=_cparams(("arbitrary", "arbitrary")),
        name="ssd",
    )(pa3, pa3, pa3, ps3, conv_w[:, :DI], conv_w[:, DI:], conv_b[:DI].reshape(1, DI),
      conv_b[DI:].reshape(1, BCW), padl(dt_bias), padl(a_log), jnp.repeat(d_skip, SSD_P).reshape(1, DI),
      norm_w.reshape(1, DI), jnp.asarray(expand, BF16), jnp.asarray(shifts, BF16))


GLA_RB = 256


def _gla_body(q_ref, k_ref, v_ref, g_ref, gk_ref, w_ref, b_ref, nw_ref, o_ref, st):
    RB, C = GLA_RB, GLA_C

    NB = q_ref.shape[0]

    @pl.when(pl.program_id(0) == 0)
    def _():
        st[...] = jnp.zeros(st.shape, F32)

    w_hi, w_mid, w_lo = _split3(w_ref[...])
    row = lax.broadcasted_iota(I32, (RB, RB), 0)
    col = lax.broadcasted_iota(I32, (RB, RB), 1)
    blocktri = jnp.where((row // C == col // C) & (row >= col), 1.0, 0.0).astype(BF16)
    bcs_all = []
    for b in range(NB):
        gk = gk_ref[b]
        pre = _dot(gk, w_hi) + _dot(gk, w_mid) + _dot(gk, w_lo) + b_ref[...]
        log_a = (jnp.minimum(pre, 0.0) - jnp.log1p(jnp.exp(-jnp.abs(pre)))) * (1.0 / GLA_GATE_NORM)
        bcs_all.append(_dot_exact_lhs(blocktri, log_a))

    r64 = lax.broadcasted_iota(I32, (C, C), 0)
    c64 = lax.broadcasted_iota(I32, (C, C), 1)
    causal = r64 >= c64
    scale = GLA_DK ** -0.5

    for s in range(RB // C):
        rs = slice(s * C, (s + 1) * C)
        q_dec, k_inv, k_st, cdec = [], [], [], []
        for b in range(NB):
            bcs = bcs_all[b][rs, :]
            last = bcs[C - 1:C, :]
            qf = q_ref[b, rs, :].astype(F32)
            kf = k_ref[b, rs, :].astype(F32)
            q_dec.append((qf * scale * jnp.exp(bcs)).astype(BF16))
            k_inv.append((kf * jnp.exp(-bcs)).astype(BF16))
            k_st.append((kf * jnp.exp(last - bcs)).astype(BF16))
            cdec.append(jnp.exp(last))
        for h in range(GLA_H):
            ks = slice(h * GLA_DK, (h + 1) * GLA_DK)
            vs = slice(h * GLA_DV, (h + 1) * GLA_DV)
            for b in range(NB):
                vh = v_ref[b, rs, vs]
                attn = jnp.where(causal, _dot_nt(q_dec[b][:, ks], k_inv[b][:, ks]), 0.0)
                st_h = st[b, h]
                o = _dot(attn.astype(BF16), vh) + _dot_nt(q_dec[b][:, ks], st_h.astype(BF16))
                st[b, h] = st_h * cdec[b][:, ks] + _dot_tn(vh, k_st[b][:, ks])
                ms = jnp.mean(o * o, axis=-1, keepdims=True)
                o = o * lax.rsqrt(ms + EPS) * nw_ref[...]
                o_ref[b, rs, vs] = (o * _silu(g_ref[b, rs, vs].astype(F32))).astype(BF16)


def _gla(pb3, ps3, w_gk2, b_gk2, norm_w):
    B, L, _ = pb3.shape
    RB, KD, VD = GLA_RB, GLA_H * GLA_DK, GLA_H * GLA_DV
    w_pad = jnp.pad(w_gk2, ((0, LANES - w_gk2.shape[0]), (0, 0)))
    const = lambda shape: pl.BlockSpec(shape, lambda c: (0,) * len(shape))
    return pl.pallas_call(
        _gla_body,
        out_shape=jax.ShapeDtypeStruct((B, L, VD), BF16),
        grid=(L // RB,),
        in_specs=[pl.BlockSpec((B, RB, KD), lambda c: (0, c, B_Q // KD)),
                  pl.BlockSpec((B, RB, KD), lambda c: (0, c, B_K // KD)),
                  pl.BlockSpec((B, RB, VD), lambda c: (0, c, B_V // VD)),
                  pl.BlockSpec((B, RB, VD), lambda c: (0, c, B_GO // VD)),
                  pl.BlockSpec((B, RB, LANES), lambda c: (0, c, 1)),
                  const((LANES, KD)), const((1, KD)), const((1, GLA_DV))],
        out_specs=pl.BlockSpec((B, RB, VD), lambda c: (0, c, 0)),
        scratch_shapes=[pltpu.VMEM((B, GLA_H, GLA_DV, GLA_DK), F32)],
        compiler_params=_cparams(("arbitrary",)),
        name="gla",
    )(pb3, pb3, pb3, pb3, ps3, w_pad, b_gk2.reshape(1, KD), norm_w.reshape(1, GLA_DV))


def _merge_body(ys_ref, yg_ref, gs_ref, gg_ref, ws_ref, wg_ref, o_ref):
    a = _dot(ys_ref[...], ws_ref[...])
    b = _dot(yg_ref[...], wg_ref[...])
    m = jax.nn.sigmoid(gs_ref[...].astype(F32)) * a + jax.nn.sigmoid(gg_ref[...].astype(F32)) * b
    o_ref[...] = m.astype(BF16)


def _merge(y_ssd, y_gla, proj_c, w_ssd_out, w_gla_out):
    T, D = y_ssd.shape
    tm, tn = 512, 1024
    return pl.pallas_call(
        _merge_body,
        out_shape=jax.ShapeDtypeStruct((T, D), BF16),
        grid=(D // tn, T // tm),
        in_specs=[pl.BlockSpec((tm, D), lambda j, i: (i, 0)),
                  pl.BlockSpec((tm, D), lambda j, i: (i, 0)),
                  pl.BlockSpec((tm, tn), lambda j, i: (i, C_GS // tn + j)),
                  pl.BlockSpec((tm, tn), lambda j, i: (i, C_GG // tn + j)),
                  pl.BlockSpec((D, tn), lambda j, i: (0, j)),
                  pl.BlockSpec((D, tn), lambda j, i: (0, j))],
        out_specs=pl.BlockSpec((tm, tn), lambda j, i: (i, j)),
        compiler_params=_cparams(("arbitrary", "arbitrary")),
        name="merge",
    )(y_ssd, y_gla, proj_c, proj_c, w_ssd_out, w_gla_out)


def _pack_bf16_pair(a, b):
    ua = lax.bitcast_convert_type(a.astype(BF16).astype(F32), U32)
    ub = lax.bitcast_convert_type(b.astype(BF16).astype(F32), U32)
    return (ua & jnp.uint32(0xFFFF0000)) | (ub >> 16)


def _unpack_bf16_pair(w):
    a = lax.bitcast_convert_type(w & jnp.uint32(0xFFFF0000), F32)
    b = lax.bitcast_convert_type(w << 16, F32)
    return a, b


def _post_body(m_ref, x_ref, g1_ref, nw_ref, sc_ref, sh_ref, wo_ref, wr_ref, br_ref, tri_ref,
               h_ref, n2p_ref, tw_ref, ti_ref, cnt_ref):
    D = x_ref.shape[1]
    h = x_ref[...] + g1_ref[0] * _dot(m_ref[...], wo_ref[...])
    h_ref[...] = h
    ms = jnp.mean(h * h, axis=-1, keepdims=True)
    n2 = h * lax.rsqrt(ms + EPS) * nw_ref[...] * (1.0 + sc_ref[0]) + sh_ref[0]
    n2p_ref[...] = _pack_bf16_pair(n2[:, :D // 2], n2[:, D // 2:])

    n_hi = n2.astype(BF16)
    n_lo = (n2 - n_hi.astype(F32)).astype(BF16)
    wr = wr_ref[...]
    w_hi = wr.astype(BF16)
    w_lo = (wr - w_hi.astype(F32)).astype(BF16)
    logits = _dot(n_hi, w_hi) + _dot(n_hi, w_lo) + _dot(n_lo, w_hi) + br_ref[...]

    lane = lax.broadcasted_iota(I32, logits.shape, 1)
    cur = jnp.where(lane < N_EXPERTS, logits, -jnp.inf)
    vals, idxs = [], []
    for _ in range(TOP_K):
        mx = jnp.max(cur, axis=-1, keepdims=True)
        ix = jnp.min(jnp.where(cur == mx, lane, LANES), axis=-1, keepdims=True)
        vals.append(mx)
        idxs.append(ix)
        cur = jnp.where(lane == ix, -jnp.inf, cur)
    es = [jnp.exp(v - vals[0]) for v in vals]
    denom = es[0] + es[1] + es[2] + es[3]

    @pl.when(pl.program_id(0) == 0)
    def _():
        cnt_ref[...] = jnp.zeros(cnt_ref.shape, F32)

    hits = [lane == ix for ix in idxs]
    onehot = jnp.where(hits[0] | hits[1] | hits[2] | hits[3], 1.0, 0.0)
    before = _dot(tri_ref[...], onehot.astype(BF16)) + cnt_ref[0:1, :]
    cnt_ref[...] = cnt_ref[...] + jnp.sum(onehot, axis=0, keepdims=True)

    tw = jnp.zeros(logits.shape, F32)
    ti = jnp.zeros(logits.shape, I32)
    for kk in range(TOP_K):
        rank = jnp.sum(jnp.where(hits[kk], before, 0.0), axis=-1, keepdims=True).astype(I32)
        tw = jnp.where(lane == kk, es[kk] / denom, tw)
        ti = jnp.where(lane == kk, idxs[kk], jnp.where(lane == TOP_K + kk, rank, ti))
    tw_ref[...] = tw
    ti_ref[...] = ti


def _post(merged, x2d, g1, norm_w, sc, sh, w_out, w_router, b_router, L):
    T, D = x2d.shape
    tm = 512
    tpb = L // tm
    wr = jnp.pad(w_router, ((0, 0), (0, LANES - w_router.shape[1])))
    br = jnp.pad(b_router.reshape(1, -1), ((0, 0), (0, LANES - b_router.shape[0])))
    row = lambda w: pl.BlockSpec((tm, w), lambda i: (i, 0))
    per_b = pl.BlockSpec((1, 1, D), lambda i: (i // tpb, 0, 0))
    const = lambda shape: pl.BlockSpec(shape, lambda i: (0,) * len(shape))
    tri = np.tril(np.ones((tm, tm), np.float32), -1)
    return pl.pallas_call(
        _post_body,
        out_shape=(jax.ShapeDtypeStruct((T, D), F32), jax.ShapeDtypeStruct((T, D // 2), U32),
                   jax.ShapeDtypeStruct((T, LANES), F32), jax.ShapeDtypeStruct((T, LANES), I32),
                   jax.ShapeDtypeStruct((8, LANES), F32)),
        grid=(T // tm,),
        in_specs=[row(D), row(D), per_b, const((1, D)), per_b, per_b, const((D, D)), const((D, LANES)),
                  const((1, LANES)), const((tm, tm))],
        out_specs=(row(D), row(D // 2), row(LANES), row(LANES), const((8, LANES))),
        compiler_params=_cparams(("arbitrary",)),
        name="post",
    )(merged, x2d, g1, norm_w, sc, sh, w_out, wr, br, jnp.asarray(tri, BF16))


def _routing(top_idx, top_rank, counts, n_sb):
    T = top_idx.shape[0]
    A = T * TOP_K
    flat_e = top_idx.reshape(A)
    rank = top_rank.reshape(A)
    padded = (counts + MOE_SUB - 1) // MOE_SUB * MOE_SUB
    pend = jnp.cumsum(padded)
    pstart = pend - padded
    dest = (pstart[flat_e] + rank).astype(I32)

    nsb = (padded + MOE_TM - 1) // MOE_TM
    sb_end = jnp.cumsum(nsb)
    sb_first = sb_end - nsb
    slot = (sb_first[flat_e] + rank // MOE_TM) * MOE_TM + rank % MOE_TM
    sb_tok = jnp.zeros((n_sb * MOE_TM,), I32).at[slot].set(jnp.arange(A, dtype=I32) // TOP_K)

    total = sb_end[-1]
    i = jnp.arange(n_sb, dtype=I32)
    e_of = jnp.minimum(jnp.searchsorted(sb_end, i, side="right"), N_EXPERTS - 1).astype(I32)
    local = i - sb_first[e_of]
    valid = i < total
    last_e = e_of[jnp.maximum(total - 1, 0)]
    sb_e = jnp.where(valid, e_of, last_e).astype(I32)
    sb_start = jnp.where(valid, pstart[e_of] + local * MOE_TM, 0).astype(I32)
    sb_start = jnp.concatenate([sb_start, pend[-1:].astype(I32)])
    sb_n = jnp.where(valid, jnp.clip(padded[e_of] - local * MOE_TM, 0, MOE_TM), 0).astype(I32)
    return dest, sb_tok.reshape(n_sb, 1, MOE_TM), sb_e, sb_start, sb_n


def _deinterleave(gu):
    rows, two_w = gu.shape
    lane = lax.broadcasted_iota(I32, (rows, LANES), 1)
    idx_e = (2 * lane) % LANES
    idx_o = idx_e + 1
    first = lane < LANES // 2
    gates, ups = [], []
    for p in range(two_w // (2 * LANES)):
        a = gu[:, (2 * p) * LANES:(2 * p + 1) * LANES]
        b = gu[:, (2 * p + 1) * LANES:(2 * p + 2) * LANES]
        gates.append(jnp.where(first, jnp.take_along_axis(a, idx_e, axis=1), jnp.take_along_axis(b, idx_e, axis=1)))
        ups.append(jnp.where(first, jnp.take_along_axis(a, idx_o, axis=1), jnp.take_along_axis(b, idx_o, axis=1)))
    return jnp.concatenate(gates, axis=1), jnp.concatenate(ups, axis=1)


MOE_DMA_UNROLL = 16


def _moe_body(sbe_ref, sbs_ref, sbn_ref, tok_ref, tokn_ref, x_hbm, wgu_ref, bgu_ref, wd_ref, bd_ref,
              y_hbm, xs_buf, xb_buf, acc, ystage, pend, sem_in, sem_out):
    i = pl.program_id(0)
    hc = pl.program_id(1)
    n_sb = pl.num_programs(0)
    n_hc = pl.num_programs(1)
    SUB = MOE_SUB
    half = xs_buf.shape[1]
    nblk = sbn_ref[i] // SUB
    start_row = sbs_ref[i]

    def row_copy(t_ref, g, u):
        base = pl.multiple_of(g * MOE_DMA_UNROLL, MOE_DMA_UNROLL)
        return pltpu.make_async_copy(x_hbm.at[pl.ds(t_ref[0, 0, base + u], 1)], xs_buf.at[pl.ds(base + u, 1)], sem_in)

    def start_in(t_ref, n_rows):
        def body(g, carry):
            for u in range(MOE_DMA_UNROLL):
                row_copy(t_ref, g, u).start(priority=u % 2)
            return carry

        lax.fori_loop(0, n_rows // MOE_DMA_UNROLL, body, 0)

    @pl.when(hc == 0)
    def _():
        @pl.when(i == 0)
        def _():
            pend[0] = 0
            pend[1] = 0
            start_in(tok_ref, sbn_ref[0])

        def wait_in(g, carry):
            for u in range(MOE_DMA_UNROLL):
                row_copy(tok_ref, g, u).wait()
            return carry

        lax.fori_loop(0, sbn_ref[i] // MOE_DMA_UNROLL, wait_in, 0)

        def convert(r, carry):
            rows = pl.ds(pl.multiple_of(r * SUB, SUB), SUB)
            a, b = _unpack_bf16_pair(xs_buf[rows, :])
            xb_buf[rows, 0:half] = a.astype(BF16)
            xb_buf[rows, half:2 * half] = b.astype(BF16)
            return carry

        lax.fori_loop(0, nblk, convert, 0)

        @pl.when(i + 1 < n_sb)
        def _():
            start_in(tokn_ref, sbn_ref[i + 1])

    def accumulate(blocks, first):
        rows = [pl.ds(pl.multiple_of(r * SUB, SUB), SUB) for r in blocks]
        wg = wgu_ref[0].astype(BF16)
        wd = wd_ref[0].astype(BF16)
        gus = [_dot(xb_buf[rw, :], wg) + bgu_ref[0] for rw in rows]
        hs = []
        for gu in gus:
            gate, up = _deinterleave(gu)
            gate = jnp.minimum(gate, SWIGLU_LIMIT)
            up = jnp.clip(up, -SWIGLU_LIMIT, SWIGLU_LIMIT)
            hs.append((gate * jax.nn.sigmoid(gate * SWIGLU_ALPHA) * (up + 1.0)).astype(BF16))
        for rw, hidden in zip(rows, hs):
            v = _dot(hidden, wd)
            if first:
                acc[rw, :] = v + bd_ref[0]
            else:
                acc[rw, :] += v

    def run(first):
        def quad(p, carry):
            accumulate([4 * p, 4 * p + 1, 4 * p + 2, 4 * p + 3], first)
            return carry

        lax.fori_loop(0, nblk // 4, quad, 0)
        done = nblk // 4 * 4

        @pl.when(nblk - done >= 2)
        def _():
            accumulate([done, done + 1], first)

        @pl.when(nblk % 2 == 1)
        def _():
            accumulate([nblk - 1], first)

    @pl.when(nblk > 0)
    def _():
        @pl.when(hc == 0)
        def _():
            run(True)

        @pl.when(hc != 0)
        def _():
            run(False)

    @pl.when(hc == n_hc - 1)
    def _():
        def out_copy(r, slot):
            dst = pl.multiple_of(start_row + r * SUB, SUB)
            return pltpu.make_async_copy(ystage.at[slot], y_hbm.at[pl.ds(dst, SUB)], sem_out.at[slot])

        def body(r, carry):
            slot = r % 2

            @pl.when(pend[slot] == 1)
            def _():
                out_copy(r, slot).wait()

            rows = pl.ds(pl.multiple_of(r * SUB, SUB), SUB)
            v = acc[rows, :]
            ystage[slot] = _pack_bf16_pair(v[:, :half], v[:, half:])
            out_copy(r, slot).start()
            pend[slot] = 1
            return carry

        lax.fori_loop(0, nblk, body, 0)

    @pl.when((i == n_sb - 1) & (hc == n_hc - 1))
    def _():
        for slot in range(2):
            @pl.when(pend[slot] == 1)
            def _():
                pltpu.make_async_copy(ystage.at[slot], y_hbm.at[pl.ds(0, SUB)], sem_out.at[slot]).wait()
                pend[slot] = 0

        ystage[0] = jnp.zeros(ystage.shape[1:], U32)

        def tail_copy(b):
            return pltpu.make_async_copy(ystage.at[0], y_hbm.at[pl.ds(pl.multiple_of(b * SUB, SUB), SUB)], sem_out.at[0])

        def start(b, carry):
            tail_copy(b).start()
            return carry

        def wait(b, carry):
            tail_copy(b).wait()
            return carry

        first, stop = sbs_ref[n_sb] // SUB, y_hbm.shape[0] // SUB
        lax.fori_loop(first, stop, start, 0)
        lax.fori_loop(first, stop, wait, 0)


def _moe(n2p, n_rows, sb_tok, sb_e, sb_start, sb_n, w_gate_up, b_gate_up, w_down, b_down):
    half = n2p.shape[1]
    E, D, H2 = w_gate_up.shape
    n_sb = sb_e.shape[0]
    TH = MOE_TH
    n_hc = (H2 // 2) // TH
    hc_eff = lambda i, hc, sbn: jnp.where(sbn[i] > 0, hc, n_hc - 1)
    grid_spec = pltpu.PrefetchScalarGridSpec(
        num_scalar_prefetch=3,
        grid=(n_sb, n_hc),
        in_specs=[pl.BlockSpec((1, 1, MOE_TM), lambda i, hc, sbe, sbs, sbn: (i, 0, 0), memory_space=pltpu.SMEM),
                  pl.BlockSpec((1, 1, MOE_TM), lambda i, hc, sbe, sbs, sbn: (jnp.minimum(i + 1, n_sb - 1), 0, 0),
                               memory_space=pltpu.SMEM),
                  pl.BlockSpec(memory_space=pl.ANY),
                  pl.BlockSpec((1, D, 2 * TH), lambda i, hc, sbe, sbs, sbn: (sbe[i], 0, hc_eff(i, hc, sbn))),
                  pl.BlockSpec((1, 1, 2 * TH), lambda i, hc, sbe, sbs, sbn: (sbe[i], 0, hc_eff(i, hc, sbn))),
                  pl.BlockSpec((1, TH, D), lambda i, hc, sbe, sbs, sbn: (sbe[i], hc_eff(i, hc, sbn), 0)),
                  pl.BlockSpec((1, 1, D), lambda i, hc, sbe, sbs, sbn: (sbe[i], 0, 0))],
        out_specs=pl.BlockSpec(memory_space=pl.ANY),
        scratch_shapes=[pltpu.VMEM((MOE_TM, half), U32), pltpu.VMEM((MOE_TM, 2 * half), BF16),
                        pltpu.VMEM((MOE_TM, D), F32), pltpu.VMEM((2, MOE_SUB, half), U32),
                        pltpu.SMEM((2,), I32),
                        pltpu.SemaphoreType.DMA(()), pltpu.SemaphoreType.DMA((2,))],
    )
    return pl.pallas_call(
        _moe_body,
        out_shape=jax.ShapeDtypeStruct((n_rows, half), U32),
        grid_spec=grid_spec,
        compiler_params=_cparams(("arbitrary", "arbitrary")),
        name="moe",
    )(sb_e, sb_start, sb_n, sb_tok, sb_tok, n2p, w_gate_up, b_gate_up.reshape(E, 1, H2), w_down, b_down.reshape(E, 1, D))


COMBINE_TM = 512


def _combine_body(dest_ref, h_ref, tw_ref, g2_ref, nw_ref, y_hbm, o_ref, ybuf, sem):
    tm = COMBINE_TM
    n = TOP_K * tm

    def copy(j):
        return pltpu.make_async_copy(y_hbm.at[pl.ds(dest_ref[0, 0, j], 1)], ybuf.at[pl.ds(j, 1)], sem)

    def start(g, carry):
        for u in range(8):
            copy(g * 8 + u).start(priority=u % 2)
        return carry

    def wait(j, carry):
        copy(j).wait()
        return carry

    lax.fori_loop(0, n // 8, start, 0)
    lax.fori_loop(0, n, wait, 0, unroll=8)

    tw = tw_ref[...]
    lo = hi = None
    for kk in range(TOP_K):
        a, b = _unpack_bf16_pair(ybuf[kk * tm:(kk + 1) * tm, :])
        wk = tw[:, kk:kk + 1]
        lo = wk * a if lo is None else lo + wk * a
        hi = wk * b if hi is None else hi + wk * b
    ffn = jnp.concatenate([lo, hi], axis=1)
    h = h_ref[...] + g2_ref[0] * ffn
    ms = jnp.mean(h * h, axis=-1, keepdims=True)
    o_ref[...] = h * lax.rsqrt(ms + EPS) * nw_ref[...]


def _combine(h1, y_sorted, dest, top_w, g2, final_norm_w, L):
    T, D = h1.shape
    tm = COMBINE_TM
    tiles = T // tm
    tpb = L // tm
    dest_tiles = dest.reshape(tiles, tm, TOP_K).transpose(0, 2, 1).reshape(tiles, 1, TOP_K * tm)
    return pl.pallas_call(
        _combine_body,
        out_shape=jax.ShapeDtypeStruct((T, D), F32),
        grid=(tiles,),
        in_specs=[pl.BlockSpec((1, 1, TOP_K * tm), lambda i: (i, 0, 0), memory_space=pltpu.SMEM),
                  pl.BlockSpec((tm, D), lambda i: (i, 0)),
                  pl.BlockSpec((tm, LANES), lambda i: (i, 0)),
                  pl.BlockSpec((1, 1, D), lambda i: (i // tpb, 0, 0)),
                  pl.BlockSpec((1, D), lambda i: (0, 0)),
                  pl.BlockSpec(memory_space=pl.ANY)],
        out_specs=pl.BlockSpec((tm, D), lambda i: (i, 0)),
        scratch_shapes=[pltpu.VMEM((TOP_K * tm, D // 2), U32), pltpu.SemaphoreType.DMA(())],
        compiler_params=_cparams(("arbitrary",)),
        name="combine",
    )(dest_tiles, h1, top_w, g2, final_norm_w.reshape(1, D), y_sorted)


def kernel(x, c, w_ada, b_ada, norm1_w, w_in, ssd_conv_w, ssd_conv_b, ssd_dt_bias, ssd_a_log, ssd_d_skip, ssd_norm_w, gla_w_gk2, gla_b_gk2, gla_norm_w, w_ssd_out, w_gla_out, w_out, norm2_w, w_router, b_router, w_gate_up, b_gate_up, w_down, b_down, final_norm_w):
    B, L, D = x.shape
    T = B * L
    assert D == D_MODEL and w_ada.shape[0] == 1
    x2d = x.reshape(T, D)

    mod = _ada(c, w_ada[0], b_ada[0])
    sh1, sc1, g1, sh2, sc2, g2 = [mod[:, i * D:(i + 1) * D].reshape(B, 1, D) for i in range(6)]

    w_t = w_in[0].T
    pad = lambda a: jnp.pad(a, ((0, LANES - a.shape[0]), (0, 0)))
    w_s = jnp.concatenate([pad(w_t[SRC_DT[0]:SRC_DT[1]]), pad(w_t[SRC_GK[0]:SRC_GK[1]])], axis=0)
    proj_a, n1 = _inproj_a(x2d, norm1_w[0].reshape(1, D), sc1, sh1, w_t, L)
    proj_b = _matmul(n1, w_t, SRC_B, PROJ_TN, "inproj_b")
    proj_c = _matmul(n1, w_t, SRC_C, PROJ_TN, "inproj_c")
    proj_s = _matmul(n1, w_s, (0, 2 * LANES), 2 * LANES, "inproj_s")
    pa3, pb3, ps3 = [p.reshape(B, L, p.shape[1]) for p in (proj_a, proj_b, proj_s)]
    y_ssd = _ssd(pa3, ps3, ssd_conv_w[0], ssd_conv_b[0], ssd_dt_bias[0], ssd_a_log[0], ssd_d_skip[0], ssd_norm_w[0])
    y_gla = _gla(pb3, ps3, gla_w_gk2[0], gla_b_gk2[0], gla_norm_w[0])
    merged = _merge(y_ssd.reshape(T, D), y_gla.reshape(T, D), proj_c, w_ssd_out[0].astype(BF16), w_gla_out[0].astype(BF16))
    h1, n2p, top_w, top_i, cnt = _post(merged, x2d, g1, norm2_w[0].reshape(1, D), sc2, sh2, w_out[0].astype(BF16),
                                       w_router[0], b_router[0], L)

    A = T * TOP_K
    n_rows = A + N_EXPERTS * MOE_SUB
    n_sb = n_rows // MOE_TM + N_EXPERTS * (MOE_TM - MOE_SUB) // MOE_TM
    dest, sb_tok, sb_e, sb_start, sb_n = _routing(top_i[:, :TOP_K], top_i[:, TOP_K:2 * TOP_K],
                                                  cnt[0, :N_EXPERTS].astype(I32), n_sb)
    y_sorted = _moe(n2p, n_rows, sb_tok, sb_e, sb_start, sb_n, w_gate_up[0], b_gate_up[0], w_down[0], b_down[0])
    out = _combine(h1, y_sorted, dest, top_w, g2, final_norm_w, L)
    return out.reshape(B, L, D)
```

```python
import functools

import jax
import jax.numpy as jnp
import numpy as np
from jax import lax
from jax.experimental import pallas as pl
from jax.experimental.pallas import tpu as pltpu

F32 = jnp.float32
BF16 = jnp.bfloat16
U32 = jnp.uint32
I32 = jnp.int32

EPS = 1e-6
LANES = 128
VMEM_LIMIT = 56 * 1024 * 1024

D_MODEL = 2048
SSD_HEADS, SSD_P, SSD_G, SSD_N, SSD_CONV, SSD_Q = 32, 64, 4, 128, 4, 256
SSD_INNER = SSD_HEADS * SSD_P
SSD_TAIL = 16
GLA_H, GLA_DK, GLA_DV, GLA_RANK, GLA_C = 4, 256, 512, 16, 64
GLA_GATE_NORM = 16.0
N_EXPERTS, TOP_K = 32, 4
SWIGLU_LIMIT, SWIGLU_ALPHA = 7.0, 1.702

SRC_A = (0, 5120)
SRC_DT = (5120, 5152)
SRC_B = (5152, 11296)
SRC_GK = (11296, 11312)
SRC_C = (11312, 15408)
A_Z, A_XS, A_BC = 0, 2048, 4096
B_Q, B_K, B_V, B_GO = 0, 1024, 2048, 4096
C_GS, C_GG = 0, 2048
PROJ_TN = 1024

MOE_SUB = 256
MOE_TM = 2048
MOE_TH = 256


def _cparams(sem, vmem=VMEM_LIMIT):
    return pltpu.CompilerParams(dimension_semantics=sem, vmem_limit_bytes=vmem)


def _split3(x):
    hi = x.astype(BF16)
    r1 = x - hi.astype(F32)
    mid = r1.astype(BF16)
    lo = (r1 - mid.astype(F32)).astype(BF16)
    return hi, mid, lo


def _dot(a, b):
    return jnp.dot(a, b, preferred_element_type=F32)


def _dot_nt(a, b):
    return lax.dot_general(a, b, (((1,), (1,)), ((), ())), preferred_element_type=F32)


def _dot_tn(a, b):
    return lax.dot_general(a, b, (((0,), (0,)), ((), ())), preferred_element_type=F32)


def _dot_exact_lhs(a01, x):
    hi, mid, lo = _split3(x)
    return _dot(a01, hi) + _dot(a01, mid) + _dot(a01, lo)


def _dot_hilo_rhs(x, b01):
    hi = x.astype(BF16)
    lo = (x - hi.astype(F32)).astype(BF16)
    return _dot(hi, b01) + _dot(lo, b01)


def _silu(x):
    return x * jax.nn.sigmoid(x)


def _softplus(x):
    return jnp.maximum(x, 0.0) + jnp.log1p(jnp.exp(-jnp.abs(x)))


def _ada_body(ct_ref, w_ref, b_ref, o_ref):
    ct = ct_ref[...]
    cs = _silu(ct)
    w = w_ref[...]
    for m in range(ct.shape[1]):
        o_ref[m:m + 1, :] = jnp.sum(w * cs[:, m:m + 1], axis=0, keepdims=True) + b_ref[...]


def _ada(c, w_ada, b_ada):
    B, D = c.shape
    N = w_ada.shape[1]
    tn = 1024
    return pl.pallas_call(
        _ada_body,
        out_shape=jax.ShapeDtypeStruct((B, N), F32),
        grid=(N // tn,),
        in_specs=[pl.BlockSpec((D, B), lambda j: (0, 0)),
                  pl.BlockSpec((D, tn), lambda j: (0, j)),
                  pl.BlockSpec((1, tn), lambda j: (0, j))],
        out_specs=pl.BlockSpec((B, tn), lambda j: (0, j)),
        compiler_params=_cparams(("arbitrary",)),
        name="ada",
    )(c.T, w_ada, b_ada.reshape(1, N))


def _inproj_a_body(x_ref, nw_ref, sc_ref, sh_ref, w_ref, o_ref, n_ref):
    @pl.when(pl.program_id(1) == 0)
    def _():
        x = x_ref[...]
        ms = jnp.mean(x * x, axis=-1, keepdims=True)
        y = x * lax.rsqrt(ms + EPS) * nw_ref[...]
        n_ref[...] = (y * (1.0 + sc_ref[0]) + sh_ref[0]).astype(BF16)

    o_ref[...] = _dot_nt(n_ref[...], w_ref[...].astype(BF16)).astype(BF16)


def _inproj_a(x2d, norm_w, sc, sh, w_t, L):
    T, D = x2d.shape
    tm, tn = 1024, PROJ_TN // 2
    n_cols = SRC_A[1] - SRC_A[0]
    tiles_per_batch = L // tm
    return pl.pallas_call(
        _inproj_a_body,
        out_shape=(jax.ShapeDtypeStruct((T, n_cols), BF16), jax.ShapeDtypeStruct((T, D), BF16)),
        grid=(T // tm, n_cols // tn),
        in_specs=[pl.BlockSpec((tm, D), lambda i, j: (i, 0)),
                  pl.BlockSpec((1, D), lambda i, j: (0, 0)),
                  pl.BlockSpec((1, 1, D), lambda i, j: (i // tiles_per_batch, 0, 0)),
                  pl.BlockSpec((1, 1, D), lambda i, j: (i // tiles_per_batch, 0, 0)),
                  pl.BlockSpec((tn, D), lambda i, j: (SRC_A[0] // tn + j, 0))],
        out_specs=(pl.BlockSpec((tm, tn), lambda i, j: (i, j)), pl.BlockSpec((tm, D), lambda i, j: (i, 0))),
        compiler_params=_cparams(("arbitrary", "arbitrary")),
        name="inproj_a",
    )(x2d, norm_w, sc, sh, w_t)


def _matmul_body(a_ref, w_ref, o_ref):
    o_ref[...] = _dot_nt(a_ref[...], w_ref[...].astype(BF16)).astype(BF16)


def _matmul(a, w_t, rows, tn, name):
    T, D = a.shape
    N = rows[1] - rows[0]
    tm = 1024
    return pl.pallas_call(
        _matmul_body,
        out_shape=jax.ShapeDtypeStruct((T, N), BF16),
        grid=(T // tm, N // tn),
        in_specs=[pl.BlockSpec((tm, D), lambda i, j: (i, 0)),
                  pl.BlockSpec((pl.Element(tn), pl.Element(D)),
                               lambda i, j: ((rows[0] // 8 + j * (tn // 8)) * 8, 0))],
        out_specs=pl.BlockSpec((tm, tn), lambda i, j: (i, j)),
        compiler_params=_cparams(("arbitrary", "arbitrary")),
        name=name,
    )(a, w_t)


def _ssd_body(z_ref, xs_ref, bc_ref, dt_ref, cwx_ref, cwbc_ref, cbx_ref, cbbc_ref, dtb_ref, alog_ref,
              dskip_ref, nw_ref, exp_ref, sh_ref, o_ref, ubx, ubbc, state):
    Q = SSD_Q
    c = pl.program_id(1)

    @pl.when(c == 0)
    def _():
        ubx[...] = jnp.zeros(ubx.shape, F32)
        ubbc[...] = jnp.zeros(ubbc.shape, F32)
        state[...] = jnp.zeros(state.shape, F32)

    def conv_silu(u_ref, tail, cw_ref, cb_ref):
        u = u_ref[0]
        head = u_ref[0, 0:SSD_TAIL, :].astype(F32)[0:8, :]
        prev = tail[...]
        r8 = lax.broadcasted_iota(I32, head.shape, 0)
        w_last = cw_ref[SSD_CONV - 1:SSD_CONV, :]
        acc = cb_ref[...] + w_last * u.astype(F32)
        acc8 = cb_ref[...] + w_last * head
        for s in range(1, SSD_CONV):
            w_s = cw_ref[SSD_CONV - 1 - s:SSD_CONV - s, :]
            acc = acc + w_s * _dot(sh_ref[s - 1], u)
            acc8 = acc8 + w_s * jnp.where(r8 < s, pltpu.roll(prev, s, 0), pltpu.roll(head, s, 0))
        tail[...] = u_ref[0, Q - SSD_TAIL:Q, :].astype(F32)[SSD_TAIL - 8:, :]
        return _silu(jnp.concatenate([acc8, acc[8:, :]], axis=0))

    xs = conv_silu(xs_ref, ubx, cwx_ref, cbx_ref)
    bcm = conv_silu(bc_ref, ubbc, cwbc_ref, cbbc_ref)
    GN = SSD_G * SSD_N
    bm_f, cm = bcm[:, :GN], bcm[:, GN:].astype(BF16)

    dt = _softplus(dt_ref[0].astype(F32) + dtb_ref[...])
    a_neg = -jnp.exp(alog_ref[...])
    dA = dt * a_neg

    row = lax.broadcasted_iota(I32, (Q, Q), 0)
    col = lax.broadcasted_iota(I32, (Q, Q), 1)
    causal = row >= col
    tri = jnp.where(causal, 1.0, 0.0).astype(BF16)
    acs = _dot_exact_lhs(tri, dA)
    acs_t = acs.T
    acs_last = acs[Q - 1:Q, :]
    exp_a = jnp.exp(acs)
    decay_st = jnp.exp(acs_last - acs)

    expand = exp_ref[...]
    dt_e = _dot_hilo_rhs(dt, expand)
    dtd_e = _dot_hilo_rhs(dt * decay_st, expand)
    expa_e = _dot_hilo_rhs(exp_a, expand)

    x_dt = (xs * dt_e).astype(BF16)
    x_dd = (xs * dtd_e).astype(BF16)

    lane = lax.broadcasted_iota(I32, (Q, LANES), 1)
    lo_half = lane < SSD_P
    HG = SSD_HEADS // SSD_G
    GW = HG * SSD_P
    y_groups = []
    for g in range(SSD_G):
        cg = cm[:, g * SSD_N:(g + 1) * SSD_N]
        bg_f = bm_f[:, g * SSD_N:(g + 1) * SSD_N]
        bg = bg_f.astype(BF16)
        scores = _dot_nt(cg, bg)
        pieces = []
        for p in range(HG // 2):
            h0 = g * HG + 2 * p
            xp = x_dt[:, h0 * SSD_P:h0 * SSD_P + LANES]
            acc = None
            for s in range(2):
                h = h0 + s
                seg = acs[:, h:h + 1] - acs_t[h:h + 1, :]
                lmat = jnp.exp(jnp.where(causal, seg, -jnp.inf))
                m = (scores * lmat).astype(BF16)
                xh = jnp.where(lo_half if s == 0 else jnp.logical_not(lo_half), xp, jnp.zeros_like(xp))
                part = _dot(m, xh)
                acc = part if acc is None else acc + part
            pieces.append(acc)
        y_diag = jnp.concatenate(pieces, axis=1)
        st_old = state[g]
        y_off = _dot(cg, st_old.astype(BF16)) * expa_e[:, g * GW:(g + 1) * GW]
        st_new = _dot(bg_f.T.astype(BF16), x_dd[:, g * GW:(g + 1) * GW])
        state[g] = st_old * expa_e[Q - 1:Q, g * GW:(g + 1) * GW] + st_new
        y_groups.append(y_diag + y_off)

    y = jnp.concatenate(y_groups, axis=1) + xs * dskip_ref[...]
    y = y * _silu(z_ref[0].astype(F32))
    outs = []
    for g in range(SSD_G):
        yg = y[:, g * GW:(g + 1) * GW]
        ms = jnp.mean(yg * yg, axis=-1, keepdims=True)
        outs.append(yg * lax.rsqrt(ms + EPS) * nw_ref[:, g * GW:(g + 1) * GW])
    o_ref[0] = jnp.concatenate(outs, axis=1).astype(BF16)


def _ssd(pa3, ps3, conv_w, conv_b, dt_bias, a_log, d_skip, norm_w):
    B, L, _ = pa3.shape
    Q, DI, BCW = SSD_Q, SSD_INNER, 2 * SSD_G * SSD_N
    padl = lambda a: jnp.pad(a.reshape(1, -1), ((0, 0), (0, LANES - a.shape[-1])))
    expand = (np.arange(LANES)[:, None] == (np.arange(DI)[None, :] // SSD_P)).astype(np.float32)
    t = np.arange(Q)[:, None]
    shifts = np.stack([(t - s == np.arange(Q)[None, :]) for s in range(1, SSD_CONV)]).astype(np.float32)
    const = lambda shape: pl.BlockSpec(shape, lambda b, c: (0,) * len(shape))
    return pl.pallas_call(
        _ssd_body,
        out_shape=jax.ShapeDtypeStruct((B, L, DI), BF16),
        grid=(B, L // Q),
        in_specs=[pl.BlockSpec((1, Q, DI), lambda b, c: (b, c, A_Z // DI)),
                  pl.BlockSpec((1, Q, DI), lambda b, c: (b, c, A_XS // DI)),
                  pl.BlockSpec((1, Q, BCW), lambda b, c: (b, c, A_BC // BCW)),
                  pl.BlockSpec((1, Q, LANES), lambda b, c: (b, c, 0)),
                  const((SSD_CONV, DI)), const((SSD_CONV, BCW)), const((1, DI)), const((1, BCW)),
                  const((1, LANES)), const((1, LANES)), const((1, DI)), const((1, DI)),
                  const((LANES, DI)), const((SSD_CONV - 1, Q, Q))],
        out_specs=pl.BlockSpec((1, Q, DI), lambda b, c: (b, c, 0)),
        scratch_shapes=[pltpu.VMEM((8, DI), F32), pltpu.VMEM((8, BCW), F32),
                        pltpu.VMEM((SSD_G, SSD_N, DI // SSD_G), F32)],
        compiler_params=_cparams(("arbitrary", "arbitrary")),
        name="ssd",
    )(pa3, pa3, pa3, ps3, conv_w[:, :DI], conv_w[:, DI:], conv_b[:DI].reshape(1, DI),
      conv_b[DI:].reshape(1, BCW), padl(dt_bias), padl(a_log), jnp.repeat(d_skip, SSD_P).reshape(1, DI),
      norm_w.reshape(1, DI), jnp.asarray(expand, BF16), jnp.asarray(shifts, BF16))


GLA_RB = 256


def _gla_body(q_ref, k_ref, v_ref, g_ref, gk_ref, w_ref, b_ref, nw_ref, o_ref, st):
    RB, C = GLA_RB, GLA_C

    NB = q_ref.shape[0]

    @pl.when(pl.program_id(0) == 0)
    def _():
        st[...] = jnp.zeros(st.shape, F32)

    w_hi, w_mid, w_lo = _split3(w_ref[...])
    row = lax.broadcasted_iota(I32, (RB, RB), 0)
    col = lax.broadcasted_iota(I32, (RB, RB), 1)
    blocktri = jnp.where((row // C == col // C) & (row >= col), 1.0, 0.0).astype(BF16)
    bcs_all = []
    for b in range(NB):
        gk = gk_ref[b]
        pre = _dot(gk, w_hi) + _dot(gk, w_mid) + _dot(gk, w_lo) + b_ref[...]
        log_a = (jnp.minimum(pre, 0.0) - jnp.log1p(jnp.exp(-jnp.abs(pre)))) * (1.0 / GLA_GATE_NORM)
        bcs_all.append(_dot_exact_lhs(blocktri, log_a))

    r64 = lax.broadcasted_iota(I32, (C, C), 0)
    c64 = lax.broadcasted_iota(I32, (C, C), 1)
    causal = r64 >= c64
    scale = GLA_DK ** -0.5

    for s in range(RB // C):
        rs = slice(s * C, (s + 1) * C)
        q_dec, k_inv, k_st, cdec = [], [], [], []
        for b in range(NB):
            bcs = bcs_all[b][rs, :]
            last = bcs[C - 1:C, :]
            qf = q_ref[b, rs, :].astype(F32)
            kf = k_ref[b, rs, :].astype(F32)
            q_dec.append((qf * scale * jnp.exp(bcs)).astype(BF16))
            k_inv.append((kf * jnp.exp(-bcs)).astype(BF16))
            k_st.append((kf * jnp.exp(last - bcs)).astype(BF16))
            cdec.append(jnp.exp(last))
        for h in range(GLA_H):
            ks = slice(h * GLA_DK, (h + 1) * GLA_DK)
            vs = slice(h * GLA_DV, (h + 1) * GLA_DV)
            for b in range(NB):
                vh = v_ref[b, rs, vs]
                attn = jnp.where(causal, _dot_nt(q_dec[b][:, ks], k_inv[b][:, ks]), 0.0)
                st_h = st[b, h]
                o = _dot(attn.astype(BF16), vh) + _dot_nt(q_dec[b][:, ks], st_h.astype(BF16))
                st[b, h] = st_h * cdec[b][:, ks] + _dot_tn(vh, k_st[b][:, ks])
                ms = jnp.mean(o * o, axis=-1, keepdims=True)
                o = o * lax.rsqrt(ms + EPS) * nw_ref[...]
                o_ref[b, rs, vs] = (o * _silu(g_ref[b, rs, vs].astype(F32))).astype(BF16)


def _gla(pb3, ps3, w_gk2, b_gk2, norm_w):
    B, L, _ = pb3.shape
    RB, KD, VD = GLA_RB, GLA_H * GLA_DK, GLA_H * GLA_DV
    w_pad = jnp.pad(w_gk2, ((0, LANES - w_gk2.shape[0]), (0, 0)))
    const = lambda shape: pl.BlockSpec(shape, lambda c: (0,) * len(shape))
    return pl.pallas_call(
        _gla_body,
        out_shape=jax.ShapeDtypeStruct((B, L, VD), BF16),
        grid=(L // RB,),
        in_specs=[pl.BlockSpec((B, RB, KD), lambda c: (0, c, B_Q // KD)),
                  pl.BlockSpec((B, RB, KD), lambda c: (0, c, B_K // KD)),
                  pl.BlockSpec((B, RB, VD), lambda c: (0, c, B_V // VD)),
                  pl.BlockSpec((B, RB, VD), lambda c: (0, c, B_GO // VD)),
                  pl.BlockSpec((B, RB, LANES), lambda c: (0, c, 1)),
                  const((LANES, KD)), const((1, KD)), const((1, GLA_DV))],
        out_specs=pl.BlockSpec((B, RB, VD), lambda c: (0, c, 0)),
        scratch_shapes=[pltpu.VMEM((B, GLA_H, GLA_DV, GLA_DK), F32)],
        compiler_params=_cparams(("arbitrary",)),
        name="gla",
    )(pb3, pb3, pb3, pb3, ps3, w_pad, b_gk2.reshape(1, KD), norm_w.reshape(1, GLA_DV))


def _merge_body(ys_ref, yg_ref, gs_ref, gg_ref, ws_ref, wg_ref, o_ref):
    a = _dot(ys_ref[...], ws_ref[...])
    b = _dot(yg_ref[...], wg_ref[...])
    m = jax.nn.sigmoid(gs_ref[...].astype(F32)) * a + jax.nn.sigmoid(gg_ref[...].astype(F32)) * b
    o_ref[...] = m.astype(BF16)


def _merge(y_ssd, y_gla, proj_c, w_ssd_out, w_gla_out):
    T, D = y_ssd.shape
    tm, tn = 512, 1024
    return pl.pallas_call(
        _merge_body,
        out_shape=jax.ShapeDtypeStruct((T, D), BF16),
        grid=(D // tn, T // tm),
        in_specs=[pl.BlockSpec((tm, D), lambda j, i: (i, 0)),
                  pl.BlockSpec((tm, D), lambda j, i: (i, 0)),
                  pl.BlockSpec((tm, tn), lambda j, i: (i, C_GS // tn + j)),
                  pl.BlockSpec((tm, tn), lambda j, i: (i, C_GG // tn + j)),
                  pl.BlockSpec((D, tn), lambda j, i: (0, j)),
                  pl.BlockSpec((D, tn), lambda j, i: (0, j))],
        out_specs=pl.BlockSpec((tm, tn), lambda j, i: (i, j)),
        compiler_params=_cparams(("arbitrary", "arbitrary")),
        name="merge",
    )(y_ssd, y_gla, proj_c, proj_c, w_ssd_out, w_gla_out)


def _pack_bf16_pair(a, b):
    ua = lax.bitcast_convert_type(a.astype(BF16).astype(F32), U32)
    ub = lax.bitcast_convert_type(b.astype(BF16).astype(F32), U32)
    return (ua & jnp.uint32(0xFFFF0000)) | (ub >> 16)


def _unpack_bf16_pair(w):
    a = lax.bitcast_convert_type(w & jnp.uint32(0xFFFF0000), F32)
    b = lax.bitcast_convert_type(w << 16, F32)
    return a, b


def _post_body(m_ref, x_ref, g1_ref, nw_ref, sc_ref, sh_ref, wo_ref, wr_ref, br_ref, tri_ref,
               h_ref, n2p_ref, tw_ref, ti_ref, cnt_ref):
    D = x_ref.shape[1]
    h = x_ref[...] + g1_ref[0] * _dot(m_ref[...], wo_ref[...])
    h_ref[...] = h
    ms = jnp.mean(h * h, axis=-1, keepdims=True)
    n2 = h * lax.rsqrt(ms + EPS) * nw_ref[...] * (1.0 + sc_ref[0]) + sh_ref[0]
    n2p_ref[...] = _pack_bf16_pair(n2[:, :D // 2], n2[:, D // 2:])

    n_hi = n2.astype(BF16)
    n_lo = (n2 - n_hi.astype(F32)).astype(BF16)
    wr = wr_ref[...]
    w_hi = wr.astype(BF16)
    w_lo = (wr - w_hi.astype(F32)).astype(BF16)
    logits = _dot(n_hi, w_hi) + _dot(n_hi, w_lo) + _dot(n_lo, w_hi) + br_ref[...]

    lane = lax.broadcasted_iota(I32, logits.shape, 1)
    cur = jnp.where(lane < N_EXPERTS, logits, -jnp.inf)
    vals, idxs = [], []
    for _ in range(TOP_K):
        mx = jnp.max(cur, axis=-1, keepdims=True)
        ix = jnp.min(jnp.where(cur == mx, lane, LANES), axis=-1, keepdims=True)
        vals.append(mx)
        idxs.append(ix)
        cur = jnp.where(lane == ix, -jnp.inf, cur)
    es = [jnp.exp(v - vals[0]) for v in vals]
    denom = es[0] + es[1] + es[2] + es[3]

    @pl.when(pl.program_id(0) == 0)
    def _():
        cnt_ref[...] = jnp.zeros(cnt_ref.shape, F32)

    hits = [lane == ix for ix in idxs]
    onehot = jnp.where(hits[0] | hits[1] | hits[2] | hits[3], 1.0, 0.0)
    before = _dot(tri_ref[...], onehot.astype(BF16)) + cnt_ref[0:1, :]
    cnt_ref[...] = cnt_ref[...] + jnp.sum(onehot, axis=0, keepdims=True)

    tw = jnp.zeros(logits.shape, F32)
    ti = jnp.zeros(logits.shape, I32)
    for kk in range(TOP_K):
        rank = jnp.sum(jnp.where(hits[kk], before, 0.0), axis=-1, keepdims=True).astype(I32)
        tw = jnp.where(lane == kk, es[kk] / denom, tw)
        ti = jnp.where(lane == kk, idxs[kk], jnp.where(lane == TOP_K + kk, rank, ti))
    tw_ref[...] = tw
    ti_ref[...] = ti


def _post(merged, x2d, g1, norm_w, sc, sh, w_out, w_router, b_router, L):
    T, D = x2d.shape
    tm = 512
    tpb = L // tm
    wr = jnp.pad(w_router, ((0, 0), (0, LANES - w_router.shape[1])))
    br = jnp.pad(b_router.reshape(1, -1), ((0, 0), (0, LANES - b_router.shape[0])))
    row = lambda w: pl.BlockSpec((tm, w), lambda i: (i, 0))
    per_b = pl.BlockSpec((1, 1, D), lambda i: (i // tpb, 0, 0))
    const = lambda shape: pl.BlockSpec(shape, lambda i: (0,) * len(shape))
    tri = np.tril(np.ones((tm, tm), np.float32), -1)
    return pl.pallas_call(
        _post_body,
        out_shape=(jax.ShapeDtypeStruct((T, D), F32), jax.ShapeDtypeStruct((T, D // 2), U32),
                   jax.ShapeDtypeStruct((T, LANES), F32), jax.ShapeDtypeStruct((T, LANES), I32),
                   jax.ShapeDtypeStruct((8, LANES), F32)),
        grid=(T // tm,),
        in_specs=[row(D), row(D), per_b, const((1, D)), per_b, per_b, const((D, D)), const((D, LANES)),
                  const((1, LANES)), const((tm, tm))],
        out_specs=(row(D), row(D // 2), row(LANES), row(LANES), const((8, LANES))),
        compiler_params=_cparams(("arbitrary",)),
        name="post",
    )(merged, x2d, g1, norm_w, sc, sh, w_out, wr, br, jnp.asarray(tri, BF16))


def _routing(top_idx, top_rank, counts, n_sb):
    T = top_idx.shape[0]
    A = T * TOP_K
    flat_e = top_idx.reshape(A)
    rank = top_rank.reshape(A)
    padded = (counts + MOE_SUB - 1) // MOE_SUB * MOE_SUB
    pend = jnp.cumsum(padded)
    pstart = pend - padded
    dest = (pstart[flat_e] + rank).astype(I32)

    nsb = (padded + MOE_TM - 1) // MOE_TM
    sb_end = jnp.cumsum(nsb)
    sb_first = sb_end - nsb
    slot = (sb_first[flat_e] + rank // MOE_TM) * MOE_TM + rank % MOE_TM
    sb_tok = jnp.zeros((n_sb * MOE_TM,), I32).at[slot].set(jnp.arange(A, dtype=I32) // TOP_K,
                                                           unique_indices=True)

    total = sb_end[-1]
    i = jnp.arange(n_sb, dtype=I32)
    e_of = jnp.minimum(jnp.searchsorted(sb_end, i, side="right"), N_EXPERTS - 1).astype(I32)
    local = i - sb_first[e_of]
    valid = i < total
    last_e = e_of[jnp.maximum(total - 1, 0)]
    sb_e = jnp.where(valid, e_of, last_e).astype(I32)
    sb_start = jnp.where(valid, pstart[e_of] + local * MOE_TM, 0).astype(I32)
    sb_start = jnp.concatenate([sb_start, pend[-1:].astype(I32)])
    sb_n = jnp.where(valid, jnp.clip(padded[e_of] - local * MOE_TM, 0, MOE_TM), 0).astype(I32)
    return dest, sb_tok.reshape(n_sb, 1, MOE_TM), sb_e, sb_start, sb_n


def _deinterleave(gu):
    rows, two_w = gu.shape
    lane = lax.broadcasted_iota(I32, (rows, LANES), 1)
    idx_e = (2 * lane) % LANES
    idx_o = idx_e + 1
    first = lane < LANES // 2
    gates, ups = [], []
    for p in range(two_w // (2 * LANES)):
        a = gu[:, (2 * p) * LANES:(2 * p + 1) * LANES]
        b = gu[:, (2 * p + 1) * LANES:(2 * p + 2) * LANES]
        gates.append(jnp.where(first, jnp.take_along_axis(a, idx_e, axis=1), jnp.take_along_axis(b, idx_e, axis=1)))
        ups.append(jnp.where(first, jnp.take_along_axis(a, idx_o, axis=1), jnp.take_along_axis(b, idx_o, axis=1)))
    return jnp.concatenate(gates, axis=1), jnp.concatenate(ups, axis=1)


MOE_DMA_UNROLL = 16


def _moe_body(sbe_ref, sbs_ref, sbn_ref, tok_ref, tokn_ref, x_hbm, wgu_ref, bgu_ref, wd_ref, bd_ref,
              y_hbm, xs_buf, xb_buf, acc, ystage, pend, sem_in, sem_out):
    i = pl.program_id(0)
    hc = pl.program_id(1)
    n_sb = pl.num_programs(0)
    n_hc = pl.num_programs(1)
    SUB = MOE_SUB
    half = xs_buf.shape[1]
    nblk = sbn_ref[i] // SUB
    start_row = sbs_ref[i]

    def row_copy(t_ref, g, u):
        base = pl.multiple_of(g * MOE_DMA_UNROLL, MOE_DMA_UNROLL)
        return pltpu.make_async_copy(x_hbm.at[pl.ds(t_ref[0, 0, base + u], 1)], xs_buf.at[pl.ds(base + u, 1)], sem_in)

    def start_in(t_ref, n_rows):
        def body(g, carry):
            for u in range(MOE_DMA_UNROLL):
                row_copy(t_ref, g, u).start(priority=u % 2)
            return carry

        lax.fori_loop(0, n_rows // MOE_DMA_UNROLL, body, 0)

    @pl.when(hc == 0)
    def _():
        @pl.when(i == 0)
        def _():
            pend[0] = 0
            pend[1] = 0
            start_in(tok_ref, sbn_ref[0])

        def wait_in(g, carry):
            for u in range(MOE_DMA_UNROLL):
                row_copy(tok_ref, g, u).wait()
            return carry

        lax.fori_loop(0, sbn_ref[i] // MOE_DMA_UNROLL, wait_in, 0)

        def convert(r, carry):
            rows = pl.ds(pl.multiple_of(r * SUB, SUB), SUB)
            a, b = _unpack_bf16_pair(xs_buf[rows, :])
            xb_buf[rows, 0:half] = a.astype(BF16)
            xb_buf[rows, half:2 * half] = b.astype(BF16)
            return carry

        lax.fori_loop(0, nblk, convert, 0)

        @pl.when(i + 1 < n_sb)
        def _():
            start_in(tokn_ref, sbn_ref[i + 1])

    def accumulate(blocks, first):
        rows = [pl.ds(pl.multiple_of(r * SUB, SUB), SUB) for r in blocks]
        wg = wgu_ref[0].astype(BF16)
        wd = wd_ref[0].astype(BF16)
        gus = [_dot(xb_buf[rw, :], wg) + bgu_ref[0] for rw in rows]
        hs = []
        for gu in gus:
            gate, up = _deinterleave(gu)
            gate = jnp.minimum(gate, SWIGLU_LIMIT)
            up = jnp.clip(up, -SWIGLU_LIMIT, SWIGLU_LIMIT)
            hs.append((gate * jax.nn.sigmoid(gate * SWIGLU_ALPHA) * (up + 1.0)).astype(BF16))
        for rw, hidden in zip(rows, hs):
            v = _dot(hidden, wd)
            if first:
                acc[rw, :] = v + bd_ref[0]
            else:
                acc[rw, :] += v

    def run(first):
        def quad(p, carry):
            accumulate([4 * p, 4 * p + 1, 4 * p + 2, 4 * p + 3], first)
            return carry

        lax.fori_loop(0, nblk // 4, quad, 0)
        done = nblk // 4 * 4

        @pl.when(nblk - done >= 2)
        def _():
            accumulate([done, done + 1], first)

        @pl.when(nblk % 2 == 1)
        def _():
            accumulate([nblk - 1], first)

    @pl.when(nblk > 0)
    def _():
        @pl.when(hc == 0)
        def _():
            run(True)

        @pl.when(hc != 0)
        def _():
            run(False)

    @pl.when(hc == n_hc - 1)
    def _():
        def out_copy(r, slot):
            dst = pl.multiple_of(start_row + r * SUB, SUB)
            return pltpu.make_async_copy(ystage.at[slot], y_hbm.at[pl.ds(dst, SUB)], sem_out.at[slot])

        def body(r, carry):
            slot = r % 2

            @pl.when(pend[slot] == 1)
            def _():
                out_copy(r, slot).wait()

            rows = pl.ds(pl.multiple_of(r * SUB, SUB), SUB)
            v = acc[rows, :]
            ystage[slot] = _pack_bf16_pair(v[:, :half], v[:, half:])
            out_copy(r, slot).start()
            pend[slot] = 1
            return carry

        lax.fori_loop(0, nblk, body, 0)

    @pl.when((i == n_sb - 1) & (hc == n_hc - 1))
    def _():
        for slot in range(2):
            @pl.when(pend[slot] == 1)
            def _():
                pltpu.make_async_copy(ystage.at[slot], y_hbm.at[pl.ds(0, SUB)], sem_out.at[slot]).wait()
                pend[slot] = 0

        ystage[0] = jnp.zeros(ystage.shape[1:], U32)

        def tail_copy(b):
            return pltpu.make_async_copy(ystage.at[0], y_hbm.at[pl.ds(pl.multiple_of(b * SUB, SUB), SUB)], sem_out.at[0])

        def start(b, carry):
            tail_copy(b).start()
            return carry

        def wait(b, carry):
            tail_copy(b).wait()
            return carry

        first, stop = sbs_ref[n_sb] // SUB, y_hbm.shape[0] // SUB
        lax.fori_loop(first, stop, start, 0)
        lax.fori_loop(first, stop, wait, 0)


def _moe(n2p, n_rows, sb_tok, sb_e, sb_start, sb_n, w_gate_up, b_gate_up, w_down, b_down):
    half = n2p.shape[1]
    E, D, H2 = w_gate_up.shape
    n_sb = sb_e.shape[0]
    TH = MOE_TH
    n_hc = (H2 // 2) // TH
    hc_eff = lambda i, hc, sbn: jnp.where(sbn[i] > 0, hc, n_hc - 1)
    grid_spec = pltpu.PrefetchScalarGridSpec(
        num_scalar_prefetch=3,
        grid=(n_sb, n_hc),
        in_specs=[pl.BlockSpec((1, 1, MOE_TM), lambda i, hc, sbe, sbs, sbn: (i, 0, 0), memory_space=pltpu.SMEM),
                  pl.BlockSpec((1, 1, MOE_TM), lambda i, hc, sbe, sbs, sbn: (jnp.minimum(i + 1, n_sb - 1), 0, 0),
                               memory_space=pltpu.SMEM),
                  pl.BlockSpec(memory_space=pl.ANY),
                  pl.BlockSpec((1, D, 2 * TH), lambda i, hc, sbe, sbs, sbn: (sbe[i], 0, hc_eff(i, hc, sbn))),
                  pl.BlockSpec((1, 1, 2 * TH), lambda i, hc, sbe, sbs, sbn: (sbe[i], 0, hc_eff(i, hc, sbn))),
                  pl.BlockSpec((1, TH, D), lambda i, hc, sbe, sbs, sbn: (sbe[i], hc_eff(i, hc, sbn), 0)),
                  pl.BlockSpec((1, 1, D), lambda i, hc, sbe, sbs, sbn: (sbe[i], 0, 0))],
        out_specs=pl.BlockSpec(memory_space=pl.ANY),
        scratch_shapes=[pltpu.VMEM((MOE_TM, half), U32), pltpu.VMEM((MOE_TM, 2 * half), BF16),
                        pltpu.VMEM((MOE_TM, D), F32), pltpu.VMEM((2, MOE_SUB, half), U32),
                        pltpu.SMEM((2,), I32),
                        pltpu.SemaphoreType.DMA(()), pltpu.SemaphoreType.DMA((2,))],
    )
    return pl.pallas_call(
        _moe_body,
        out_shape=jax.ShapeDtypeStruct((n_rows, half), U32),
        grid_spec=grid_spec,
        compiler_params=_cparams(("arbitrary", "arbitrary")),
        name="moe",
    )(sb_e, sb_start, sb_n, sb_tok, sb_tok, n2p, w_gate_up, b_gate_up.reshape(E, 1, H2), w_down, b_down.reshape(E, 1, D))


COMBINE_TM = 512


def _combine_body(dest_ref, h_ref, tw_ref, g2_ref, nw_ref, y_hbm, o_ref, ybuf, sem):
    tm = COMBINE_TM
    n = TOP_K * tm

    def copy(j):
        return pltpu.make_async_copy(y_hbm.at[pl.ds(dest_ref[0, 0, j], 1)], ybuf.at[pl.ds(j, 1)], sem)

    def start(g, carry):
        for u in range(8):
            copy(g * 8 + u).start(priority=u % 2)
        return carry

    def wait(j, carry):
        copy(j).wait()
        return carry

    lax.fori_loop(0, n // 8, start, 0)
    lax.fori_loop(0, n, wait, 0, unroll=8)

    tw = tw_ref[...]
    lo = hi = None
    for kk in range(TOP_K):
        a, b = _unpack_bf16_pair(ybuf[kk * tm:(kk + 1) * tm, :])
        wk = tw[:, kk:kk + 1]
        lo = wk * a if lo is None else lo + wk * a
        hi = wk * b if hi is None else hi + wk * b
    ffn = jnp.concatenate([lo, hi], axis=1)
    h = h_ref[...] + g2_ref[0] * ffn
    ms = jnp.mean(h * h, axis=-1, keepdims=True)
    o_ref[...] = h * lax.rsqrt(ms + EPS) * nw_ref[...]


def _combine(h1, y_sorted, dest, top_w, g2, final_norm_w, L):
    T, D = h1.shape
    tm = COMBINE_TM
    tiles = T // tm
    tpb = L // tm
    dest_tiles = dest.reshape(tiles, tm, TOP_K).transpose(0, 2, 1).reshape(tiles, 1, TOP_K * tm)
    return pl.pallas_call(
        _combine_body,
        out_shape=jax.ShapeDtypeStruct((T, D), F32),
        grid=(tiles,),
        in_specs=[pl.BlockSpec((1, 1, TOP_K * tm), lambda i: (i, 0, 0), memory_space=pltpu.SMEM),
                  pl.BlockSpec((tm, D), lambda i: (i, 0)),
                  pl.BlockSpec((tm, LANES), lambda i: (i, 0)),
                  pl.BlockSpec((1, 1, D), lambda i: (i // tpb, 0, 0)),
                  pl.BlockSpec((1, D), lambda i: (0, 0)),
                  pl.BlockSpec(memory_space=pl.ANY)],
        out_specs=pl.BlockSpec((tm, D), lambda i: (i, 0)),
        scratch_shapes=[pltpu.VMEM((TOP_K * tm, D // 2), U32), pltpu.SemaphoreType.DMA(())],
        compiler_params=_cparams(("arbitrary",)),
        name="combine",
    )(dest_tiles, h1, top_w, g2, final_norm_w.reshape(1, D), y_sorted)


def kernel(x, c, w_ada, b_ada, norm1_w, w_in, ssd_conv_w, ssd_conv_b, ssd_dt_bias, ssd_a_log, ssd_d_skip, ssd_norm_w, gla_w_gk2, gla_b_gk2, gla_norm_w, w_ssd_out, w_gla_out, w_out, norm2_w, w_router, b_router, w_gate_up, b_gate_up, w_down, b_down, final_norm_w):
    B, L, D = x.shape
    T = B * L
    assert D == D_MODEL and w_ada.shape[0] == 1
    x2d = x.reshape(T, D)

    mod = _ada(c, w_ada[0], b_ada[0])
    sh1, sc1, g1, sh2, sc2, g2 = [mod[:, i * D:(i + 1) * D].reshape(B, 1, D) for i in range(6)]

    w_t = w_in[0].T
    pad = lambda a: jnp.pad(a, ((0, LANES - a.shape[0]), (0, 0)))
    w_s = jnp.concatenate([pad(w_t[SRC_DT[0]:SRC_DT[1]]), pad(w_t[SRC_GK[0]:SRC_GK[1]])], axis=0)
    proj_a, n1 = _inproj_a(x2d, norm1_w[0].reshape(1, D), sc1, sh1, w_t, L)
    proj_b = _matmul(n1, w_t, SRC_B, PROJ_TN, "inproj_b")
    proj_c = _matmul(n1, w_t, SRC_C, PROJ_TN, "inproj_c")
    proj_s = _matmul(n1, w_s, (0, 2 * LANES), 2 * LANES, "inproj_s")
    pa3, pb3, ps3 = [p.reshape(B, L, p.shape[1]) for p in (proj_a, proj_b, proj_s)]
    y_ssd = _ssd(pa3, ps3, ssd_conv_w[0], ssd_conv_b[0], ssd_dt_bias[0], ssd_a_log[0], ssd_d_skip[0], ssd_norm_w[0])
    y_gla = _gla(pb3, ps3, gla_w_gk2[0], gla_b_gk2[0], gla_norm_w[0])
    merged = _merge(y_ssd.reshape(T, D), y_gla.reshape(T, D), proj_c, w_ssd_out[0].astype(BF16), w_gla_out[0].astype(BF16))
    h1, n2p, top_w, top_i, cnt = _post(merged, x2d, g1, norm2_w[0].reshape(1, D), sc2, sh2, w_out[0].astype(BF16),
                                       w_router[0], b_router[0], L)

    A = T * TOP_K
    n_rows = A + N_EXPERTS * MOE_SUB
    n_sb = n_rows // MOE_TM + N_EXPERTS * (MOE_TM - MOE_SUB) // MOE_TM
    dest, sb_tok, sb_e, sb_start, sb_n = _routing(top_i[:, :TOP_K], top_i[:, TOP_K:2 * TOP_K],
                                                  cnt[0, :N_EXPERTS].astype(I32), n_sb)
    y_sorted = _moe(n2p, n_rows, sb_tok, sb_e, sb_start, sb_n, w_gate_up[0], b_gate_up[0], w_down[0], b_down[0])
    out = _combine(h1, y_sorted, dest, top_w, g2, final_norm_w, L)
    return out.reshape(B, L, D)
```

```python
import functools

import jax
import jax.numpy as jnp
import numpy as np
from jax import lax
from jax.experimental import pallas as pl
from jax.experimental.pallas import tpu as pltpu

F32 = jnp.float32
BF16 = jnp.bfloat16
U32 = jnp.uint32
I32 = jnp.int32

EPS = 1e-6
LANES = 128
VMEM_LIMIT = 56 * 1024 * 1024

D_MODEL = 2048
SSD_HEADS, SSD_P, SSD_G, SSD_N, SSD_CONV, SSD_Q = 32, 64, 4, 128, 4, 256
SSD_INNER = SSD_HEADS * SSD_P
SSD_TAIL = 16
GLA_H, GLA_DK, GLA_DV, GLA_RANK, GLA_C = 4, 256, 512, 16, 64
GLA_GATE_NORM = 16.0
N_EXPERTS, TOP_K = 32, 4
SWIGLU_LIMIT, SWIGLU_ALPHA = 7.0, 1.702

SRC_A = (0, 5120)
SRC_DT = (5120, 5152)
SRC_B = (5152, 11296)
SRC_GK = (11296, 11312)
SRC_C = (11312, 15408)
A_Z, A_XS, A_BC = 0, 2048, 4096
B_Q, B_K, B_V, B_GO = 0, 1024, 2048, 4096
C_GS, C_GG = 0, 2048
PROJ_TN = 1024

MOE_SUB = 256
MOE_TM = 2048
MOE_TH = 256


def _cparams(sem, vmem=VMEM_LIMIT):
    return pltpu.CompilerParams(dimension_semantics=sem, vmem_limit_bytes=vmem)


def _split3(x):
    hi = x.astype(BF16)
    r1 = x - hi.astype(F32)
    mid = r1.astype(BF16)
    lo = (r1 - mid.astype(F32)).astype(BF16)
    return hi, mid, lo


def _dot(a, b):
    return jnp.dot(a, b, preferred_element_type=F32)


def _dot_nt(a, b):
    return lax.dot_general(a, b, (((1,), (1,)), ((), ())), preferred_element_type=F32)


def _dot_tn(a, b):
    return lax.dot_general(a, b, (((0,), (0,)), ((), ())), preferred_element_type=F32)


def _dot_exact_lhs(a01, x):
    hi, mid, lo = _split3(x)
    return _dot(a01, hi) + _dot(a01, mid) + _dot(a01, lo)


def _dot_hilo_rhs(x, b01):
    hi = x.astype(BF16)
    lo = (x - hi.astype(F32)).astype(BF16)
    return _dot(hi, b01) + _dot(lo, b01)


def _silu(x):
    return x * jax.nn.sigmoid(x)


def _softplus(x):
    return jnp.maximum(x, 0.0) + jnp.log1p(jnp.exp(-jnp.abs(x)))


def _ada_body(ct_ref, w_ref, b_ref, o_ref):
    ct = ct_ref[...]
    cs = _silu(ct)
    w = w_ref[...]
    for m in range(ct.shape[1]):
        o_ref[m:m + 1, :] = jnp.sum(w * cs[:, m:m + 1], axis=0, keepdims=True) + b_ref[...]


def _ada(c, w_ada, b_ada):
    B, D = c.shape
    N = w_ada.shape[1]
    tn = 1024
    return pl.pallas_call(
        _ada_body,
        out_shape=jax.ShapeDtypeStruct((B, N), F32),
        grid=(N // tn,),
        in_specs=[pl.BlockSpec((D, B), lambda j: (0, 0)),
                  pl.BlockSpec((D, tn), lambda j: (0, j)),
                  pl.BlockSpec((1, tn), lambda j: (0, j))],
        out_specs=pl.BlockSpec((B, tn), lambda j: (0, j)),
        compiler_params=_cparams(("arbitrary",)),
        name="ada",
    )(c.T, w_ada, b_ada.reshape(1, N))


def _inproj_a_body(x_ref, nw_ref, sc_ref, sh_ref, w_ref, o_ref, n_ref):
    @pl.when(pl.program_id(1) == 0)
    def _():
        x = x_ref[...]
        ms = jnp.mean(x * x, axis=-1, keepdims=True)
        y = x * lax.rsqrt(ms + EPS) * nw_ref[...]
        n_ref[...] = (y * (1.0 + sc_ref[0]) + sh_ref[0]).astype(BF16)

    o_ref[...] = _dot_nt(n_ref[...], w_ref[...].astype(BF16)).astype(BF16)


def _inproj_a(x2d, norm_w, sc, sh, w_t, L):
    T, D = x2d.shape
    tm, tn = 1024, PROJ_TN // 2
    n_cols = SRC_A[1] - SRC_A[0]
    tiles_per_batch = L // tm
    return pl.pallas_call(
        _inproj_a_body,
        out_shape=(jax.ShapeDtypeStruct((T, n_cols), BF16), jax.ShapeDtypeStruct((T, D), BF16)),
        grid=(T // tm, n_cols // tn),
        in_specs=[pl.BlockSpec((tm, D), lambda i, j: (i, 0)),
                  pl.BlockSpec((1, D), lambda i, j: (0, 0)),
                  pl.BlockSpec((1, 1, D), lambda i, j: (i // tiles_per_batch, 0, 0)),
                  pl.BlockSpec((1, 1, D), lambda i, j: (i // tiles_per_batch, 0, 0)),
                  pl.BlockSpec((tn, D), lambda i, j: (SRC_A[0] // tn + j, 0))],
        out_specs=(pl.BlockSpec((tm, tn), lambda i, j: (i, j)), pl.BlockSpec((tm, D), lambda i, j: (i, 0))),
        compiler_params=_cparams(("arbitrary", "arbitrary")),
        name="inproj_a",
    )(x2d, norm_w, sc, sh, w_t)


def _matmul_body(a_ref, w_ref, o_ref):
    o_ref[...] = _dot_nt(a_ref[...], w_ref[...].astype(BF16)).astype(BF16)


def _matmul(a, w_t, rows, tn, name):
    T, D = a.shape
    N = rows[1] - rows[0]
    tm = 2048
    return pl.pallas_call(
        _matmul_body,
        out_shape=jax.ShapeDtypeStruct((T, N), BF16),
        grid=(T // tm, N // tn),
        in_specs=[pl.BlockSpec((tm, D), lambda i, j: (i, 0)),
                  pl.BlockSpec((pl.Element(tn), pl.Element(D)),
                               lambda i, j: ((rows[0] // 8 + j * (tn // 8)) * 8, 0))],
        out_specs=pl.BlockSpec((tm, tn), lambda i, j: (i, j)),
        compiler_params=_cparams(("arbitrary", "arbitrary")),
        name=name,
    )(a, w_t)


def _ssd_body(z_ref, xs_ref, bc_ref, dt_ref, cwx_ref, cwbc_ref, cbx_ref, cbbc_ref, dtb_ref, alog_ref,
              dskip_ref, nw_ref, exp_ref, sh_ref, o_ref, ubx, ubbc, state):
    Q = SSD_Q
    c = pl.program_id(1)

    @pl.when(c == 0)
    def _():
        ubx[...] = jnp.zeros(ubx.shape, F32)
        ubbc[...] = jnp.zeros(ubbc.shape, F32)
        state[...] = jnp.zeros(state.shape, F32)

    def conv_silu(u_ref, tail, cw_ref, cb_ref):
        u = u_ref[0]
        head = u_ref[0, 0:SSD_TAIL, :].astype(F32)[0:8, :]
        prev = tail[...]
        r8 = lax.broadcasted_iota(I32, head.shape, 0)
        w_last = cw_ref[SSD_CONV - 1:SSD_CONV, :]
        acc = cb_ref[...] + w_last * u.astype(F32)
        acc8 = cb_ref[...] + w_last * head
        for s in range(1, SSD_CONV):
            w_s = cw_ref[SSD_CONV - 1 - s:SSD_CONV - s, :]
            acc = acc + w_s * _dot(sh_ref[s - 1], u)
            acc8 = acc8 + w_s * jnp.where(r8 < s, pltpu.roll(prev, s, 0), pltpu.roll(head, s, 0))
        tail[...] = u_ref[0, Q - SSD_TAIL:Q, :].astype(F32)[SSD_TAIL - 8:, :]
        return _silu(jnp.concatenate([acc8, acc[8:, :]], axis=0))

    xs = conv_silu(xs_ref, ubx, cwx_ref, cbx_ref)
    bcm = conv_silu(bc_ref, ubbc, cwbc_ref, cbbc_ref)
    GN = SSD_G * SSD_N
    bm_f, cm = bcm[:, :GN], bcm[:, GN:].astype(BF16)

    dt = _softplus(dt_ref[0].astype(F32) + dtb_ref[...])
    a_neg = -jnp.exp(alog_ref[...])
    dA = dt * a_neg

    row = lax.broadcasted_iota(I32, (Q, Q), 0)
    col = lax.broadcasted_iota(I32, (Q, Q), 1)
    causal = row >= col
    tri = jnp.where(causal, 1.0, 0.0).astype(BF16)
    acs = _dot_exact_lhs(tri, dA)
    acs_t = acs.T
    acs_last = acs[Q - 1:Q, :]
    exp_a = jnp.exp(acs)
    decay_st = jnp.exp(acs_last - acs)

    expand = exp_ref[...]
    dt_e = _dot_hilo_rhs(dt, expand)
    dtd_e = _dot_hilo_rhs(dt * decay_st, expand)
    expa_e = _dot_hilo_rhs(exp_a, expand)

    x_dt = (xs * dt_e).astype(BF16)
    x_dd = (xs * dtd_e).astype(BF16)

    lane = lax.broadcasted_iota(I32, (Q, LANES), 1)
    lo_half = lane < SSD_P
    HG = SSD_HEADS // SSD_G
    GW = HG * SSD_P
    y_groups = []
    for g in range(SSD_G):
        cg = cm[:, g * SSD_N:(g + 1) * SSD_N]
        bg_f = bm_f[:, g * SSD_N:(g + 1) * SSD_N]
        bg = bg_f.astype(BF16)
        scores = _dot_nt(cg, bg)
        pieces = []
        for p in range(HG // 2):
            h0 = g * HG + 2 * p
            xp = x_dt[:, h0 * SSD_P:h0 * SSD_P + LANES]
            acc = None
            for s in range(2):
                h = h0 + s
                seg = acs[:, h:h + 1] - acs_t[h:h + 1, :]
                lmat = jnp.exp(jnp.where(causal, seg, -jnp.inf))
                m = (scores * lmat).astype(BF16)
                xh = jnp.where(lo_half if s == 0 else jnp.logical_not(lo_half), xp, jnp.zeros_like(xp))
                part = _dot(m, xh)
                acc = part if acc is None else acc + part
            pieces.append(acc)
        y_diag = jnp.concatenate(pieces, axis=1)
        st_old = state[g]
        y_off = _dot(cg, st_old.astype(BF16)) * expa_e[:, g * GW:(g + 1) * GW]
        st_new = _dot(bg_f.T.astype(BF16), x_dd[:, g * GW:(g + 1) * GW])
        state[g] = st_old * expa_e[Q - 1:Q, g * GW:(g + 1) * GW] + st_new
        y_groups.append(y_diag + y_off)

    y = jnp.concatenate(y_groups, axis=1) + xs * dskip_ref[...]
    y = y * _silu(z_ref[0].astype(F32))
    outs = []
    for g in range(SSD_G):
        yg = y[:, g * GW:(g + 1) * GW]
        ms = jnp.mean(yg * yg, axis=-1, keepdims=True)
        outs.append(yg * lax.rsqrt(ms + EPS) * nw_ref[:, g * GW:(g + 1) * GW])
    o_ref[0] = jnp.concatenate(outs, axis=1).astype(BF16)


def _ssd(pa3, ps3, conv_w, conv_b, dt_bias, a_log, d_skip, norm_w):
    B, L, _ = pa3.shape
    Q, DI, BCW = SSD_Q, SSD_INNER, 2 * SSD_G * SSD_N
    padl = lambda a: jnp.pad(a.reshape(1, -1), ((0, 0), (0, LANES - a.shape[-1])))
    expand = (np.arange(LANES)[:, None] == (np.arange(DI)[None, :] // SSD_P)).astype(np.float32)
    t = np.arange(Q)[:, None]
    shifts = np.stack([(t - s == np.arange(Q)[None, :]) for s in range(1, SSD_CONV)]).astype(np.float32)
    const = lambda shape: pl.BlockSpec(shape, lambda b, c: (0,) * len(shape))
    return pl.pallas_call(
        _ssd_body,
        out_shape=jax.ShapeDtypeStruct((B, L, DI), BF16),
        grid=(B, L // Q),
        in_specs=[pl.BlockSpec((1, Q, DI), lambda b, c: (b, c, A_Z // DI)),
                  pl.BlockSpec((1, Q, DI), lambda b, c: (b, c, A_XS // DI)),
                  pl.BlockSpec((1, Q, BCW), lambda b, c: (b, c, A_BC // BCW)),
                  pl.BlockSpec((1, Q, LANES), lambda b, c: (b, c, 0)),
                  const((SSD_CONV, DI)), const((SSD_CONV, BCW)), const((1, DI)), const((1, BCW)),
                  const((1, LANES)), const((1, LANES)), const((1, DI)), const((1, DI)),
                  const((LANES, DI)), const((SSD_CONV - 1, Q, Q))],
        out_specs=pl.BlockSpec((1, Q, DI), lambda b, c: (b, c, 0)),
        scratch_shapes=[pltpu.VMEM((8, DI), F32), pltpu.VMEM((8, BCW), F32),
                        pltpu.VMEM((SSD_G, SSD_N, DI // SSD_G), F32)],
        compiler_params=_cparams(("arbitrary", "arbitrary")),
        name="ssd",
    )(pa3, pa3, pa3, ps3, conv_w[:, :DI], conv_w[:, DI:], conv_b[:DI].reshape(1, DI),
      conv_b[DI:].reshape(1, BCW), padl(dt_bias), padl(a_log), jnp.repeat(d_skip, SSD_P).reshape(1, DI),
      norm_w.reshape(1, DI), jnp.asarray(expand, BF16), jnp.asarray(shifts, BF16))


GLA_RB = 256


def _gla_body(q_ref, k_ref, v_ref, g_ref, gk_ref, w_ref, b_ref, nw_ref, o_ref, st):
    RB, C = GLA_RB, GLA_C

    NB = q_ref.shape[0]

    @pl.when(pl.program_id(0) == 0)
    def _():
        st[...] = jnp.zeros(st.shape, F32)

    w_hi, w_mid, w_lo = _split3(w_ref[...])
    row = lax.broadcasted_iota(I32, (RB, RB), 0)
    col = lax.broadcasted_iota(I32, (RB, RB), 1)
    blocktri = jnp.where((row // C == col // C) & (row >= col), 1.0, 0.0).astype(BF16)
    bcs_all = []
    for b in range(NB):
        gk = gk_ref[b]
        pre = _dot(gk, w_hi) + _dot(gk, w_mid) + _dot(gk, w_lo) + b_ref[...]
        log_a = (jnp.minimum(pre, 0.0) - jnp.log1p(jnp.exp(-jnp.abs(pre)))) * (1.0 / GLA_GATE_NORM)
        bcs_all.append(_dot_exact_lhs(blocktri, log_a))

    r64 = lax.broadcasted_iota(I32, (C, C), 0)
    c64 = lax.broadcasted_iota(I32, (C, C), 1)
    causal = r64 >= c64
    scale = GLA_DK ** -0.5

    for s in range(RB // C):
        rs = slice(s * C, (s + 1) * C)
        q_dec, k_inv, k_st, cdec = [], [], [], []
        for b in range(NB):
            bcs = bcs_all[b][rs, :]
            last = bcs[C - 1:C, :]
            qf = q_ref[b, rs, :].astype(F32)
            kf = k_ref[b, rs, :].astype(F32)
            q_dec.append((qf * scale * jnp.exp(bcs)).astype(BF16))
            k_inv.append((kf * jnp.exp(-bcs)).astype(BF16))
            k_st.append((kf * jnp.exp(last - bcs)).astype(BF16))
            cdec.append(jnp.exp(last))
        for h in range(GLA_H):
            ks = slice(h * GLA_DK, (h + 1) * GLA_DK)
            vs = slice(h * GLA_DV, (h + 1) * GLA_DV)
            for b in range(NB):
                vh = v_ref[b, rs, vs]
                attn = jnp.where(causal, _dot_nt(q_dec[b][:, ks], k_inv[b][:, ks]), 0.0)
                st_h = st[b, h]
                o = _dot(attn.astype(BF16), vh) + _dot_nt(q_dec[b][:, ks], st_h.astype(BF16))
                st[b, h] = st_h * cdec[b][:, ks] + _dot_tn(vh, k_st[b][:, ks])
                ms = jnp.mean(o * o, axis=-1, keepdims=True)
                o = o * lax.rsqrt(ms + EPS) * nw_ref[...]
                o_ref[b, rs, vs] = (o * _silu(g_ref[b, rs, vs].astype(F32))).astype(BF16)


def _gla(pb3, ps3, w_gk2, b_gk2, norm_w):
    B, L, _ = pb3.shape
    RB, KD, VD = GLA_RB, GLA_H * GLA_DK, GLA_H * GLA_DV
    w_pad = jnp.pad(w_gk2, ((0, LANES - w_gk2.shape[0]), (0, 0)))
    const = lambda shape: pl.BlockSpec(shape, lambda c: (0,) * len(shape))
    return pl.pallas_call(
        _gla_body,
        out_shape=jax.ShapeDtypeStruct((B, L, VD), BF16),
        grid=(L // RB,),
        in_specs=[pl.BlockSpec((B, RB, KD), lambda c: (0, c, B_Q // KD)),
                  pl.BlockSpec((B, RB, KD), lambda c: (0, c, B_K // KD)),
                  pl.BlockSpec((B, RB, VD), lambda c: (0, c, B_V // VD)),
                  pl.BlockSpec((B, RB, VD), lambda c: (0, c, B_GO // VD)),
                  pl.BlockSpec((B, RB, LANES), lambda c: (0, c, 1)),
                  const((LANES, KD)), const((1, KD)), const((1, GLA_DV))],
        out_specs=pl.BlockSpec((B, RB, VD), lambda c: (0, c, 0)),
        scratch_shapes=[pltpu.VMEM((B, GLA_H, GLA_DV, GLA_DK), F32)],
        compiler_params=_cparams(("arbitrary",)),
        name="gla",
    )(pb3, pb3, pb3, pb3, ps3, w_pad, b_gk2.reshape(1, KD), norm_w.reshape(1, GLA_DV))


def _merge_body(ys_ref, yg_ref, gs_ref, gg_ref, ws_ref, wg_ref, o_ref):
    a = _dot(ys_ref[...], ws_ref[...])
    b = _dot(yg_ref[...], wg_ref[...])
    m = jax.nn.sigmoid(gs_ref[...].astype(F32)) * a + jax.nn.sigmoid(gg_ref[...].astype(F32)) * b
    o_ref[...] = m.astype(BF16)


def _merge(y_ssd, y_gla, proj_c, w_ssd_out, w_gla_out):
    T, D = y_ssd.shape
    tm, tn = 1024, 1024
    return pl.pallas_call(
        _merge_body,
        out_shape=jax.ShapeDtypeStruct((T, D), BF16),
        grid=(D // tn, T // tm),
        in_specs=[pl.BlockSpec((tm, D), lambda j, i: (i, 0)),
                  pl.BlockSpec((tm, D), lambda j, i: (i, 0)),
                  pl.BlockSpec((tm, tn), lambda j, i: (i, C_GS // tn + j)),
                  pl.BlockSpec((tm, tn), lambda j, i: (i, C_GG // tn + j)),
                  pl.BlockSpec((D, tn), lambda j, i: (0, j)),
                  pl.BlockSpec((D, tn), lambda j, i: (0, j))],
        out_specs=pl.BlockSpec((tm, tn), lambda j, i: (i, j)),
        compiler_params=_cparams(("arbitrary", "arbitrary")),
        name="merge",
    )(y_ssd, y_gla, proj_c, proj_c, w_ssd_out, w_gla_out)


def _pack_bf16_pair(a, b):
    ua = lax.bitcast_convert_type(a.astype(BF16).astype(F32), U32)
    ub = lax.bitcast_convert_type(b.astype(BF16).astype(F32), U32)
    return (ua & jnp.uint32(0xFFFF0000)) | (ub >> 16)


def _unpack_bf16_pair(w):
    a = lax.bitcast_convert_type(w & jnp.uint32(0xFFFF0000), F32)
    b = lax.bitcast_convert_type(w << 16, F32)
    return a, b


def _post_body(m_ref, x_ref, g1_ref, nw_ref, sc_ref, sh_ref, wo_ref, wr_ref, br_ref, tri_ref,
               h_ref, n2p_ref, tw_ref, ti_ref, cnt_ref):
    D = x_ref.shape[1]
    h = x_ref[...] + g1_ref[0] * _dot(m_ref[...], wo_ref[...])
    h_ref[...] = h
    ms = jnp.mean(h * h, axis=-1, keepdims=True)
    n2 = h * lax.rsqrt(ms + EPS) * nw_ref[...] * (1.0 + sc_ref[0]) + sh_ref[0]
    n2p_ref[...] = _pack_bf16_pair(n2[:, :D // 2], n2[:, D // 2:])

    n_hi = n2.astype(BF16)
    n_lo = (n2 - n_hi.astype(F32)).astype(BF16)
    wr = wr_ref[...]
    w_hi = wr.astype(BF16)
    w_lo = (wr - w_hi.astype(F32)).astype(BF16)
    logits = _dot(n_hi, w_hi) + _dot(n_hi, w_lo) + _dot(n_lo, w_hi) + br_ref[...]

    lane = lax.broadcasted_iota(I32, logits.shape, 1)
    cur = jnp.where(lane < N_EXPERTS, logits, -jnp.inf)
    vals, idxs = [], []
    for _ in range(TOP_K):
        mx = jnp.max(cur, axis=-1, keepdims=True)
        ix = jnp.min(jnp.where(cur == mx, lane, LANES), axis=-1, keepdims=True)
        vals.append(mx)
        idxs.append(ix)
        cur = jnp.where(lane == ix, -jnp.inf, cur)
    es = [jnp.exp(v - vals[0]) for v in vals]
    denom = es[0] + es[1] + es[2] + es[3]

    @pl.when(pl.program_id(0) == 0)
    def _():
        cnt_ref[...] = jnp.zeros(cnt_ref.shape, F32)

    hits = [lane == ix for ix in idxs]
    onehot = jnp.where(hits[0] | hits[1] | hits[2] | hits[3], 1.0, 0.0)
    before = _dot(tri_ref[...], onehot.astype(BF16)) + cnt_ref[0:1, :]
    cnt_ref[...] = cnt_ref[...] + jnp.sum(onehot, axis=0, keepdims=True)

    tw = jnp.zeros(logits.shape, F32)
    ti = jnp.zeros(logits.shape, I32)
    for kk in range(TOP_K):
        rank = jnp.sum(jnp.where(hits[kk], before, 0.0), axis=-1, keepdims=True).astype(I32)
        tw = jnp.where(lane == kk, es[kk] / denom, tw)
        ti = jnp.where(lane == kk, idxs[kk], jnp.where(lane == TOP_K + kk, rank, ti))
    tw_ref[...] = tw
    ti_ref[...] = ti


def _post(merged, x2d, g1, norm_w, sc, sh, w_out, w_router, b_router, L):
    T, D = x2d.shape
    tm = 512
    tpb = L // tm
    wr = jnp.pad(w_router, ((0, 0), (0, LANES - w_router.shape[1])))
    br = jnp.pad(b_router.reshape(1, -1), ((0, 0), (0, LANES - b_router.shape[0])))
    row = lambda w: pl.BlockSpec((tm, w), lambda i: (i, 0))
    per_b = pl.BlockSpec((1, 1, D), lambda i: (i // tpb, 0, 0))
    const = lambda shape: pl.BlockSpec(shape, lambda i: (0,) * len(shape))
    tri = np.tril(np.ones((tm, tm), np.float32), -1)
    return pl.pallas_call(
        _post_body,
        out_shape=(jax.ShapeDtypeStruct((T, D), F32), jax.ShapeDtypeStruct((T, D // 2), U32),
                   jax.ShapeDtypeStruct((T, LANES), F32), jax.ShapeDtypeStruct((T, LANES), I32),
                   jax.ShapeDtypeStruct((8, LANES), F32)),
        grid=(T // tm,),
        in_specs=[row(D), row(D), per_b, const((1, D)), per_b, per_b, const((D, D)), const((D, LANES)),
                  const((1, LANES)), const((tm, tm))],
        out_specs=(row(D), row(D // 2), row(LANES), row(LANES), const((8, LANES))),
        compiler_params=_cparams(("arbitrary",)),
        name="post",
    )(merged, x2d, g1, norm_w, sc, sh, w_out, wr, br, jnp.asarray(tri, BF16))


def _routing(top_idx, top_rank, counts, n_sb):
    T = top_idx.shape[0]
    A = T * TOP_K
    flat_e = top_idx.reshape(A)
    rank = top_rank.reshape(A)
    padded = (counts + MOE_SUB - 1) // MOE_SUB * MOE_SUB
    pend = jnp.cumsum(padded)
    pstart = pend - padded
    dest = (pstart[flat_e] + rank).astype(I32)

    nsb = (padded + MOE_TM - 1) // MOE_TM
    sb_end = jnp.cumsum(nsb)
    sb_first = sb_end - nsb
    slot = (sb_first[flat_e] + rank // MOE_TM) * MOE_TM + rank % MOE_TM
    sb_tok = jnp.zeros((n_sb * MOE_TM,), I32).at[slot].set(jnp.arange(A, dtype=I32) // TOP_K,
                                                           unique_indices=True)

    total = sb_end[-1]
    i = jnp.arange(n_sb, dtype=I32)
    e_of = jnp.minimum(jnp.searchsorted(sb_end, i, side="right"), N_EXPERTS - 1).astype(I32)
    local = i - sb_first[e_of]
    valid = i < total
    last_e = e_of[jnp.maximum(total - 1, 0)]
    sb_e = jnp.where(valid, e_of, last_e).astype(I32)
    sb_start = jnp.where(valid, pstart[e_of] + local * MOE_TM, 0).astype(I32)
    sb_start = jnp.concatenate([sb_start, pend[-1:].astype(I32)])
    sb_n = jnp.where(valid, jnp.clip(padded[e_of] - local * MOE_TM, 0, MOE_TM), 0).astype(I32)
    return dest, sb_tok.reshape(n_sb, 1, MOE_TM), sb_e, sb_start, sb_n


def _deinterleave(gu):
    rows, two_w = gu.shape
    lane = lax.broadcasted_iota(I32, (rows, LANES), 1)
    idx_e = (2 * lane) % LANES
    idx_o = idx_e + 1
    first = lane < LANES // 2
    gates, ups = [], []
    for p in range(two_w // (2 * LANES)):
        a = gu[:, (2 * p) * LANES:(2 * p + 1) * LANES]
        b = gu[:, (2 * p + 1) * LANES:(2 * p + 2) * LANES]
        gates.append(jnp.where(first, jnp.take_along_axis(a, idx_e, axis=1), jnp.take_along_axis(b, idx_e, axis=1)))
        ups.append(jnp.where(first, jnp.take_along_axis(a, idx_o, axis=1), jnp.take_along_axis(b, idx_o, axis=1)))
    return jnp.concatenate(gates, axis=1), jnp.concatenate(ups, axis=1)


MOE_DMA_UNROLL = 16


def _moe_body(sbe_ref, sbs_ref, sbn_ref, tok_ref, tokn_ref, x_hbm, wgu_ref, bgu_ref, wd_ref, bd_ref,
              y_hbm, xs_buf, xb_buf, acc, ystage, pend, sem_in, sem_out):
    i = pl.program_id(0)
    hc = pl.program_id(1)
    n_sb = pl.num_programs(0)
    n_hc = pl.num_programs(1)
    SUB = MOE_SUB
    half = xs_buf.shape[1]
    nblk = sbn_ref[i] // SUB
    start_row = sbs_ref[i]

    def row_copy(t_ref, g, u):
        base = pl.multiple_of(g * MOE_DMA_UNROLL, MOE_DMA_UNROLL)
        return pltpu.make_async_copy(x_hbm.at[pl.ds(t_ref[0, 0, base + u], 1)], xs_buf.at[pl.ds(base + u, 1)], sem_in)

    def start_in(t_ref, n_rows):
        def body(g, carry):
            for u in range(MOE_DMA_UNROLL):
                row_copy(t_ref, g, u).start(priority=u % 2)
            return carry

        lax.fori_loop(0, n_rows // MOE_DMA_UNROLL, body, 0)

    @pl.when(hc == 0)
    def _():
        @pl.when(i == 0)
        def _():
            pend[0] = 0
            pend[1] = 0
            start_in(tok_ref, sbn_ref[0])

        def wait_in(g, carry):
            for u in range(MOE_DMA_UNROLL):
                row_copy(tok_ref, g, u).wait()
            return carry

        lax.fori_loop(0, sbn_ref[i] // MOE_DMA_UNROLL, wait_in, 0)

        def convert(r, carry):
            rows = pl.ds(pl.multiple_of(r * SUB, SUB), SUB)
            a, b = _unpack_bf16_pair(xs_buf[rows, :])
            xb_buf[rows, 0:half] = a.astype(BF16)
            xb_buf[rows, half:2 * half] = b.astype(BF16)
            return carry

        lax.fori_loop(0, nblk, convert, 0)

        @pl.when(i + 1 < n_sb)
        def _():
            start_in(tokn_ref, sbn_ref[i + 1])

    def accumulate(blocks, first):
        rows = [pl.ds(pl.multiple_of(r * SUB, SUB), SUB) for r in blocks]
        wg = wgu_ref[0].astype(BF16)
        wd = wd_ref[0].astype(BF16)
        gus = [_dot(xb_buf[rw, :], wg) + bgu_ref[0] for rw in rows]
        hs = []
        for gu in gus:
            gate, up = _deinterleave(gu)
            gate = jnp.minimum(gate, SWIGLU_LIMIT)
            up = jnp.clip(up, -SWIGLU_LIMIT, SWIGLU_LIMIT)
            hs.append((gate * jax.nn.sigmoid(gate * SWIGLU_ALPHA) * (up + 1.0)).astype(BF16))
        for rw, hidden in zip(rows, hs):
            v = _dot(hidden, wd)
            if first:
                acc[rw, :] = v + bd_ref[0]
            else:
                acc[rw, :] += v

    def run(first):
        def quad(p, carry):
            accumulate([4 * p, 4 * p + 1, 4 * p + 2, 4 * p + 3], first)
            return carry

        lax.fori_loop(0, nblk // 4, quad, 0)
        done = nblk // 4 * 4

        @pl.when(nblk - done >= 2)
        def _():
            accumulate([done, done + 1], first)

        @pl.when(nblk % 2 == 1)
        def _():
            accumulate([nblk - 1], first)

    @pl.when(nblk > 0)
    def _():
        @pl.when(hc == 0)
        def _():
            run(True)

        @pl.when(hc != 0)
        def _():
            run(False)

    @pl.when(hc == n_hc - 1)
    def _():
        def out_copy(r, slot):
            dst = pl.multiple_of(start_row + r * SUB, SUB)
            return pltpu.make_async_copy(ystage.at[slot], y_hbm.at[pl.ds(dst, SUB)], sem_out.at[slot])

        def body(r, carry):
            slot = r % 2

            @pl.when(pend[slot] == 1)
            def _():
                out_copy(r, slot).wait()

            rows = pl.ds(pl.multiple_of(r * SUB, SUB), SUB)
            v = acc[rows, :]
            ystage[slot] = _pack_bf16_pair(v[:, :half], v[:, half:])
            out_copy(r, slot).start()
            pend[slot] = 1
            return carry

        lax.fori_loop(0, nblk, body, 0)

    @pl.when((i == n_sb - 1) & (hc == n_hc - 1))
    def _():
        for slot in range(2):
            @pl.when(pend[slot] == 1)
            def _():
                pltpu.make_async_copy(ystage.at[slot], y_hbm.at[pl.ds(0, SUB)], sem_out.at[slot]).wait()
                pend[slot] = 0

        ystage[0] = jnp.zeros(ystage.shape[1:], U32)

        def tail_copy(b):
            return pltpu.make_async_copy(ystage.at[0], y_hbm.at[pl.ds(pl.multiple_of(b * SUB, SUB), SUB)], sem_out.at[0])

        def start(b, carry):
            tail_copy(b).start()
            return carry

        def wait(b, carry):
            tail_copy(b).wait()
            return carry

        first, stop = sbs_ref[n_sb] // SUB, y_hbm.shape[0] // SUB
        lax.fori_loop(first, stop, start, 0)
        lax.fori_loop(first, stop, wait, 0)


def _moe(n2p, n_rows, sb_tok, sb_e, sb_start, sb_n, w_gate_up, b_gate_up, w_down, b_down):
    half = n2p.shape[1]
    E, D, H2 = w_gate_up.shape
    n_sb = sb_e.shape[0]
    TH = MOE_TH
    n_hc = (H2 // 2) // TH
    hc_eff = lambda i, hc, sbn: jnp.where(sbn[i] > 0, hc, n_hc - 1)
    grid_spec = pltpu.PrefetchScalarGridSpec(
        num_scalar_prefetch=3,
        grid=(n_sb, n_hc),
        in_specs=[pl.BlockSpec((1, 1, MOE_TM), lambda i, hc, sbe, sbs, sbn: (i, 0, 0), memory_space=pltpu.SMEM),
                  pl.BlockSpec((1, 1, MOE_TM), lambda i, hc, sbe, sbs, sbn: (jnp.minimum(i + 1, n_sb - 1), 0, 0),
                               memory_space=pltpu.SMEM),
                  pl.BlockSpec(memory_space=pl.ANY),
                  pl.BlockSpec((1, D, 2 * TH), lambda i, hc, sbe, sbs, sbn: (sbe[i], 0, hc_eff(i, hc, sbn))),
                  pl.BlockSpec((1, 1, 2 * TH), lambda i, hc, sbe, sbs, sbn: (sbe[i], 0, hc_eff(i, hc, sbn))),
                  pl.BlockSpec((1, TH, D), lambda i, hc, sbe, sbs, sbn: (sbe[i], hc_eff(i, hc, sbn), 0)),
                  pl.BlockSpec((1, 1, D), lambda i, hc, sbe, sbs, sbn: (sbe[i], 0, 0))],
        out_specs=pl.BlockSpec(memory_space=pl.ANY),
        scratch_shapes=[pltpu.VMEM((MOE_TM, half), U32), pltpu.VMEM((MOE_TM, 2 * half), BF16),
                        pltpu.VMEM((MOE_TM, D), F32), pltpu.VMEM((2, MOE_SUB, half), U32),
                        pltpu.SMEM((2,), I32),
                        pltpu.SemaphoreType.DMA(()), pltpu.SemaphoreType.DMA((2,))],
    )
    return pl.pallas_call(
        _moe_body,
        out_shape=jax.ShapeDtypeStruct((n_rows, half), U32),
        grid_spec=grid_spec,
        compiler_params=_cparams(("arbitrary", "arbitrary")),
        name="moe",
    )(sb_e, sb_start, sb_n, sb_tok, sb_tok, n2p, w_gate_up, b_gate_up.reshape(E, 1, H2), w_down, b_down.reshape(E, 1, D))


COMBINE_TM = 512


def _combine_body(dest_ref, h_ref, tw_ref, g2_ref, nw_ref, y_hbm, o_ref, ybuf, sem):
    tm = COMBINE_TM
    n = TOP_K * tm

    def copy(j):
        return pltpu.make_async_copy(y_hbm.at[pl.ds(dest_ref[0, 0, j], 1)], ybuf.at[pl.ds(j, 1)], sem)

    def start(g, carry):
        for u in range(8):
            copy(g * 8 + u).start(priority=u % 2)
        return carry

    def wait(j, carry):
        copy(j).wait()
        return carry

    lax.fori_loop(0, n // 8, start, 0)
    lax.fori_loop(0, n, wait, 0, unroll=8)

    tw = tw_ref[...]
    lo = hi = None
    for kk in range(TOP_K):
        a, b = _unpack_bf16_pair(ybuf[kk * tm:(kk + 1) * tm, :])
        wk = tw[:, kk:kk + 1]
        lo = wk * a if lo is None else lo + wk * a
        hi = wk * b if hi is None else hi + wk * b
    ffn = jnp.concatenate([lo, hi], axis=1)
    h = h_ref[...] + g2_ref[0] * ffn
    ms = jnp.mean(h * h, axis=-1, keepdims=True)
    o_ref[...] = h * lax.rsqrt(ms + EPS) * nw_ref[...]


def _combine(h1, y_sorted, dest, top_w, g2, final_norm_w, L):
    T, D = h1.shape
    tm = COMBINE_TM
    tiles = T // tm
    tpb = L // tm
    dest_tiles = dest.reshape(tiles, tm, TOP_K).transpose(0, 2, 1).reshape(tiles, 1, TOP_K * tm)
    return pl.pallas_call(
        _combine_body,
        out_shape=jax.ShapeDtypeStruct((T, D), F32),
        grid=(tiles,),
        in_specs=[pl.BlockSpec((1, 1, TOP_K * tm), lambda i: (i, 0, 0), memory_space=pltpu.SMEM),
                  pl.BlockSpec((tm, D), lambda i: (i, 0)),
                  pl.BlockSpec((tm, LANES), lambda i: (i, 0)),
                  pl.BlockSpec((1, 1, D), lambda i: (i // tpb, 0, 0)),
                  pl.BlockSpec((1, D), lambda i: (0, 0)),
                  pl.BlockSpec(memory_space=pl.ANY)],
        out_specs=pl.BlockSpec((tm, D), lambda i: (i, 0)),
        scratch_shapes=[pltpu.VMEM((TOP_K * tm, D // 2), U32), pltpu.SemaphoreType.DMA(())],
        compiler_params=_cparams(("arbitrary",)),
        name="combine",
    )(dest_tiles, h1, top_w, g2, final_norm_w.reshape(1, D), y_sorted)


def kernel(x, c, w_ada, b_ada, norm1_w, w_in, ssd_conv_w, ssd_conv_b, ssd_dt_bias, ssd_a_log, ssd_d_skip, ssd_norm_w, gla_w_gk2, gla_b_gk2, gla_norm_w, w_ssd_out, w_gla_out, w_out, norm2_w, w_router, b_router, w_gate_up, b_gate_up, w_down, b_down, final_norm_w):
    B, L, D = x.shape
    T = B * L
    assert D == D_MODEL and w_ada.shape[0] == 1
    x2d = x.reshape(T, D)

    mod = _ada(c, w_ada[0], b_ada[0])
    sh1, sc1, g1, sh2, sc2, g2 = [mod[:, i * D:(i + 1) * D].reshape(B, 1, D) for i in range(6)]

    w_t = w_in[0].T
    pad = lambda a: jnp.pad(a, ((0, LANES - a.shape[0]), (0, 0)))
    w_s = jnp.concatenate([pad(w_t[SRC_DT[0]:SRC_DT[1]]), pad(w_t[SRC_GK[0]:SRC_GK[1]])], axis=0)
    proj_a, n1 = _inproj_a(x2d, norm1_w[0].reshape(1, D), sc1, sh1, w_t, L)
    proj_b = _matmul(n1, w_t, SRC_B, PROJ_TN, "inproj_b")
    proj_c = _matmul(n1, w_t, SRC_C, PROJ_TN, "inproj_c")
    proj_s = _matmul(n1, w_s, (0, 2 * LANES), 2 * LANES, "inproj_s")
    pa3, pb3, ps3 = [p.reshape(B, L, p.shape[1]) for p in (proj_a, proj_b, proj_s)]
    y_ssd = _ssd(pa3, ps3, ssd_conv_w[0], ssd_conv_b[0], ssd_dt_bias[0], ssd_a_log[0], ssd_d_skip[0], ssd_norm_w[0])
    y_gla = _gla(pb3, ps3, gla_w_gk2[0], gla_b_gk2[0], gla_norm_w[0])
    merged = _merge(y_ssd.reshape(T, D), y_gla.reshape(T, D), proj_c, w_ssd_out[0].astype(BF16), w_gla_out[0].astype(BF16))
    h1, n2p, top_w, top_i, cnt = _post(merged, x2d, g1, norm2_w[0].reshape(1, D), sc2, sh2, w_out[0].astype(BF16),
                                       w_router[0], b_router[0], L)

    A = T * TOP_K
    n_rows = A + N_EXPERTS * MOE_SUB
    n_sb = n_rows // MOE_TM + N_EXPERTS * (MOE_TM - MOE_SUB) // MOE_TM
    dest, sb_tok, sb_e, sb_start, sb_n = _routing(top_i[:, :TOP_K], top_i[:, TOP_K:2 * TOP_K],
                                                  cnt[0, :N_EXPERTS].astype(I32), n_sb)
    y_sorted = _moe(n2p, n_rows, sb_tok, sb_e, sb_start, sb_n, w_gate_up[0], b_gate_up[0], w_down[0], b_down[0])
    out = _combine(h1, y_sorted, dest, top_w, g2, final_norm_w, L)
    return out.reshape(B, L, D)
```

```python
import functools

import jax
import jax.numpy as jnp
import numpy as np
from jax import lax
from jax.experimental import pallas as pl
from jax.experimental.pallas import tpu as pltpu

F32 = jnp.float32
BF16 = jnp.bfloat16
U32 = jnp.uint32
I32 = jnp.int32

EPS = 1e-6
LANES = 128
VMEM_LIMIT = 56 * 1024 * 1024

D_MODEL = 2048
SSD_HEADS, SSD_P, SSD_G, SSD_N, SSD_CONV, SSD_Q = 32, 64, 4, 128, 4, 256
SSD_INNER = SSD_HEADS * SSD_P
SSD_TAIL = 16
GLA_H, GLA_DK, GLA_DV, GLA_RANK, GLA_C = 4, 256, 512, 16, 64
GLA_GATE_NORM = 16.0
N_EXPERTS, TOP_K = 32, 4
SWIGLU_LIMIT, SWIGLU_ALPHA = 7.0, 1.702

SRC_A = (0, 5120)
SRC_DT = (5120, 5152)
SRC_B = (5152, 11296)
SRC_GK = (11296, 11312)
SRC_C = (11312, 15408)
A_Z, A_XS, A_BC = 0, 2048, 4096
B_Q, B_K, B_V, B_GO = 0, 1024, 2048, 4096
C_GS, C_GG = 0, 2048
PROJ_TN = 1024

MOE_SUB = 256
MOE_TM = 2048
MOE_TH = 256


def _cparams(sem, vmem=VMEM_LIMIT):
    return pltpu.CompilerParams(dimension_semantics=sem, vmem_limit_bytes=vmem)


def _split3(x):
    hi = x.astype(BF16)
    r1 = x - hi.astype(F32)
    mid = r1.astype(BF16)
    lo = (r1 - mid.astype(F32)).astype(BF16)
    return hi, mid, lo


def _dot(a, b):
    return jnp.dot(a, b, preferred_element_type=F32)


def _dot_nt(a, b):
    return lax.dot_general(a, b, (((1,), (1,)), ((), ())), preferred_element_type=F32)


def _dot_tn(a, b):
    return lax.dot_general(a, b, (((0,), (0,)), ((), ())), preferred_element_type=F32)


def _dot_exact_lhs(a01, x):
    hi, mid, lo = _split3(x)
    return _dot(a01, hi) + _dot(a01, mid) + _dot(a01, lo)


def _dot_hilo_rhs(x, b01):
    hi = x.astype(BF16)
    lo = (x - hi.astype(F32)).astype(BF16)
    return _dot(hi, b01) + _dot(lo, b01)


def _silu(x):
    return x * jax.nn.sigmoid(x)


def _softplus(x):
    return jnp.maximum(x, 0.0) + jnp.log1p(jnp.exp(-jnp.abs(x)))


def _ada_body(ct_ref, w_ref, b_ref, o_ref):
    ct = ct_ref[...]
    cs = _silu(ct)
    w = w_ref[...]
    for m in range(ct.shape[1]):
        o_ref[m:m + 1, :] = jnp.sum(w * cs[:, m:m + 1], axis=0, keepdims=True) + b_ref[...]


def _ada(c, w_ada, b_ada):
    B, D = c.shape
    N = w_ada.shape[1]
    tn = 1024
    return pl.pallas_call(
        _ada_body,
        out_shape=jax.ShapeDtypeStruct((B, N), F32),
        grid=(N // tn,),
        in_specs=[pl.BlockSpec((D, B), lambda j: (0, 0)),
                  pl.BlockSpec((D, tn), lambda j: (0, j)),
                  pl.BlockSpec((1, tn), lambda j: (0, j))],
        out_specs=pl.BlockSpec((B, tn), lambda j: (0, j)),
        compiler_params=_cparams(("arbitrary",)),
        name="ada",
    )(c.T, w_ada, b_ada.reshape(1, N))


def _inproj_a_body(x_ref, nw_ref, sc_ref, sh_ref, w_ref, o_ref, n_ref):
    @pl.when(pl.program_id(1) == 0)
    def _():
        x = x_ref[...]
        ms = jnp.mean(x * x, axis=-1, keepdims=True)
        y = x * lax.rsqrt(ms + EPS) * nw_ref[...]
        n_ref[...] = (y * (1.0 + sc_ref[0]) + sh_ref[0]).astype(BF16)

    o_ref[...] = _dot_nt(n_ref[...], w_ref[...].astype(BF16)).astype(BF16)


def _inproj_a(x2d, norm_w, sc, sh, w_t, L):
    T, D = x2d.shape
    tm, tn = 1024, PROJ_TN // 2
    n_cols = SRC_A[1] - SRC_A[0]
    tiles_per_batch = L // tm
    return pl.pallas_call(
        _inproj_a_body,
        out_shape=(jax.ShapeDtypeStruct((T, n_cols), BF16), jax.ShapeDtypeStruct((T, D), BF16)),
        grid=(T // tm, n_cols // tn),
        in_specs=[pl.BlockSpec((tm, D), lambda i, j: (i, 0)),
                  pl.BlockSpec((1, D), lambda i, j: (0, 0)),
                  pl.BlockSpec((1, 1, D), lambda i, j: (i // tiles_per_batch, 0, 0)),
                  pl.BlockSpec((1, 1, D), lambda i, j: (i // tiles_per_batch, 0, 0)),
                  pl.BlockSpec((tn, D), lambda i, j: (SRC_A[0] // tn + j, 0))],
        out_specs=(pl.BlockSpec((tm, tn), lambda i, j: (i, j)), pl.BlockSpec((tm, D), lambda i, j: (i, 0))),
        compiler_params=_cparams(("arbitrary", "arbitrary")),
        name="inproj_a",
    )(x2d, norm_w, sc, sh, w_t)


def _matmul_body(a_ref, w_ref, o_ref):
    o_ref[...] = _dot_nt(a_ref[...], w_ref[...].astype(BF16)).astype(BF16)


def _matmul(a, w_t, rows, tn, name):
    T, D = a.shape
    N = rows[1] - rows[0]
    tm = 2048
    return pl.pallas_call(
        _matmul_body,
        out_shape=jax.ShapeDtypeStruct((T, N), BF16),
        grid=(T // tm, N // tn),
        in_specs=[pl.BlockSpec((tm, D), lambda i, j: (i, 0)),
                  pl.BlockSpec((pl.Element(tn), pl.Element(D)),
                               lambda i, j: ((rows[0] // 8 + j * (tn // 8)) * 8, 0))],
        out_specs=pl.BlockSpec((tm, tn), lambda i, j: (i, j)),
        compiler_params=_cparams(("arbitrary", "arbitrary")),
        name=name,
    )(a, w_t)


def _ssd_body(z_ref, xs_ref, bc_ref, dt_ref, cwx_ref, cwbc_ref, cbx_ref, cbbc_ref, dtb_ref, alog_ref,
              dskip_ref, nw_ref, exp_ref, sh_ref, o_ref, ubx, ubbc, state):
    Q = SSD_Q
    c = pl.program_id(1)

    @pl.when(c == 0)
    def _():
        ubx[...] = jnp.zeros(ubx.shape, F32)
        ubbc[...] = jnp.zeros(ubbc.shape, F32)
        state[...] = jnp.zeros(state.shape, F32)

    def conv_silu(u_ref, tail, cw_ref, cb_ref):
        u = u_ref[0]
        head = u_ref[0, 0:SSD_TAIL, :].astype(F32)[0:8, :]
        prev = tail[...]
        r8 = lax.broadcasted_iota(I32, head.shape, 0)
        w_last = cw_ref[SSD_CONV - 1:SSD_CONV, :]
        acc = cb_ref[...] + w_last * u.astype(F32)
        acc8 = cb_ref[...] + w_last * head
        for s in range(1, SSD_CONV):
            w_s = cw_ref[SSD_CONV - 1 - s:SSD_CONV - s, :]
            acc = acc + w_s * _dot(sh_ref[s - 1], u)
            acc8 = acc8 + w_s * jnp.where(r8 < s, pltpu.roll(prev, s, 0), pltpu.roll(head, s, 0))
        tail[...] = u_ref[0, Q - SSD_TAIL:Q, :].astype(F32)[SSD_TAIL - 8:, :]
        return _silu(jnp.concatenate([acc8, acc[8:, :]], axis=0))

    xs = conv_silu(xs_ref, ubx, cwx_ref, cbx_ref)
    bcm = conv_silu(bc_ref, ubbc, cwbc_ref, cbbc_ref)
    GN = SSD_G * SSD_N
    bm_f, cm = bcm[:, :GN], bcm[:, GN:].astype(BF16)

    dt = _softplus(dt_ref[0].astype(F32) + dtb_ref[...])
    a_neg = -jnp.exp(alog_ref[...])
    dA = dt * a_neg

    row = lax.broadcasted_iota(I32, (Q, Q), 0)
    col = lax.broadcasted_iota(I32, (Q, Q), 1)
    causal = row >= col
    tri = jnp.where(causal, 1.0, 0.0).astype(BF16)
    acs = _dot_exact_lhs(tri, dA)
    acs_t = acs.T
    acs_last = acs[Q - 1:Q, :]
    exp_a = jnp.exp(acs)
    decay_st = jnp.exp(acs_last - acs)

    expand = exp_ref[...]
    dt_e = _dot_hilo_rhs(dt, expand)
    dtd_e = _dot_hilo_rhs(dt * decay_st, expand)
    expa_e = _dot_hilo_rhs(exp_a, expand)

    x_dt = (xs * dt_e).astype(BF16)
    x_dd = (xs * dtd_e).astype(BF16)

    lane = lax.broadcasted_iota(I32, (Q, LANES), 1)
    lo_half = lane < SSD_P
    HG = SSD_HEADS // SSD_G
    GW = HG * SSD_P
    y_groups = []
    for g in range(SSD_G):
        cg = cm[:, g * SSD_N:(g + 1) * SSD_N]
        bg_f = bm_f[:, g * SSD_N:(g + 1) * SSD_N]
        bg = bg_f.astype(BF16)
        scores = _dot_nt(cg, bg)
        pieces = []
        for p in range(HG // 2):
            h0 = g * HG + 2 * p
            xp = x_dt[:, h0 * SSD_P:h0 * SSD_P + LANES]
            acc = None
            for s in range(2):
                h = h0 + s
                seg = acs[:, h:h + 1] - acs_t[h:h + 1, :]
                lmat = jnp.exp(jnp.where(causal, seg, -jnp.inf))
                m = (scores * lmat).astype(BF16)
                xh = jnp.where(lo_half if s == 0 else jnp.logical_not(lo_half), xp, jnp.zeros_like(xp))
                part = _dot(m, xh)
                acc = part if acc is None else acc + part
            pieces.append(acc)
        y_diag = jnp.concatenate(pieces, axis=1)
        st_old = state[g]
        y_off = _dot(cg, st_old.astype(BF16)) * expa_e[:, g * GW:(g + 1) * GW]
        st_new = _dot(bg_f.T.astype(BF16), x_dd[:, g * GW:(g + 1) * GW])
        state[g] = st_old * expa_e[Q - 1:Q, g * GW:(g + 1) * GW] + st_new
        y_groups.append(y_diag + y_off)

    y = jnp.concatenate(y_groups, axis=1) + xs * dskip_ref[...]
    y = y * _silu(z_ref[0].astype(F32))
    outs = []
    for g in range(SSD_G):
        yg = y[:, g * GW:(g + 1) * GW]
        ms = jnp.mean(yg * yg, axis=-1, keepdims=True)
        outs.append(yg * lax.rsqrt(ms + EPS) * nw_ref[:, g * GW:(g + 1) * GW])
    o_ref[0] = jnp.concatenate(outs, axis=1).astype(BF16)


def _ssd(pa3, ps3, conv_w, conv_b, dt_bias, a_log, d_skip, norm_w):
    B, L, _ = pa3.shape
    Q, DI, BCW = SSD_Q, SSD_INNER, 2 * SSD_G * SSD_N
    padl = lambda a: jnp.pad(a.reshape(1, -1), ((0, 0), (0, LANES - a.shape[-1])))
    expand = (np.arange(LANES)[:, None] == (np.arange(DI)[None, :] // SSD_P)).astype(np.float32)
    t = np.arange(Q)[:, None]
    shifts = np.stack([(t - s == np.arange(Q)[None, :]) for s in range(1, SSD_CONV)]).astype(np.float32)
    const = lambda shape: pl.BlockSpec(shape, lambda b, c: (0,) * len(shape))
    return pl.pallas_call(
        _ssd_body,
        out_shape=jax.ShapeDtypeStruct((B, L, DI), BF16),
        grid=(B, L // Q),
        in_specs=[pl.BlockSpec((1, Q, DI), lambda b, c: (b, c, A_Z // DI)),
                  pl.BlockSpec((1, Q, DI), lambda b, c: (b, c, A_XS // DI)),
                  pl.BlockSpec((1, Q, BCW), lambda b, c: (b, c, A_BC // BCW)),
                  pl.BlockSpec((1, Q, LANES), lambda b, c: (b, c, 0)),
                  const((SSD_CONV, DI)), const((SSD_CONV, BCW)), const((1, DI)), const((1, BCW)),
                  const((1, LANES)), const((1, LANES)), const((1, DI)), const((1, DI)),
                  const((LANES, DI)), const((SSD_CONV - 1, Q, Q))],
        out_specs=pl.BlockSpec((1, Q, DI), lambda b, c: (b, c, 0)),
        scratch_shapes=[pltpu.VMEM((8, DI), F32), pltpu.VMEM((8, BCW), F32),
                        pltpu.VMEM((SSD_G, SSD_N, DI // SSD_G), F32)],
        compiler_params=_cparams(("arbitrary", "arbitrary")),
        name="ssd",
    )(pa3, pa3, pa3, ps3, conv_w[:, :DI], conv_w[:, DI:], conv_b[:DI].reshape(1, DI),
      conv_b[DI:].reshape(1, BCW), padl(dt_bias), padl(a_log), jnp.repeat(d_skip, SSD_P).reshape(1, DI),
      norm_w.reshape(1, DI), jnp.asarray(expand, BF16), jnp.asarray(shifts, BF16))


GLA_RB = 256


def _gla_body(q_ref, k_ref, v_ref, g_ref, gk_ref, w_ref, b_ref, nw_ref, o_ref, st):
    RB, C = GLA_RB, GLA_C

    NB = q_ref.shape[0]

    @pl.when(pl.program_id(0) == 0)
    def _():
        st[...] = jnp.zeros(st.shape, F32)

    w_hi, w_mid, w_lo = _split3(w_ref[...])
    row = lax.broadcasted_iota(I32, (RB, RB), 0)
    col = lax.broadcasted_iota(I32, (RB, RB), 1)
    blocktri = jnp.where((row // C == col // C) & (row >= col), 1.0, 0.0).astype(BF16)
    bcs_all = []
    for b in range(NB):
        gk = gk_ref[b]
        pre = _dot(gk, w_hi) + _dot(gk, w_mid) + _dot(gk, w_lo) + b_ref[...]
        log_a = (jnp.minimum(pre, 0.0) - jnp.log1p(jnp.exp(-jnp.abs(pre)))) * (1.0 / GLA_GATE_NORM)
        bcs_all.append(_dot_exact_lhs(blocktri, log_a))

    r64 = lax.broadcasted_iota(I32, (C, C), 0)
    c64 = lax.broadcasted_iota(I32, (C, C), 1)
    causal = r64 >= c64
    scale = GLA_DK ** -0.5

    for s in range(RB // C):
        rs = slice(s * C, (s + 1) * C)
        q_dec, k_inv, k_st, cdec = [], [], [], []
        for b in range(NB):
            bcs = bcs_all[b][rs, :]
            last = bcs[C - 1:C, :]
            qf = q_ref[b, rs, :].astype(F32)
            kf = k_ref[b, rs, :].astype(F32)
            q_dec.append((qf * scale * jnp.exp(bcs)).astype(BF16))
            k_inv.append((kf * jnp.exp(-bcs)).astype(BF16))
            k_st.append((kf * jnp.exp(last - bcs)).astype(BF16))
            cdec.append(jnp.exp(last))
        for h in range(GLA_H):
            ks = slice(h * GLA_DK, (h + 1) * GLA_DK)
            vs = slice(h * GLA_DV, (h + 1) * GLA_DV)
            for b in range(NB):
                vh = v_ref[b, rs, vs]
                attn = jnp.where(causal, _dot_nt(q_dec[b][:, ks], k_inv[b][:, ks]), 0.0)
                st_h = st[b, h]
                o = _dot(attn.astype(BF16), vh) + _dot_nt(q_dec[b][:, ks], st_h.astype(BF16))
                st[b, h] = st_h * cdec[b][:, ks] + _dot_tn(vh, k_st[b][:, ks])
                ms = jnp.mean(o * o, axis=-1, keepdims=True)
                o = o * lax.rsqrt(ms + EPS) * nw_ref[...]
                o_ref[b, rs, vs] = (o * _silu(g_ref[b, rs, vs].astype(F32))).astype(BF16)


def _gla(pb3, ps3, w_gk2, b_gk2, norm_w):
    B, L, _ = pb3.shape
    RB, KD, VD = GLA_RB, GLA_H * GLA_DK, GLA_H * GLA_DV
    w_pad = jnp.pad(w_gk2, ((0, LANES - w_gk2.shape[0]), (0, 0)))
    const = lambda shape: pl.BlockSpec(shape, lambda c: (0,) * len(shape))
    return pl.pallas_call(
        _gla_body,
        out_shape=jax.ShapeDtypeStruct((B, L, VD), BF16),
        grid=(L // RB,),
        in_specs=[pl.BlockSpec((B, RB, KD), lambda c: (0, c, B_Q // KD)),
                  pl.BlockSpec((B, RB, KD), lambda c: (0, c, B_K // KD)),
                  pl.BlockSpec((B, RB, VD), lambda c: (0, c, B_V // VD)),
                  pl.BlockSpec((B, RB, VD), lambda c: (0, c, B_GO // VD)),
                  pl.BlockSpec((B, RB, LANES), lambda c: (0, c, 1)),
                  const((LANES, KD)), const((1, KD)), const((1, GLA_DV))],
        out_specs=pl.BlockSpec((B, RB, VD), lambda c: (0, c, 0)),
        scratch_shapes=[pltpu.VMEM((B, GLA_H, GLA_DV, GLA_DK), F32)],
        compiler_params=_cparams(("arbitrary",)),
        name="gla",
    )(pb3, pb3, pb3, pb3, ps3, w_pad, b_gk2.reshape(1, KD), norm_w.reshape(1, GLA_DV))


def _merge_body(ys_ref, yg_ref, gs_ref, gg_ref, ws_ref, wg_ref, o_ref):
    a = _dot(ys_ref[...], ws_ref[...])
    b = _dot(yg_ref[...], wg_ref[...])
    m = jax.nn.sigmoid(gs_ref[...].astype(F32)) * a + jax.nn.sigmoid(gg_ref[...].astype(F32)) * b
    o_ref[...] = m.astype(BF16)


def _merge(y_ssd, y_gla, proj_c, w_ssd_out, w_gla_out):
    T, D = y_ssd.shape
    tm, tn = 1024, 1024
    return pl.pallas_call(
        _merge_body,
        out_shape=jax.ShapeDtypeStruct((T, D), BF16),
        grid=(D // tn, T // tm),
        in_specs=[pl.BlockSpec((tm, D), lambda j, i: (i, 0)),
                  pl.BlockSpec((tm, D), lambda j, i: (i, 0)),
                  pl.BlockSpec((tm, tn), lambda j, i: (i, C_GS // tn + j)),
                  pl.BlockSpec((tm, tn), lambda j, i: (i, C_GG // tn + j)),
                  pl.BlockSpec((D, tn), lambda j, i: (0, j)),
                  pl.BlockSpec((D, tn), lambda j, i: (0, j))],
        out_specs=pl.BlockSpec((tm, tn), lambda j, i: (i, j)),
        compiler_params=_cparams(("arbitrary", "arbitrary")),
        name="merge",
    )(y_ssd, y_gla, proj_c, proj_c, w_ssd_out, w_gla_out)


def _pack_bf16_pair(a, b):
    ua = lax.bitcast_convert_type(a.astype(BF16).astype(F32), U32)
    ub = lax.bitcast_convert_type(b.astype(BF16).astype(F32), U32)
    return (ua & jnp.uint32(0xFFFF0000)) | (ub >> 16)


def _unpack_bf16_pair(w):
    a = lax.bitcast_convert_type(w & jnp.uint32(0xFFFF0000), F32)
    b = lax.bitcast_convert_type(w << 16, F32)
    return a, b


def _post_body(m_ref, x_ref, g1_ref, nw_ref, sc_ref, sh_ref, wo_ref, wr_ref, br_ref, tri_ref,
               h_ref, n2p_ref, tw_ref, ti_ref, cnt_ref):
    D = x_ref.shape[1]
    h = x_ref[...] + g1_ref[0] * _dot(m_ref[...], wo_ref[...])
    h_ref[...] = h
    ms = jnp.mean(h * h, axis=-1, keepdims=True)
    n2 = h * lax.rsqrt(ms + EPS) * nw_ref[...] * (1.0 + sc_ref[0]) + sh_ref[0]
    n2p_ref[...] = _pack_bf16_pair(n2[:, :D // 2], n2[:, D // 2:])

    n_hi = n2.astype(BF16)
    n_lo = (n2 - n_hi.astype(F32)).astype(BF16)
    wr = wr_ref[...]
    w_hi = wr.astype(BF16)
    w_lo = (wr - w_hi.astype(F32)).astype(BF16)
    logits = _dot(n_hi, w_hi) + _dot(n_hi, w_lo) + _dot(n_lo, w_hi) + br_ref[...]

    lane = lax.broadcasted_iota(I32, logits.shape, 1)
    cur = jnp.where(lane < N_EXPERTS, logits, -jnp.inf)
    vals, idxs = [], []
    for _ in range(TOP_K):
        mx = jnp.max(cur, axis=-1, keepdims=True)
        ix = jnp.min(jnp.where(cur == mx, lane, LANES), axis=-1, keepdims=True)
        vals.append(mx)
        idxs.append(ix)
        cur = jnp.where(lane == ix, -jnp.inf, cur)
    es = [jnp.exp(v - vals[0]) for v in vals]
    denom = es[0] + es[1] + es[2] + es[3]

    @pl.when(pl.program_id(0) == 0)
    def _():
        cnt_ref[...] = jnp.zeros(cnt_ref.shape, F32)

    hits = [lane == ix for ix in idxs]
    onehot = jnp.where(hits[0] | hits[1] | hits[2] | hits[3], 1.0, 0.0)
    before = _dot(tri_ref[...], onehot.astype(BF16)) + cnt_ref[0:1, :]
    cnt_ref[...] = cnt_ref[...] + jnp.sum(onehot, axis=0, keepdims=True)

    tw = jnp.zeros(logits.shape, F32)
    ti = jnp.zeros(logits.shape, I32)
    for kk in range(TOP_K):
        rank = jnp.sum(jnp.where(hits[kk], before, 0.0), axis=-1, keepdims=True).astype(I32)
        tw = jnp.where(lane == kk, es[kk] / denom, tw)
        ti = jnp.where(lane == kk, idxs[kk], jnp.where(lane == TOP_K + kk, rank, ti))
    tw_ref[...] = tw
    ti_ref[...] = ti


def _post(merged, x2d, g1, norm_w, sc, sh, w_out, w_router, b_router, L):
    T, D = x2d.shape
    tm = 512
    tpb = L // tm
    wr = jnp.pad(w_router, ((0, 0), (0, LANES - w_router.shape[1])))
    br = jnp.pad(b_router.reshape(1, -1), ((0, 0), (0, LANES - b_router.shape[0])))
    row = lambda w: pl.BlockSpec((tm, w), lambda i: (i, 0))
    per_b = pl.BlockSpec((1, 1, D), lambda i: (i // tpb, 0, 0))
    const = lambda shape: pl.BlockSpec(shape, lambda i: (0,) * len(shape))
    tri = np.tril(np.ones((tm, tm), np.float32), -1)
    return pl.pallas_call(
        _post_body,
        out_shape=(jax.ShapeDtypeStruct((T, D), F32), jax.ShapeDtypeStruct((T, D // 2), U32),
                   jax.ShapeDtypeStruct((T, LANES), F32), jax.ShapeDtypeStruct((T, LANES), I32),
                   jax.ShapeDtypeStruct((8, LANES), F32)),
        grid=(T // tm,),
        in_specs=[row(D), row(D), per_b, const((1, D)), per_b, per_b, const((D, D)), const((D, LANES)),
                  const((1, LANES)), const((tm, tm))],
        out_specs=(row(D), row(D // 2), row(LANES), row(LANES), const((8, LANES))),
        compiler_params=_cparams(("arbitrary",)),
        name="post",
    )(merged, x2d, g1, norm_w, sc, sh, w_out, wr, br, jnp.asarray(tri, BF16))


def _merge_post_body(ys_ref, yg_ref, gs_ref, gg_ref, ws_ref, wg_ref, x_ref, g1_ref, nw_ref, sc_ref, sh_ref, wo_ref,
                     wr_ref, br_ref, tri_ref, h_ref, n2p_ref, tw_ref, ti_ref, cnt_ref, m_scr):
    _merge_body(ys_ref, yg_ref, gs_ref, gg_ref, ws_ref, wg_ref, m_scr)
    _post_body(m_scr, x_ref, g1_ref, nw_ref, sc_ref, sh_ref, wo_ref, wr_ref, br_ref, tri_ref,
               h_ref, n2p_ref, tw_ref, ti_ref, cnt_ref)


def _merge_post(y_ssd, y_gla, proj_c, w_ssd_out, w_gla_out, x2d, g1, norm_w, sc, sh, w_out, w_router, b_router, L):
    T, D = x2d.shape
    tm = 256
    tpb = L // tm
    wr = jnp.pad(w_router, ((0, 0), (0, LANES - w_router.shape[1])))
    br = jnp.pad(b_router.reshape(1, -1), ((0, 0), (0, LANES - b_router.shape[0])))
    row = lambda w: pl.BlockSpec((tm, w), lambda i: (i, 0))
    per_b = pl.BlockSpec((1, 1, D), lambda i: (i // tpb, 0, 0))
    const = lambda shape: pl.BlockSpec(shape, lambda i: (0,) * len(shape))
    resident = pl.BlockSpec((D, D), lambda i: (0, 0), pipeline_mode=pl.Buffered(1))
    tri = np.tril(np.ones((tm, tm), np.float32), -1)
    return pl.pallas_call(
        _merge_post_body,
        out_shape=(jax.ShapeDtypeStruct((T, D), F32), jax.ShapeDtypeStruct((T, D // 2), U32),
                   jax.ShapeDtypeStruct((T, LANES), F32), jax.ShapeDtypeStruct((T, LANES), I32),
                   jax.ShapeDtypeStruct((8, LANES), F32)),
        grid=(T // tm,),
        in_specs=[row(D), row(D),
                  pl.BlockSpec((tm, D), lambda i: (i, C_GS // D)), pl.BlockSpec((tm, D), lambda i: (i, C_GG // D)),
                  resident, resident, row(D), per_b, const((1, D)), per_b, per_b, resident, const((D, LANES)),
                  const((1, LANES)), const((tm, tm))],
        out_specs=(row(D), row(D // 2), row(LANES), row(LANES), const((8, LANES))),
        scratch_shapes=[pltpu.VMEM((tm, D), BF16)],
        compiler_params=_cparams(("arbitrary",)),
        name="merge_post",
    )(y_ssd, y_gla, proj_c, proj_c, w_ssd_out, w_gla_out, x2d, g1, norm_w, sc, sh, w_out, wr, br,
      jnp.asarray(tri, BF16))


def _routing(top_idx, top_rank, counts, n_sb):
    T = top_idx.shape[0]
    A = T * TOP_K
    flat_e = top_idx.reshape(A)
    rank = top_rank.reshape(A)
    padded = (counts + MOE_SUB - 1) // MOE_SUB * MOE_SUB
    pend = jnp.cumsum(padded)
    pstart = pend - padded
    dest = (pstart[flat_e] + rank).astype(I32)

    nsb = (padded + MOE_TM - 1) // MOE_TM
    sb_end = jnp.cumsum(nsb)
    sb_first = sb_end - nsb
    slot = (sb_first[flat_e] + rank // MOE_TM) * MOE_TM + rank % MOE_TM
    sb_tok = jnp.zeros((n_sb * MOE_TM,), I32).at[slot].set(jnp.arange(A, dtype=I32) // TOP_K,
                                                           unique_indices=True)

    total = sb_end[-1]
    i = jnp.arange(n_sb, dtype=I32)
    e_of = jnp.minimum(jnp.searchsorted(sb_end, i, side="right"), N_EXPERTS - 1).astype(I32)
    local = i - sb_first[e_of]
    valid = i < total
    last_e = e_of[jnp.maximum(total - 1, 0)]
    sb_e = jnp.where(valid, e_of, last_e).astype(I32)
    sb_start = jnp.where(valid, pstart[e_of] + local * MOE_TM, 0).astype(I32)
    sb_start = jnp.concatenate([sb_start, pend[-1:].astype(I32)])
    sb_n = jnp.where(valid, jnp.clip(padded[e_of] - local * MOE_TM, 0, MOE_TM), 0).astype(I32)
    return dest, sb_tok.reshape(n_sb, 1, MOE_TM), sb_e, sb_start, sb_n


def _deinterleave(gu):
    rows, two_w = gu.shape
    lane = lax.broadcasted_iota(I32, (rows, LANES), 1)
    idx_e = (2 * lane) % LANES
    idx_o = idx_e + 1
    first = lane < LANES // 2
    gates, ups = [], []
    for p in range(two_w // (2 * LANES)):
        a = gu[:, (2 * p) * LANES:(2 * p + 1) * LANES]
        b = gu[:, (2 * p + 1) * LANES:(2 * p + 2) * LANES]
        gates.append(jnp.where(first, jnp.take_along_axis(a, idx_e, axis=1), jnp.take_along_axis(b, idx_e, axis=1)))
        ups.append(jnp.where(first, jnp.take_along_axis(a, idx_o, axis=1), jnp.take_along_axis(b, idx_o, axis=1)))
    return jnp.concatenate(gates, axis=1), jnp.concatenate(ups, axis=1)


MOE_DMA_UNROLL = 16


def _moe_body(sbe_ref, sbs_ref, sbn_ref, tok_ref, tokn_ref, x_hbm, wgu_ref, bgu_ref, wd_ref, bd_ref,
              y_hbm, xs_buf, xb_buf, acc, ystage, pend, sem_in, sem_out):
    i = pl.program_id(0)
    hc = pl.program_id(1)
    n_sb = pl.num_programs(0)
    n_hc = pl.num_programs(1)
    SUB = MOE_SUB
    half = xs_buf.shape[1]
    nblk = sbn_ref[i] // SUB
    start_row = sbs_ref[i]

    def row_copy(t_ref, g, u):
        base = pl.multiple_of(g * MOE_DMA_UNROLL, MOE_DMA_UNROLL)
        return pltpu.make_async_copy(x_hbm.at[pl.ds(t_ref[0, 0, base + u], 1)], xs_buf.at[pl.ds(base + u, 1)], sem_in)

    def start_in(t_ref, n_rows):
        def body(g, carry):
            for u in range(MOE_DMA_UNROLL):
                row_copy(t_ref, g, u).start(priority=u % 2)
            return carry

        lax.fori_loop(0, n_rows // MOE_DMA_UNROLL, body, 0)

    @pl.when(hc == 0)
    def _():
        @pl.when(i == 0)
        def _():
            pend[0] = 0
            pend[1] = 0
            start_in(tok_ref, sbn_ref[0])

        def wait_in(g, carry):
            for u in range(MOE_DMA_UNROLL):
                row_copy(tok_ref, g, u).wait()
            return carry

        lax.fori_loop(0, sbn_ref[i] // MOE_DMA_UNROLL, wait_in, 0)

        def convert(r, carry):
            rows = pl.ds(pl.multiple_of(r * SUB, SUB), SUB)
            a, b = _unpack_bf16_pair(xs_buf[rows, :])
            xb_buf[rows, 0:half] = a.astype(BF16)
            xb_buf[rows, half:2 * half] = b.astype(BF16)
            return carry

        lax.fori_loop(0, nblk, convert, 0)

        @pl.when(i + 1 < n_sb)
        def _():
            start_in(tokn_ref, sbn_ref[i + 1])

    def accumulate(blocks, first):
        rows = [pl.ds(pl.multiple_of(r * SUB, SUB), SUB) for r in blocks]
        wg = wgu_ref[0].astype(BF16)
        wd = wd_ref[0].astype(BF16)
        gus = [_dot(xb_buf[rw, :], wg) + bgu_ref[0] for rw in rows]
        hs = []
        for gu in gus:
            gate, up = _deinterleave(gu)
            gate = jnp.minimum(gate, SWIGLU_LIMIT)
            up = jnp.clip(up, -SWIGLU_LIMIT, SWIGLU_LIMIT)
            hs.append((gate * jax.nn.sigmoid(gate * SWIGLU_ALPHA) * (up + 1.0)).astype(BF16))
        for rw, hidden in zip(rows, hs):
            v = _dot(hidden, wd)
            if first:
                acc[rw, :] = v + bd_ref[0]
            else:
                acc[rw, :] += v

    def run(first):
        def quad(p, carry):
            accumulate([4 * p, 4 * p + 1, 4 * p + 2, 4 * p + 3], first)
            return carry

        lax.fori_loop(0, nblk // 4, quad, 0)
        done = nblk // 4 * 4

        @pl.when(nblk - done >= 2)
        def _():
            accumulate([done, done + 1], first)

        @pl.when(nblk % 2 == 1)
        def _():
            accumulate([nblk - 1], first)

    @pl.when(nblk > 0)
    def _():
        @pl.when(hc == 0)
        def _():
            run(True)

        @pl.when(hc != 0)
        def _():
            run(False)

    @pl.when(hc == n_hc - 1)
    def _():
        def out_copy(r, slot):
            dst = pl.multiple_of(start_row + r * SUB, SUB)
            return pltpu.make_async_copy(ystage.at[slot], y_hbm.at[pl.ds(dst, SUB)], sem_out.at[slot])

        def body(r, carry):
            slot = r % 2

            @pl.when(pend[slot] == 1)
            def _():
                out_copy(r, slot).wait()

            rows = pl.ds(pl.multiple_of(r * SUB, SUB), SUB)
            v = acc[rows, :]
            ystage[slot] = _pack_bf16_pair(v[:, :half], v[:, half:])
            out_copy(r, slot).start()
            pend[slot] = 1
            return carry

        lax.fori_loop(0, nblk, body, 0)

    @pl.when((i == n_sb - 1) & (hc == n_hc - 1))
    def _():
        for slot in range(2):
            @pl.when(pend[slot] == 1)
            def _():
                pltpu.make_async_copy(ystage.at[slot], y_hbm.at[pl.ds(0, SUB)], sem_out.at[slot]).wait()
                pend[slot] = 0

        ystage[0] = jnp.zeros(ystage.shape[1:], U32)

        def tail_copy(b):
            return pltpu.make_async_copy(ystage.at[0], y_hbm.at[pl.ds(pl.multiple_of(b * SUB, SUB), SUB)], sem_out.at[0])

        def start(b, carry):
            tail_copy(b).start()
            return carry

        def wait(b, carry):
            tail_copy(b).wait()
            return carry

        first, stop = sbs_ref[n_sb] // SUB, y_hbm.shape[0] // SUB
        lax.fori_loop(first, stop, start, 0)
        lax.fori_loop(first, stop, wait, 0)


def _moe(n2p, n_rows, sb_tok, sb_e, sb_start, sb_n, w_gate_up, b_gate_up, w_down, b_down):
    half = n2p.shape[1]
    E, D, H2 = w_gate_up.shape
    n_sb = sb_e.shape[0]
    TH = MOE_TH
    n_hc = (H2 // 2) // TH
    hc_eff = lambda i, hc, sbn: jnp.where(sbn[i] > 0, hc, n_hc - 1)
    grid_spec = pltpu.PrefetchScalarGridSpec(
        num_scalar_prefetch=3,
        grid=(n_sb, n_hc),
        in_specs=[pl.BlockSpec((1, 1, MOE_TM), lambda i, hc, sbe, sbs, sbn: (i, 0, 0), memory_space=pltpu.SMEM),
                  pl.BlockSpec((1, 1, MOE_TM), lambda i, hc, sbe, sbs, sbn: (jnp.minimum(i + 1, n_sb - 1), 0, 0),
                               memory_space=pltpu.SMEM),
                  pl.BlockSpec(memory_space=pl.ANY),
                  pl.BlockSpec((1, D, 2 * TH), lambda i, hc, sbe, sbs, sbn: (sbe[i], 0, hc_eff(i, hc, sbn))),
                  pl.BlockSpec((1, 1, 2 * TH), lambda i, hc, sbe, sbs, sbn: (sbe[i], 0, hc_eff(i, hc, sbn))),
                  pl.BlockSpec((1, TH, D), lambda i, hc, sbe, sbs, sbn: (sbe[i], hc_eff(i, hc, sbn), 0)),
                  pl.BlockSpec((1, 1, D), lambda i, hc, sbe, sbs, sbn: (sbe[i], 0, 0))],
        out_specs=pl.BlockSpec(memory_space=pl.ANY),
        scratch_shapes=[pltpu.VMEM((MOE_TM, half), U32), pltpu.VMEM((MOE_TM, 2 * half), BF16),
                        pltpu.VMEM((MOE_TM, D), F32), pltpu.VMEM((2, MOE_SUB, half), U32),
                        pltpu.SMEM((2,), I32),
                        pltpu.SemaphoreType.DMA(()), pltpu.SemaphoreType.DMA((2,))],
    )
    return pl.pallas_call(
        _moe_body,
        out_shape=jax.ShapeDtypeStruct((n_rows, half), U32),
        grid_spec=grid_spec,
        compiler_params=_cparams(("arbitrary", "arbitrary")),
        name="moe",
    )(sb_e, sb_start, sb_n, sb_tok, sb_tok, n2p, w_gate_up, b_gate_up.reshape(E, 1, H2), w_down, b_down.reshape(E, 1, D))


COMBINE_TM = 512


def _combine_body(dest_ref, h_ref, tw_ref, g2_ref, nw_ref, y_hbm, o_ref, ybuf, sem):
    tm = COMBINE_TM
    n = TOP_K * tm

    def copy(j):
        return pltpu.make_async_copy(y_hbm.at[pl.ds(dest_ref[0, 0, j], 1)], ybuf.at[pl.ds(j, 1)], sem)

    def start(g, carry):
        for u in range(8):
            copy(g * 8 + u).start(priority=u % 2)
        return carry

    def wait(j, carry):
        copy(j).wait()
        return carry

    lax.fori_loop(0, n // 8, start, 0)
    lax.fori_loop(0, n, wait, 0, unroll=8)

    tw = tw_ref[...]
    lo = hi = None
    for kk in range(TOP_K):
        a, b = _unpack_bf16_pair(ybuf[kk * tm:(kk + 1) * tm, :])
        wk = tw[:, kk:kk + 1]
        lo = wk * a if lo is None else lo + wk * a
        hi = wk * b if hi is None else hi + wk * b
    ffn = jnp.concatenate([lo, hi], axis=1)
    h = h_ref[...] + g2_ref[0] * ffn
    ms = jnp.mean(h * h, axis=-1, keepdims=True)
    o_ref[...] = h * lax.rsqrt(ms + EPS) * nw_ref[...]


def _combine(h1, y_sorted, dest, top_w, g2, final_norm_w, L):
    T, D = h1.shape
    tm = COMBINE_TM
    tiles = T // tm
    tpb = L // tm
    dest_tiles = dest.reshape(tiles, tm, TOP_K).transpose(0, 2, 1).reshape(tiles, 1, TOP_K * tm)
    return pl.pallas_call(
        _combine_body,
        out_shape=jax.ShapeDtypeStruct((T, D), F32),
        grid=(tiles,),
        in_specs=[pl.BlockSpec((1, 1, TOP_K * tm), lambda i: (i, 0, 0), memory_space=pltpu.SMEM),
                  pl.BlockSpec((tm, D), lambda i: (i, 0)),
                  pl.BlockSpec((tm, LANES), lambda i: (i, 0)),
                  pl.BlockSpec((1, 1, D), lambda i: (i // tpb, 0, 0)),
                  pl.BlockSpec((1, D), lambda i: (0, 0)),
                  pl.BlockSpec(memory_space=pl.ANY)],
        out_specs=pl.BlockSpec((tm, D), lambda i: (i, 0)),
        scratch_shapes=[pltpu.VMEM((TOP_K * tm, D // 2), U32), pltpu.SemaphoreType.DMA(())],
        compiler_params=_cparams(("arbitrary",)),
        name="combine",
    )(dest_tiles, h1, top_w, g2, final_norm_w.reshape(1, D), y_sorted)


def kernel(x, c, w_ada, b_ada, norm1_w, w_in, ssd_conv_w, ssd_conv_b, ssd_dt_bias, ssd_a_log, ssd_d_skip, ssd_norm_w, gla_w_gk2, gla_b_gk2, gla_norm_w, w_ssd_out, w_gla_out, w_out, norm2_w, w_router, b_router, w_gate_up, b_gate_up, w_down, b_down, final_norm_w):
    B, L, D = x.shape
    T = B * L
    assert D == D_MODEL and w_ada.shape[0] == 1
    x2d = x.reshape(T, D)

    mod = _ada(c, w_ada[0], b_ada[0])
    sh1, sc1, g1, sh2, sc2, g2 = [mod[:, i * D:(i + 1) * D].reshape(B, 1, D) for i in range(6)]

    w_t = w_in[0].T
    pad = lambda a: jnp.pad(a, ((0, LANES - a.shape[0]), (0, 0)))
    w_s = jnp.concatenate([pad(w_t[SRC_DT[0]:SRC_DT[1]]), pad(w_t[SRC_GK[0]:SRC_GK[1]])], axis=0)
    proj_a, n1 = _inproj_a(x2d, norm1_w[0].reshape(1, D), sc1, sh1, w_t, L)
    proj_b = _matmul(n1, w_t, SRC_B, PROJ_TN, "inproj_b")
    proj_c = _matmul(n1, w_t, SRC_C, PROJ_TN, "inproj_c")
    proj_s = _matmul(n1, w_s, (0, 2 * LANES), 2 * LANES, "inproj_s")
    pa3, pb3, ps3 = [p.reshape(B, L, p.shape[1]) for p in (proj_a, proj_b, proj_s)]
    y_ssd = _ssd(pa3, ps3, ssd_conv_w[0], ssd_conv_b[0], ssd_dt_bias[0], ssd_a_log[0], ssd_d_skip[0], ssd_norm_w[0])
    y_gla = _gla(pb3, ps3, gla_w_gk2[0], gla_b_gk2[0], gla_norm_w[0])
    h1, n2p, top_w, top_i, cnt = _merge_post(y_ssd.reshape(T, D), y_gla.reshape(T, D), proj_c, w_ssd_out[0].astype(BF16),
                                             w_gla_out[0].astype(BF16), x2d, g1, norm2_w[0].reshape(1, D), sc2, sh2,
                                             w_out[0].astype(BF16), w_router[0], b_router[0], L)

    A = T * TOP_K
    n_rows = A + N_EXPERTS * MOE_SUB
    n_sb = n_rows // MOE_TM + N_EXPERTS * (MOE_TM - MOE_SUB) // MOE_TM
    dest, sb_tok, sb_e, sb_start, sb_n = _routing(top_i[:, :TOP_K], top_i[:, TOP_K:2 * TOP_K],
                                                  cnt[0, :N_EXPERTS].astype(I32), n_sb)
    y_sorted = _moe(n2p, n_rows, sb_tok, sb_e, sb_start, sb_n, w_gate_up[0], b_gate_up[0], w_down[0], b_down[0])
    out = _combine(h1, y_sorted, dest, top_w, g2, final_norm_w, L)
    return out.reshape(B, L, D)
```
